```python
import jax, jax.numpy as jnp
from jax import lax
import numpy as np

D_MODEL = 1024
BATCH = 4
SEQ = 4096
DEPTH = 4
DEC_BATCH = 128
DEC_SEQ = 8
PAST_LEN = 8192
PAGE_SIZE = 128

HEAD_DIM = 64
N_HEADS = 8
N_KV_HEADS = 2
GQA_GROUP = N_HEADS // N_KV_HEADS
D_ATTN = N_HEADS * HEAD_DIM
D_KV = N_KV_HEADS * HEAD_DIM
WINDOW = 128
BLOCK = 128
D_CONV = D_MODEL // 2
CONV_WIDTH = 31
POOL_WINDOWS = (2, 4, 8, 16)
N_POOL_GROUPS = len(POOL_WINDOWS)
POOL_GROUP_DIM = D_MODEL // N_POOL_GROUPS
POOL_BUF = max(POOL_WINDOWS) - 1
D_FF = -(-(8 * D_MODEL) // (3 * 256)) * 256
N_EVEN = (DEPTH + 1) // 2
N_ODD = DEPTH // 2
D_IN_EVEN = 2 * D_CONV + D_ATTN + 2 * D_KV
RMS_EPS = 1e-6
LN_EPS = 1e-5
NEG_INF = -1e30

kernel_name = 'hybrid_conv_swa_pool_decoder_step'


def _rmsnorm(x, g):
    xf = x.astype(jnp.float32)
    y = xf * lax.rsqrt(jnp.mean(xf * xf, axis=-1, keepdims=True) + RMS_EPS)
    return (y * g.astype(jnp.float32)).astype(x.dtype)


def _layernorm(x, g, b):
    xf = x.astype(jnp.float32)
    mu = jnp.mean(xf, axis=-1, keepdims=True)
    var = jnp.mean(jnp.square(xf - mu), axis=-1, keepdims=True)
    y = (xf - mu) * lax.rsqrt(var + LN_EPS)
    return (y * g.astype(jnp.float32) + b.astype(jnp.float32)).astype(x.dtype)


def _alibi_slopes():
    return jnp.asarray(np.array([2.0 ** (-8.0 * (h + 1) / N_HEADS) for h in range(N_HEADS)], dtype=np.float32))


def _sink_attend(q, k, v, dist, valid, sinks):
    s = jnp.einsum('...qkgd,...skd->...kgqs', q, k).astype(jnp.float32) * (HEAD_DIM ** -0.5)
    slopes = _alibi_slopes().reshape(N_KV_HEADS, GQA_GROUP, 1, 1)
    s = s - slopes * dist.astype(jnp.float32)[..., None, None, :, :]
    s = jnp.where(valid[..., None, None, :, :], s, NEG_INF)
    sink = sinks.astype(jnp.float32).reshape(N_KV_HEADS, GQA_GROUP, 1, 1)
    m = jnp.maximum(jnp.max(s, axis=-1, keepdims=True), sink)
    p = jnp.exp(s - m)
    denom = jnp.sum(p, axis=-1, keepdims=True) + jnp.exp(sink - m)
    probs = (p / denom).astype(v.dtype)
    return jnp.einsum('...kgqs,...skd->...qkgd', probs, v)


def _swa_prompt(q, k, v, sinks):
    n, t = q.shape[0], q.shape[1]
    nb = t // BLOCK
    qb = q.reshape(n, nb, BLOCK, N_KV_HEADS, GQA_GROUP, HEAD_DIM)
    kb = k.reshape(n, nb, BLOCK, N_KV_HEADS, HEAD_DIM)
    vb = v.reshape(n, nb, BLOCK, N_KV_HEADS, HEAD_DIM)
    pad = ((0, 0), (1, 0), (0, 0), (0, 0), (0, 0))
    kk = jnp.concatenate([jnp.pad(kb, pad)[:, :-1], kb], axis=2)
    vv = jnp.concatenate([jnp.pad(vb, pad)[:, :-1], vb], axis=2)
    i = jnp.arange(BLOCK)[:, None]
    j = jnp.arange(2 * BLOCK)[None, :]
    dist = BLOCK + i - j
    blk = jnp.arange(nb)[:, None, None]
    valid = (dist >= 0) & (dist < WINDOW) & ((blk > 0) | (j >= BLOCK))
    out = _sink_attend(qb, kk, vv, dist, valid, sinks)
    keep = min(WINDOW, t)
    return out.reshape(n, t, D_ATTN), k[:, -keep:], v[:, -keep:]


def _swa_sample(q, k, v, k_buf, v_buf, sinks):
    n, t = q.shape[0], q.shape[1]
    buf_len = k_buf.shape[1]
    kk = jnp.concatenate([k_buf.astype(k.dtype), k], axis=1)
    vv = jnp.concatenate([v_buf.astype(v.dtype), v], axis=1)
    dist = buf_len + jnp.arange(t)[:, None] - jnp.arange(buf_len + t)[None, :]
    valid = (dist >= 0) & (dist < WINDOW)
    out = _sink_attend(q, kk, vv, dist, valid, sinks)
    return out.reshape(n, t, D_ATTN), kk[:, -buf_len:], vv[:, -buf_len:]


def _conformer_conv(u, buf, w_dw, b_dw, g, b):
    a, gate = jnp.split(u, 2, axis=-1)
    glu = a * jax.nn.sigmoid(gate)
    ext = jnp.concatenate([buf.astype(glu.dtype), glu], axis=1)
    y = lax.conv_general_dilated(ext, w_dw.astype(ext.dtype)[:, None, :], window_strides=(1,), padding='VALID',
                                 dimension_numbers=('NWC', 'WIO', 'NWC'), feature_group_count=D_CONV)
    y = y + b_dw.astype(y.dtype)
    y = jax.nn.silu(_layernorm(y, g, b))
    return y, ext[:, -(CONV_WIDTH - 1):]


def _even_layer(x, conv_buf, k_buf, v_buf, norm_g, w_in, q_norm, k_norm, sinks, w_dw, b_dw, cn_g, cn_b, w_out):
    n, t, _ = x.shape
    xn = _rmsnorm(x, norm_g)
    u = xn @ w_in
    o_q = 2 * D_CONV
    o_k = o_q + D_ATTN
    o_v = o_k + D_KV
    q = _rmsnorm(u[..., o_q:o_k].reshape(n, t, N_KV_HEADS, GQA_GROUP, HEAD_DIM), q_norm)
    k = _rmsnorm(u[..., o_k:o_v].reshape(n, t, N_KV_HEADS, HEAD_DIM), k_norm)
    v = u[..., o_v:].reshape(n, t, N_KV_HEADS, HEAD_DIM)
    conv_out, new_conv = _conformer_conv(u[..., :o_q], conv_buf, w_dw, b_dw, cn_g, cn_b)
    if k_buf is None:
        attn, new_k, new_v = _swa_prompt(q, k, v, sinks)
    else:
        attn, new_k, new_v = _swa_sample(q, k, v, k_buf, v_buf, sinks)
    y = jnp.concatenate([conv_out, attn.astype(conv_out.dtype)], axis=-1) @ w_out
    return x + y, new_conv, new_k, new_v


def _pool_mixer(xn, buf, pos0, w_pool, pool_scale):
    n, t, _ = xn.shape
    ext = jnp.concatenate([buf.astype(xn.dtype), xn], axis=1)
    cs = jnp.pad(jnp.cumsum(ext.astype(jnp.float32), axis=1), ((0, 0), (1, 0), (0, 0)))
    pos = pos0 + jnp.arange(t)
    xf = xn.astype(jnp.float32)
    groups = []
    for g, w in enumerate(POOL_WINDOWS):
        c0, c1 = g * POOL_GROUP_DIM, (g + 1) * POOL_GROUP_DIM
        hi = cs[:, POOL_BUF + 1:POOL_BUF + 1 + t, c0:c1]
        lo = cs[:, POOL_BUF + 1 - w:POOL_BUF + 1 - w + t, c0:c1]
        cnt = jnp.minimum(w, pos + 1).astype(jnp.float32)[:, None]
        groups.append((hi - lo) / cnt - xf[..., c0:c1])
    d = jnp.stack(groups, axis=2).astype(xn.dtype)
    y = jnp.einsum('btgc,gcd->btgd', d, w_pool).reshape(n, t, D_MODEL)
    return y * pool_scale.astype(y.dtype), ext[:, -POOL_BUF:]


def _odd_layer(x, pool_buf, pos0, norm_g, w_pool, pool_scale):
    y, new_buf = _pool_mixer(_rmsnorm(x, norm_g), pool_buf, pos0, w_pool, pool_scale)
    return x + y, new_buf


def _swiglu_ffn(x, g, w_gate, w_up, w_down):
    h = _rmsnorm(x, g)
    return x + (jax.nn.silu(h @ w_gate) * (h @ w_up)) @ w_down


def _trunk(x, conv_bufs, k_bufs, v_bufs, pool_bufs, pos0, norm_mix, norm_ffn, w_in, q_norm, k_norm, sinks,
           w_dw, b_dw, conv_norm_g, conv_norm_b, w_out, w_pool, pool_scale, w_gate, w_up, w_down):
    new_conv, new_k, new_v, new_pool = [], [], [], []
    for layer in range(DEPTH):
        i = layer // 2
        if layer % 2 == 0:
            kb = None if k_bufs is None else k_bufs[i]
            vb = None if v_bufs is None else v_bufs[i]
            x, c, kk, vv = _even_layer(x, conv_bufs[i], kb, vb, norm_mix[layer], w_in[i], q_norm[i], k_norm[i],
                                       sinks[i], w_dw[i], b_dw[i], conv_norm_g[i], conv_norm_b[i], w_out[i])
            new_conv.append(c)
            new_k.append(kk)
            new_v.append(vv)
        else:
            x, pb = _odd_layer(x, pool_bufs[i], pos0, norm_mix[layer], w_pool[i], pool_scale[i])
            new_pool.append(pb)
        x = _swiglu_ffn(x, norm_ffn[layer], w_gate[layer], w_up[layer], w_down[layer])
    return x, jnp.stack(new_conv), jnp.stack(new_k), jnp.stack(new_v), jnp.stack(new_pool)


def setup_inputs(seed: int = 0) -> dict:
    key = jax.random.key(seed)
    ks = jax.random.split(key, 24)
    f32 = jnp.float32

    def nrm(k, shape, scale):
        return jax.random.normal(k, shape, f32) * scale

    win_buf = min(WINDOW, PAST_LEN)
    return {
        'x_prompt': nrm(ks[0], (BATCH, SEQ, D_MODEL), 1.0),
        'x_sample': nrm(ks[1], (DEC_BATCH, DEC_SEQ, D_MODEL), 1.0),
        'cache_conv': nrm(ks[2], (N_EVEN, DEC_BATCH, CONV_WIDTH - 1, D_CONV), 0.5),
        'cache_k': nrm(ks[3], (N_EVEN, DEC_BATCH, win_buf, N_KV_HEADS, HEAD_DIM), 1.0),
        'cache_v': nrm(ks[4], (N_EVEN, DEC_BATCH, win_buf, N_KV_HEADS, HEAD_DIM), 1.0),
        'state_pool': nrm(ks[5], (N_ODD, DEC_BATCH, POOL_BUF, D_MODEL), 1.0),
        'norm_mix': 1.0 + nrm(ks[6], (DEPTH, D_MODEL), 0.02),
        'norm_ffn': 1.0 + nrm(ks[7], (DEPTH, D_MODEL), 0.02),
        'w_in': nrm(ks[8], (N_EVEN, D_MODEL, D_IN_EVEN), D_MODEL ** -0.5),
        'q_norm': 1.0 + nrm(ks[9], (N_EVEN, HEAD_DIM), 0.02),
        'k_norm': 1.0 + nrm(ks[10], (N_EVEN, HEAD_DIM), 0.02),
        'sinks': nrm(ks[11], (N_EVEN, N_HEADS), 0.5),
        'w_dw': nrm(ks[12], (N_EVEN, CONV_WIDTH, D_CONV), CONV_WIDTH ** -0.5),
        'b_dw': nrm(ks[13], (N_EVEN, D_CONV), 0.02),
        'conv_norm_g': 1.0 + nrm(ks[14], (N_EVEN, D_CONV), 0.02),
        'conv_norm_b': nrm(ks[15], (N_EVEN, D_CONV), 0.02),
        'w_out': nrm(ks[16], (N_EVEN, D_CONV + D_ATTN, D_MODEL), (D_CONV + D_ATTN) ** -0.5),
        'w_pool': nrm(ks[17], (N_ODD, N_POOL_GROUPS, POOL_GROUP_DIM, POOL_GROUP_DIM), POOL_GROUP_DIM ** -0.5),
        'pool_scale': 1.0 + nrm(ks[18], (N_ODD, D_MODEL), 0.02),
        'w_gate': nrm(ks[19], (DEPTH, D_MODEL, D_FF), D_MODEL ** -0.5),
        'w_up': nrm(ks[20], (DEPTH, D_MODEL, D_FF), D_MODEL ** -0.5),
        'w_down': nrm(ks[21], (DEPTH, D_FF, D_MODEL), D_FF ** -0.5),
    }


def reference(x_prompt, x_sample, cache_conv, cache_k, cache_v, state_pool, norm_mix, norm_ffn, w_in, q_norm,
              k_norm, sinks, w_dw, b_dw, conv_norm_g, conv_norm_b, w_out, w_pool, pool_scale, w_gate, w_up, w_down):
    n_p = x_prompt.shape[0]
    zero_conv = jnp.zeros((n_p, CONV_WIDTH - 1, D_CONV), x_prompt.dtype)
    zero_pool = jnp.zeros((n_p, POOL_BUF, D_MODEL), x_prompt.dtype)
    y_prompt, conv_p, k_p, v_p, pool_p = _trunk(
        x_prompt, [zero_conv] * N_EVEN, None, None, [zero_pool] * N_ODD, 0,
        norm_mix, norm_ffn, w_in, q_norm, k_norm, sinks, w_dw, b_dw, conv_norm_g, conv_norm_b, w_out,
        w_pool, pool_scale, w_gate, w_up, w_down)
    y_sample, conv_s, k_s, v_s, pool_s = _trunk(
        x_sample, cache_conv, cache_k, cache_v, state_pool, PAST_LEN,
        norm_mix, norm_ffn, w_in, q_norm, k_norm, sinks, w_dw, b_dw, conv_norm_g, conv_norm_b, w_out,
        w_pool, pool_scale, w_gate, w_up, w_down)
    return (y_prompt, y_sample, conv_p, k_p, v_p, pool_p, conv_s, k_s, v_s, pool_s)
```

```python
import functools

import jax
import jax.numpy as jnp
from jax import lax
from jax.experimental import pallas as pl
from jax.experimental.pallas import tpu as pltpu

F32 = jnp.float32
BF16 = jnp.bfloat16

HEAD_DIM = 64
N_HEADS = 8
N_KV_HEADS = 2
GQA_GROUP = N_HEADS // N_KV_HEADS
D_ATTN = N_HEADS * HEAD_DIM
D_KV = N_KV_HEADS * HEAD_DIM
WINDOW = 128
PAST_LEN = 8192
D_CONV = 512
CONV_WIDTH = 31
CONV_HALO = 32
POOL_WINDOWS = (2, 4, 8, 16)
POOL_BUF = max(POOL_WINDOWS) - 1
POOL_HALO = 16
RMS_EPS = 1e-6
LN_EPS = 1e-5
NEG_INF = -1e30
LANES = 128
ALIBI_SLOPES = tuple(2.0 ** (-8.0 * (h + 1) / N_HEADS) for h in range(N_HEADS))

TOKEN_TILE = 512
CONV_TIME_TILE = 256
POOL_TIME_TILE = 512
FFN_CHUNK = 1408
SAMPLE_SEQ_TILE = 32
VMEM_LIMIT = 56 * 1024 * 1024


def _params(n_axes):
    return pltpu.CompilerParams(dimension_semantics=("arbitrary",) * n_axes, vmem_limit_bytes=VMEM_LIMIT)


def _rmsnorm(x, g):
    return x * lax.rsqrt(jnp.mean(x * x, axis=-1, keepdims=True) + RMS_EPS) * g


def _head_rmsnorm(x, g):
    lane = lax.broadcasted_iota(jnp.int32, (x.shape[0], LANES), 1)
    low = lane < HEAD_DIM
    outs = []
    for s in range(x.shape[1] // LANES):
        xs = x[:, s * LANES:(s + 1) * LANES]
        sq = xs * xs
        s_low = jnp.sum(jnp.where(low, sq, 0.0), axis=-1, keepdims=True)
        s_high = jnp.sum(jnp.where(low, 0.0, sq), axis=-1, keepdims=True)
        ms = jnp.where(low, s_low, s_high) * (1.0 / HEAD_DIM)
        outs.append(xs * lax.rsqrt(ms + RMS_EPS) * g[:, s * LANES:(s + 1) * LANES])
    return outs[0] if len(outs) == 1 else jnp.concatenate(outs, axis=-1)


def _layernorm_silu(y, g, b):
    mu = jnp.mean(y, axis=-1, keepdims=True)
    yc = y - mu
    var = jnp.mean(yc * yc, axis=-1, keepdims=True)
    z = yc * lax.rsqrt(var + LN_EPS) * g + b
    return z * jax.nn.sigmoid(z)


def _sink_softmax_weights(s, sink):
    m = jnp.maximum(jnp.max(s, axis=-1, keepdims=True), sink)
    p = jnp.exp(s - m)
    denom = jnp.sum(p, axis=-1, keepdims=True) + jnp.exp(sink - m)
    return p, 1.0 / denom


def _ffn_body(x_ref, g_ref, wg_ref, wu_ref, wd_ref, o_ref):
    x = x_ref[...]
    h = _rmsnorm(x, g_ref[...]).astype(BF16)
    acc = x
    for c0 in range(0, wg_ref.shape[1], FFN_CHUNK):
        gate = jnp.dot(h, wg_ref[:, c0:c0 + FFN_CHUNK], preferred_element_type=F32)
        up = jnp.dot(h, wu_ref[:, c0:c0 + FFN_CHUNK], preferred_element_type=F32)
        a = (gate * jax.nn.sigmoid(gate) * up).astype(BF16)
        acc = acc + jnp.dot(a, wd_ref[c0:c0 + FFN_CHUNK, :], preferred_element_type=F32)
    o_ref[...] = acc


def _ffn(x, norm_g, w_gate, w_up, w_down, layer):
    t, d = x.shape
    f = w_gate.shape[2]
    tile = pl.BlockSpec((TOKEN_TILE, d), lambda i: (i, 0))
    resident = functools.partial(pl.BlockSpec, pipeline_mode=pl.Buffered(1))
    return pl.pallas_call(
        _ffn_body,
        grid=(t // TOKEN_TILE,),
        in_specs=[tile,
                  pl.BlockSpec((None, 1, d), lambda i: (layer, 0, 0)),
                  resident((None, d, f), lambda i: (layer, 0, 0)),
                  resident((None, d, f), lambda i: (layer, 0, 0)),
                  resident((None, f, d), lambda i: (layer, 0, 0))],
        out_specs=tile,
        out_shape=jax.ShapeDtypeStruct((t, d), F32),
        compiler_params=_params(1),
        name="ffn",
    )(x, norm_g, w_gate, w_up, w_down)


def _inproj_body(x_ref, g_ref, w_ref, qn_ref, kn_ref, glu_ref, q_ref, k_ref, v_ref):
    h = _rmsnorm(x_ref[...], g_ref[...]).astype(BF16)
    u = jnp.dot(h, w_ref[...], preferred_element_type=F32)
    o_q = 2 * D_CONV
    o_k = o_q + D_ATTN
    o_v = o_k + D_KV
    glu_ref[...] = u[:, :D_CONV] * jax.nn.sigmoid(u[:, D_CONV:o_q])
    q_ref[...] = _head_rmsnorm(u[:, o_q:o_k], qn_ref[...]).astype(BF16)
    k_ref[...] = _head_rmsnorm(u[:, o_k:o_v], kn_ref[...])
    v_ref[...] = u[:, o_v:]


def _inproj(x, norm_g, w_in, q_gain, k_gain, layer, i_even):
    t, d = x.shape
    n = w_in.shape[2]
    row = lambda width: pl.BlockSpec((TOKEN_TILE, width), lambda i: (i, 0))
    return pl.pallas_call(
        _inproj_body,
        grid=(t // TOKEN_TILE,),
        in_specs=[row(d),
                  pl.BlockSpec((None, 1, d), lambda i: (layer, 0, 0)),
                  pl.BlockSpec((None, d, n), lambda i: (i_even, 0, 0), pipeline_mode=pl.Buffered(1)),
                  pl.BlockSpec((None, 1, D_ATTN), lambda i: (i_even, 0, 0)),
                  pl.BlockSpec((None, 1, D_KV), lambda i: (i_even, 0, 0))],
        out_specs=[row(D_CONV), row(D_ATTN), row(D_KV), row(D_KV)],
        out_shape=[jax.ShapeDtypeStruct((t, D_CONV), F32), jax.ShapeDtypeStruct((t, D_ATTN), BF16),
                   jax.ShapeDtypeStruct((t, D_KV), F32), jax.ShapeDtypeStruct((t, D_KV), F32)],
        compiler_params=_params(1),
        name="inproj",
    )(x, norm_g, w_in, q_gain, k_gain)


def _outproj_body(x_ref, c_ref, a_ref, w_ref, o_ref):
    y = jnp.dot(c_ref[...], w_ref[:D_CONV, :], preferred_element_type=F32)
    y = y + jnp.dot(a_ref[...], w_ref[D_CONV:, :], preferred_element_type=F32)
    o_ref[...] = x_ref[...] + y


def _outproj(x, conv_out, attn, w_out, i_even):
    t, d = x.shape
    row = lambda width: pl.BlockSpec((TOKEN_TILE, width), lambda i: (i, 0))
    return pl.pallas_call(
        _outproj_body,
        grid=(t // TOKEN_TILE,),
        in_specs=[row(d), row(D_CONV), row(D_ATTN),
                  pl.BlockSpec((None, D_CONV + D_ATTN, d), lambda i: (i_even, 0, 0), pipeline_mode=pl.Buffered(1))],
        out_specs=row(d),
        out_shape=jax.ShapeDtypeStruct((t, d), F32),
        compiler_params=_params(1),
        name="outproj",
    )(x, conv_out, attn, w_out)


def _conv_prompt_body(cur_ref, prev_ref, w_ref, b_ref, g_ref, beta_ref, o_ref, ext_ref, y_ref):
    tt = cur_ref.shape[0]
    first = pl.program_id(1) == 0
    ext_ref[0:CONV_HALO, :] = jnp.where(first, 0.0, prev_ref[...])
    ext_ref[CONV_HALO:, :] = cur_ref[...]
    rows = 32
    lead = CONV_HALO - (CONV_WIDTH - 1)
    for c0 in range(0, D_CONV, LANES):
        taps = [w_ref[k:k + 1, c0:c0 + LANES] for k in range(CONV_WIDTH)]
        bias = b_ref[:, c0:c0 + LANES]
        for r0 in range(0, tt, rows):
            acc = jnp.broadcast_to(bias, (rows, LANES))
            for k in range(CONV_WIDTH):
                acc = acc + taps[k] * ext_ref[r0 + lead + k:r0 + lead + k + rows, c0:c0 + LANES]
            y_ref[r0:r0 + rows, c0:c0 + LANES] = acc
    o_ref[...] = _layernorm_silu(y_ref[...], g_ref[...], beta_ref[...]).astype(o_ref.dtype)


def _conv_prompt(glu, w_dw, b_dw, ln_g, ln_b, i_even):
    n, t, c = glu.shape
    tt = CONV_TIME_TILE
    halo_per_tile = tt // CONV_HALO
    vec = pl.BlockSpec((None, 1, c), lambda b, i: (i_even, 0, 0))
    return pl.pallas_call(
        _conv_prompt_body,
        grid=(n, t // tt),
        in_specs=[pl.BlockSpec((None, tt, c), lambda b, i: (b, i, 0)),
                  pl.BlockSpec((None, CONV_HALO, c), lambda b, i: (b, jnp.maximum(i * halo_per_tile - 1, 0), 0)),
                  pl.BlockSpec((None, CONV_WIDTH, c), lambda b, i: (i_even, 0, 0)),
                  vec, vec, vec],
        out_specs=pl.BlockSpec((None, tt, c), lambda b, i: (b, i, 0)),
        out_shape=jax.ShapeDtypeStruct((n, t, c), BF16),
        scratch_shapes=[pltpu.VMEM((CONV_HALO + tt, c), F32), pltpu.VMEM((tt, c), F32)],
        compiler_params=_params(2),
        name="conv_prompt",
    )(glu, glu, w_dw, b_dw, ln_g, ln_b)


def _attn_prompt_body(sink_ref, q_ref, kc_ref, kp_ref, vc_ref, vp_ref, o_ref):
    blk = pl.program_id(1)
    q = q_ref[...]
    k2 = jnp.concatenate([kp_ref[...], kc_ref[...]], axis=0).astype(BF16)
    v2 = jnp.concatenate([vp_ref[...], vc_ref[...]], axis=0).astype(BF16)
    row = lax.broadcasted_iota(jnp.int32, (WINDOW, 2 * WINDOW), 0)
    col = lax.broadcasted_iota(jnp.int32, (WINDOW, 2 * WINDOW), 1)
    dist = WINDOW + row - col
    valid = (dist >= 0) & (dist < WINDOW) & ((blk > 0) | (col >= WINDOW))
    distf = dist.astype(F32)
    outs = []
    for h in range(N_HEADS):
        kv = h // GQA_GROUP
        qh = q[:, h * HEAD_DIM:(h + 1) * HEAD_DIM]
        kh = k2[:, kv * HEAD_DIM:(kv + 1) * HEAD_DIM]
        vh = v2[:, kv * HEAD_DIM:(kv + 1) * HEAD_DIM]
        s = lax.dot_general(qh, kh, (((1,), (1,)), ((), ())), preferred_element_type=F32) * (HEAD_DIM ** -0.5)
        s = jnp.where(valid, s - ALIBI_SLOPES[h] * distf, NEG_INF)
        p, inv = _sink_softmax_weights(s, sink_ref[h])
        outs.append(jnp.dot(p.astype(BF16), vh, preferred_element_type=F32) * inv)
    o_ref[...] = jnp.concatenate(outs, axis=-1).astype(o_ref.dtype)


def _attn_prompt(q, k, v, sinks):
    n, t, _ = q.shape
    cur = lambda width: pl.BlockSpec((None, WINDOW, width), lambda b, i: (b, i, 0))
    prev = lambda width: pl.BlockSpec((None, WINDOW, width), lambda b, i: (b, jnp.maximum(i - 1, 0), 0))
    return pl.pallas_call(
        _attn_prompt_body,
        grid=(n, t // WINDOW),
        in_specs=[pl.BlockSpec(memory_space=pltpu.SMEM), cur(D_ATTN), cur(D_KV), prev(D_KV), cur(D_KV), prev(D_KV)],
        out_specs=cur(D_ATTN),
        out_shape=jax.ShapeDtypeStruct((n, t, D_ATTN), BF16),
        compiler_params=_params(2),
        name="attn_prompt",
    )(sinks, q, k, k, v, v)


def _pool_windows(ext_at, xn, pos):
    gdim = xn.shape[-1] // len(POOL_WINDOWS)
    parts = []
    for g, w in enumerate(POOL_WINDOWS):
        c0, c1 = g * gdim, (g + 1) * gdim
        acc = ext_at(0, c0, c1)
        for j in range(1, w):
            acc = acc + ext_at(j, c0, c1)
        cnt = jnp.minimum(w, pos + 1).astype(F32)
        parts.append(acc / cnt - xn[..., c0:c1])
    return parts


def _pool_prompt_body(x_ref, g_ref, w_ref, sc_ref, o_ref, tail_ref, ext_ref):
    tt, d = x_ref.shape
    i = pl.program_id(1)

    @pl.when(i == 0)
    def _():
        ext_ref[0:POOL_HALO, :] = jnp.zeros((POOL_HALO, d), F32)

    @pl.when(i > 0)
    def _():
        ext_ref[0:POOL_HALO, :] = ext_ref[tt:tt + POOL_HALO, :]

    x = x_ref[...]
    xn = _rmsnorm(x, g_ref[...])
    ext_ref[POOL_HALO:, :] = xn
    tail_ref[...] = xn[tt - POOL_HALO:, :]
    pos = i * tt + lax.broadcasted_iota(jnp.int32, (tt, 1), 0)
    parts = _pool_windows(lambda j, c0, c1: ext_ref[POOL_HALO - j:POOL_HALO - j + tt, c0:c1], xn, pos)
    ys = [jnp.dot(p.astype(BF16), w_ref[g], preferred_element_type=F32) for g, p in enumerate(parts)]
    o_ref[...] = x + jnp.concatenate(ys, axis=-1) * sc_ref[...]


def _pool_prompt(x, norm_g, w_pool, pool_scale, layer, i_odd):
    n, t, d = x.shape
    tt = POOL_TIME_TILE
    ng, gd, _ = w_pool.shape[1:]
    tile = pl.BlockSpec((None, tt, d), lambda b, i: (b, i, 0))
    return pl.pallas_call(
        _pool_prompt_body,
        grid=(n, t // tt),
        in_specs=[tile,
                  pl.BlockSpec((None, 1, d), lambda b, i: (layer, 0, 0)),
                  pl.BlockSpec((None, ng, gd, gd), lambda b, i: (i_odd, 0, 0, 0)),
                  pl.BlockSpec((None, 1, d), lambda b, i: (i_odd, 0, 0))],
        out_specs=[tile, pl.BlockSpec((None, POOL_HALO, d), lambda b, i: (b, 0, 0))],
        out_shape=[jax.ShapeDtypeStruct((n, t, d), F32), jax.ShapeDtypeStruct((n, POOL_HALO, d), F32)],
        scratch_shapes=[pltpu.VMEM((POOL_HALO + tt, d), F32)],
        compiler_params=_params(2),
        name="pool_prompt",
    )(x, norm_g, w_pool, pool_scale)


def _conv_sample_body(ext_ref, w_ref, b_ref, g_ref, beta_ref, o_ref):
    nb, _, c = ext_ref.shape
    t = o_ref.shape[1]
    acc = jnp.broadcast_to(b_ref[...].reshape(1, 1, c), (nb, t, c))
    for k in range(CONV_WIDTH):
        acc = acc + w_ref[k:k + 1, :].reshape(1, 1, c) * ext_ref[:, k:k + t, :]
    o_ref[...] = _layernorm_silu(acc, g_ref[...].reshape(1, 1, c), beta_ref[...].reshape(1, 1, c)).astype(o_ref.dtype)


def _conv_sample(ext, w_dw, b_dw, ln_g, ln_b, i_even, t):
    n, rows, c = ext.shape
    nb = SAMPLE_SEQ_TILE
    vec = pl.BlockSpec((None, 1, c), lambda b: (i_even, 0, 0))
    return pl.pallas_call(
        _conv_sample_body,
        grid=(n // nb,),
        in_specs=[pl.BlockSpec((nb, rows, c), lambda b: (b, 0, 0)),
                  pl.BlockSpec((None, CONV_WIDTH, c), lambda b: (i_even, 0, 0)),
                  vec, vec, vec],
        out_specs=pl.BlockSpec((nb, t, c), lambda b: (b, 0, 0)),
        out_shape=jax.ShapeDtypeStruct((n, t, c), BF16),
        compiler_params=_params(1),
        name="conv_sample",
    )(ext, w_dw, b_dw, ln_g, ln_b)


def _attn_sample_body(sink_ref, q_ref, k_ref, v_ref, o_ref, *, t, buf_len):
    kv = pl.program_id(0)
    nb, rows, _ = q_ref.shape
    keys = k_ref.shape[1]
    s = jnp.einsum("bqd,bkd->bqk", q_ref[...], k_ref[...], preferred_element_type=F32) * (HEAD_DIM ** -0.5)
    row = lax.broadcasted_iota(jnp.int32, (rows, keys), 0)
    col = lax.broadcasted_iota(jnp.int32, (rows, keys), 1)
    dist = buf_len + row % t - col
    valid = (dist >= 0) & (dist < WINDOW)
    head_in_group = lax.broadcasted_iota(jnp.int32, (rows, 1), 0) // t
    slope = jnp.zeros((rows, 1), F32)
    sink = jnp.zeros((rows, 1), F32)
    for j in range(GQA_GROUP):
        slope_j = jnp.where(kv == 0, ALIBI_SLOPES[j], ALIBI_SLOPES[GQA_GROUP + j])
        slope = jnp.where(head_in_group == j, slope_j, slope)
        sink = jnp.where(head_in_group == j, sink_ref[kv * GQA_GROUP + j], sink)
    s = jnp.where(valid, s - slope * dist.astype(F32), NEG_INF)
    p, inv = _sink_softmax_weights(s, sink)
    o = jnp.einsum("bqk,bkd->bqd", p.astype(BF16), v_ref[...], preferred_element_type=F32) * inv
    o_ref[...] = o.astype(o_ref.dtype)


def _attn_sample(q, k2, v2, sinks, t, buf_len):
    nkv, n, rows, hd = q.shape
    keys = k2.shape[2]
    nb = SAMPLE_SEQ_TILE
    spec = lambda r: pl.BlockSpec((None, nb, r, hd), lambda g, b: (g, b, 0, 0))
    return pl.pallas_call(
        functools.partial(_attn_sample_body, t=t, buf_len=buf_len),
        grid=(nkv, n // nb),
        in_specs=[pl.BlockSpec(memory_space=pltpu.SMEM), spec(rows), spec(keys), spec(keys)],
        out_specs=spec(rows),
        out_shape=jax.ShapeDtypeStruct((nkv, n, rows, hd), BF16),
        compiler_params=_params(2),
        name="attn_sample",
    )(sinks, q, k2, v2)


def _pool_sample_body(x_ref, st_ref, g_ref, w_ref, sc_ref, o_ref, new_ref, ext_ref, *, pos0):
    nb, t, d = x_ref.shape
    x = x_ref[...]
    xn = _rmsnorm(x, g_ref[...].reshape(1, 1, d))
    ext_ref[:, POOL_HALO - POOL_BUF:POOL_HALO, :] = st_ref[...]
    ext_ref[:, POOL_HALO:, :] = xn
    new_ref[...] = ext_ref[:, POOL_HALO + t - POOL_BUF:, :]
    pos = pos0 + lax.broadcasted_iota(jnp.int32, (1, t, 1), 1)
    parts = _pool_windows(lambda j, c0, c1: ext_ref[:, POOL_HALO - j:POOL_HALO - j + t, c0:c1], xn, pos)
    ys = [jnp.dot(p.reshape(nb * t, -1).astype(BF16), w_ref[g], preferred_element_type=F32)
          for g, p in enumerate(parts)]
    y = jnp.concatenate(ys, axis=-1).reshape(nb, t, d)
    o_ref[...] = x + y * sc_ref[...].reshape(1, 1, d)


def _pool_sample(x, state, norm_g, w_pool, pool_scale, layer, i_odd, pos0):
    n, t, d = x.shape
    nb = SAMPLE_SEQ_TILE
    ng, gd, _ = w_pool.shape[1:]
    return pl.pallas_call(
        functools.partial(_pool_sample_body, pos0=pos0),
        grid=(n // nb,),
        in_specs=[pl.BlockSpec((nb, t, d), lambda b: (b, 0, 0)),
                  pl.BlockSpec((None, nb, POOL_BUF, d), lambda b: (i_odd, b, 0, 0)),
                  pl.BlockSpec((None, 1, d), lambda b: (layer, 0, 0)),
                  pl.BlockSpec((None, ng, gd, gd), lambda b: (i_odd, 0, 0, 0)),
                  pl.BlockSpec((None, 1, d), lambda b: (i_odd, 0, 0))],
        out_specs=[pl.BlockSpec((nb, t, d), lambda b: (b, 0, 0)),
                   pl.BlockSpec((nb, POOL_BUF, d), lambda b: (b, 0, 0))],
        out_shape=[jax.ShapeDtypeStruct((n, t, d), F32), jax.ShapeDtypeStruct((n, POOL_BUF, d), F32)],
        scratch_shapes=[pltpu.VMEM((nb, POOL_HALO + t, d), F32)],
        compiler_params=_params(1),
        name="pool_sample",
    )(x, state, norm_g, w_pool, pool_scale)


def _prepare_weights(norm_mix, norm_ffn, w_in, q_norm, k_norm, sinks, w_dw, b_dw, conv_norm_g, conv_norm_b, w_out,
                     w_pool, pool_scale, w_gate, w_up, w_down):
    vec = lambda a: a[:, None, :]
    return dict(
        norm_mix=vec(norm_mix), norm_ffn=vec(norm_ffn), w_in=w_in.astype(BF16),
        q_gain=vec(jnp.tile(q_norm, (1, N_HEADS))), k_gain=vec(jnp.tile(k_norm, (1, N_KV_HEADS))), sinks=sinks,
        w_dw=w_dw, b_dw=vec(b_dw), ln_g=vec(conv_norm_g), ln_b=vec(conv_norm_b), w_out=w_out.astype(BF16),
        w_pool=w_pool.astype(BF16), pool_scale=vec(pool_scale),
        w_gate=w_gate.astype(BF16), w_up=w_up.astype(BF16), w_down=w_down.astype(BF16))


def _trunk_prompt(x, p, depth):
    n, t, d = x.shape
    xt = x.reshape(n * t, d)
    new_conv, new_k, new_v, new_pool = [], [], [], []
    for layer in range(depth):
        i = layer // 2
        if layer % 2 == 0:
            glu, q, k, v = _inproj(xt, p["norm_mix"], p["w_in"], p["q_gain"], p["k_gain"], layer, i)
            glu, q, k, v = (a.reshape(n, t, -1) for a in (glu, q, k, v))
            conv_out = _conv_prompt(glu, p["w_dw"], p["b_dw"], p["ln_g"], p["ln_b"], i)
            attn = _attn_prompt(q, k, v, p["sinks"][i])
            xt = _outproj(xt, conv_out.reshape(n * t, -1), attn.reshape(n * t, -1), p["w_out"], i)
            keep = min(WINDOW, t)
            new_conv.append(glu[:, t - (CONV_WIDTH - 1):])
            new_k.append(k[:, t - keep:].reshape(n, keep, N_KV_HEADS, HEAD_DIM))
            new_v.append(v[:, t - keep:].reshape(n, keep, N_KV_HEADS, HEAD_DIM))
        else:
            y, tail = _pool_prompt(xt.reshape(n, t, d), p["norm_mix"], p["w_pool"], p["pool_scale"], layer, i)
            xt = y.reshape(n * t, d)
            new_pool.append(tail[:, POOL_HALO - POOL_BUF:])
        xt = _ffn(xt, p["norm_ffn"], p["w_gate"], p["w_up"], p["w_down"], layer)
    return xt.reshape(n, t, d), jnp.stack(new_conv), jnp.stack(new_k), jnp.stack(new_v), jnp.stack(new_pool)


def _trunk_sample(x, conv_bufs, k_bufs, v_bufs, pool_bufs, pos0, p, depth):
    n, t, d = x.shape
    buf_len = k_bufs.shape[2]
    xt = x.reshape(n * t, d)
    new_conv, new_k, new_v, new_pool = [], [], [], []
    for layer in range(depth):
        i = layer // 2
        if layer % 2 == 0:
            glu, q, k, v = _inproj(xt, p["norm_mix"], p["w_in"], p["q_gain"], p["k_gain"], layer, i)
            ext = jnp.concatenate([conv_bufs[i], glu.reshape(n, t, D_CONV)], axis=1)
            conv_out = _conv_sample(ext, p["w_dw"], p["b_dw"], p["ln_g"], p["ln_b"], i, t)
            k2 = jnp.concatenate([k_bufs[i], k.reshape(n, t, N_KV_HEADS, HEAD_DIM)], axis=1)
            v2 = jnp.concatenate([v_bufs[i], v.reshape(n, t, N_KV_HEADS, HEAD_DIM)], axis=1)
            qh = q.reshape(n, t, N_KV_HEADS, GQA_GROUP, HEAD_DIM).transpose(2, 0, 3, 1, 4)
            qh = qh.reshape(N_KV_HEADS, n, GQA_GROUP * t, HEAD_DIM)
            kh = k2.astype(BF16).transpose(2, 0, 1, 3)
            vh = v2.astype(BF16).transpose(2, 0, 1, 3)
            oh = _attn_sample(qh, kh, vh, p["sinks"][i], t, buf_len)
            attn = oh.reshape(N_KV_HEADS, n, GQA_GROUP, t, HEAD_DIM).transpose(1, 3, 0, 2, 4).reshape(n * t, D_ATTN)
            xt = _outproj(xt, conv_out.reshape(n * t, D_CONV), attn, p["w_out"], i)
            new_conv.append(ext[:, t:])
            new_k.append(k2[:, t:])
            new_v.append(v2[:, t:])
        else:
            y, pool_new = _pool_sample(xt.reshape(n, t, d), pool_bufs, p["norm_mix"], p["w_pool"], p["pool_scale"],
                                       layer, i, pos0)
            xt = y.reshape(n * t, d)
            new_pool.append(pool_new)
        xt = _ffn(xt, p["norm_ffn"], p["w_gate"], p["w_up"], p["w_down"], layer)
    return xt.reshape(n, t, d), jnp.stack(new_conv), jnp.stack(new_k), jnp.stack(new_v), jnp.stack(new_pool)


def kernel(x_prompt, x_sample, cache_conv, cache_k, cache_v, state_pool, norm_mix, norm_ffn, w_in, q_norm, k_norm,
           sinks, w_dw, b_dw, conv_norm_g, conv_norm_b, w_out, w_pool, pool_scale, w_gate, w_up, w_down):
    depth = norm_mix.shape[0]
    p = _prepare_weights(norm_mix, norm_ffn, w_in, q_norm, k_norm, sinks, w_dw, b_dw, conv_norm_g, conv_norm_b, w_out,
                         w_pool, pool_scale, w_gate, w_up, w_down)
    y_p, conv_p, k_p, v_p, pool_p = _trunk_prompt(x_prompt, p, depth)
    y_s, conv_s, k_s, v_s, pool_s = _trunk_sample(x_sample, cache_conv, cache_k, cache_v, state_pool, PAST_LEN, p, depth)
    return (y_p, y_s, conv_p, k_p, v_p, pool_p, conv_s, k_s, v_s, pool_s)
```

```python
import functools

import jax
import jax.numpy as jnp
from jax import lax
from jax.experimental import pallas as pl
from jax.experimental.pallas import tpu as pltpu

F32 = jnp.float32
BF16 = jnp.bfloat16

HEAD_DIM = 64
N_HEADS = 8
N_KV_HEADS = 2
GQA_GROUP = N_HEADS // N_KV_HEADS
D_ATTN = N_HEADS * HEAD_DIM
D_KV = N_KV_HEADS * HEAD_DIM
WINDOW = 128
PAST_LEN = 8192
D_CONV = 512
CONV_WIDTH = 31
CONV_HALO = 32
POOL_WINDOWS = (2, 4, 8, 16)
POOL_BUF = max(POOL_WINDOWS) - 1
POOL_HALO = 16
RMS_EPS = 1e-6
LN_EPS = 1e-5
NEG_INF = -1e30
LANES = 128
ALIBI_SLOPES = tuple(2.0 ** (-8.0 * (h + 1) / N_HEADS) for h in range(N_HEADS))

TOKEN_TILE = 512
CONV_TIME_TILE = 512
CONV_PHASES = 4
ATTN_TIME_TILE = 512
POOL_TIME_TILE = 512
FFN_CHUNK = 1408
SAMPLE_SEQ_TILE = 32
VMEM_LIMIT = 56 * 1024 * 1024


def _params(n_axes):
    return pltpu.CompilerParams(dimension_semantics=("arbitrary",) * n_axes, vmem_limit_bytes=VMEM_LIMIT)


def _rmsnorm(x, g):
    return x * lax.rsqrt(jnp.mean(x * x, axis=-1, keepdims=True) + RMS_EPS) * g


def _head_rmsnorm(x, g):
    lane = lax.broadcasted_iota(jnp.int32, (x.shape[0], LANES), 1)
    low = lane < HEAD_DIM
    outs = []
    for s in range(x.shape[1] // LANES):
        xs = x[:, s * LANES:(s + 1) * LANES]
        sq = xs * xs
        s_low = jnp.sum(jnp.where(low, sq, 0.0), axis=-1, keepdims=True)
        s_high = jnp.sum(jnp.where(low, 0.0, sq), axis=-1, keepdims=True)
        ms = jnp.where(low, s_low, s_high) * (1.0 / HEAD_DIM)
        outs.append(xs * lax.rsqrt(ms + RMS_EPS) * g[:, s * LANES:(s + 1) * LANES])
    return outs[0] if len(outs) == 1 else jnp.concatenate(outs, axis=-1)


def _layernorm_silu(y, g, b):
    mu = jnp.mean(y, axis=-1, keepdims=True)
    yc = y - mu
    var = jnp.mean(yc * yc, axis=-1, keepdims=True)
    z = yc * lax.rsqrt(var + LN_EPS) * g + b
    return z * jax.nn.sigmoid(z)


def _sink_softmax_weights(s, sink):
    m = jnp.maximum(jnp.max(s, axis=-1, keepdims=True), sink)
    p = jnp.exp(s - m)
    denom = jnp.sum(p, axis=-1, keepdims=True) + jnp.exp(sink - m)
    return p, 1.0 / denom


def _ffn_body(x_ref, g_ref, wg_ref, wu_ref, wd_ref, o_ref):
    x = x_ref[...]
    h = _rmsnorm(x, g_ref[...]).astype(BF16)
    acc = x
    for c0 in range(0, wg_ref.shape[1], FFN_CHUNK):
        gate = jnp.dot(h, wg_ref[:, c0:c0 + FFN_CHUNK], preferred_element_type=F32)
        up = jnp.dot(h, wu_ref[:, c0:c0 + FFN_CHUNK], preferred_element_type=F32)
        a = (gate * jax.nn.sigmoid(gate) * up).astype(BF16)
        acc = acc + jnp.dot(a, wd_ref[c0:c0 + FFN_CHUNK, :], preferred_element_type=F32)
    o_ref[...] = acc


def _ffn(x, norm_g, w_gate, w_up, w_down, layer):
    t, d = x.shape
    f = w_gate.shape[2]
    tile = pl.BlockSpec((TOKEN_TILE, d), lambda i: (i, 0))
    resident = functools.partial(pl.BlockSpec, pipeline_mode=pl.Buffered(1))
    return pl.pallas_call(
        _ffn_body,
        grid=(t // TOKEN_TILE,),
        in_specs=[tile,
                  pl.BlockSpec((None, 1, d), lambda i: (layer, 0, 0)),
                  resident((None, d, f), lambda i: (layer, 0, 0)),
                  resident((None, d, f), lambda i: (layer, 0, 0)),
                  resident((None, f, d), lambda i: (layer, 0, 0))],
        out_specs=tile,
        out_shape=jax.ShapeDtypeStruct((t, d), F32),
        compiler_params=_params(1),
        name="ffn",
    )(x, norm_g, w_gate, w_up, w_down)


def _inproj_body(x_ref, g_ref, w_ref, qn_ref, kn_ref, glu_ref, q_ref, k_ref, v_ref):
    h = _rmsnorm(x_ref[...], g_ref[...]).astype(BF16)
    u = jnp.dot(h, w_ref[...], preferred_element_type=F32)
    o_q = 2 * D_CONV
    o_k = o_q + D_ATTN
    o_v = o_k + D_KV
    glu_ref[...] = u[:, :D_CONV] * jax.nn.sigmoid(u[:, D_CONV:o_q])
    q_ref[...] = (_head_rmsnorm(u[:, o_q:o_k], qn_ref[...]) * (HEAD_DIM ** -0.5)).astype(BF16)
    k_ref[...] = _head_rmsnorm(u[:, o_k:o_v], kn_ref[...])
    v_ref[...] = u[:, o_v:]


def _inproj(x, norm_g, w_in, q_gain, k_gain, layer, i_even):
    t, d = x.shape
    n = w_in.shape[2]
    row = lambda width: pl.BlockSpec((TOKEN_TILE, width), lambda i: (i, 0))
    return pl.pallas_call(
        _inproj_body,
        grid=(t // TOKEN_TILE,),
        in_specs=[row(d),
                  pl.BlockSpec((None, 1, d), lambda i: (layer, 0, 0)),
                  pl.BlockSpec((None, d, n), lambda i: (i_even, 0, 0), pipeline_mode=pl.Buffered(1)),
                  pl.BlockSpec((None, 1, D_ATTN), lambda i: (i_even, 0, 0)),
                  pl.BlockSpec((None, 1, D_KV), lambda i: (i_even, 0, 0))],
        out_specs=[row(D_CONV), row(D_ATTN), row(D_KV), row(D_KV)],
        out_shape=[jax.ShapeDtypeStruct((t, D_CONV), F32), jax.ShapeDtypeStruct((t, D_ATTN), BF16),
                   jax.ShapeDtypeStruct((t, D_KV), F32), jax.ShapeDtypeStruct((t, D_KV), F32)],
        compiler_params=_params(1),
        name="inproj",
    )(x, norm_g, w_in, q_gain, k_gain)


def _outproj_body(x_ref, c_ref, a_ref, w_ref, o_ref):
    y = jnp.dot(c_ref[...], w_ref[:D_CONV, :], preferred_element_type=F32)
    y = y + jnp.dot(a_ref[...], w_ref[D_CONV:, :], preferred_element_type=F32)
    o_ref[...] = x_ref[...] + y


def _outproj(x, conv_out, attn, w_out, i_even):
    t, d = x.shape
    row = lambda width: pl.BlockSpec((TOKEN_TILE, width), lambda i: (i, 0))
    return pl.pallas_call(
        _outproj_body,
        grid=(t // TOKEN_TILE,),
        in_specs=[row(d), row(D_CONV), row(D_ATTN),
                  pl.BlockSpec((None, D_CONV + D_ATTN, d), lambda i: (i_even, 0, 0), pipeline_mode=pl.Buffered(1))],
        out_specs=row(d),
        out_shape=jax.ShapeDtypeStruct((t, d), F32),
        compiler_params=_params(1),
        name="outproj",
    )(x, conv_out, attn, w_out)


def _depthwise_conv_slab(ext_ref, y_ref, w_ref, b_ref, slab, n_rows):
    lead = CONV_HALO - (CONV_WIDTH - 1)
    rows = 8 * CONV_PHASES
    lanes = slice(slab * LANES, (slab + 1) * LANES)

    def chunk(c, carry):
        r0 = c * rows
        acc = [b_ref[:, lanes]] * CONV_PHASES
        for e in range(CONV_WIDTH + CONV_PHASES - 1):
            v = ext_ref[slab, pl.ds(r0 + lead + e, 8, stride=CONV_PHASES), :]
            for ph in range(CONV_PHASES):
                k = e - ph
                if 0 <= k < CONV_WIDTH:
                    acc[ph] = acc[ph] + w_ref[k, :, lanes] * v
        for ph in range(CONV_PHASES):
            y_ref[slab, pl.ds(r0 + ph, 8, stride=CONV_PHASES), :] = acc[ph]
        return carry

    lax.fori_loop(0, n_rows // rows, chunk, 0, unroll=2)


def _layernorm_silu_slabs(y_ref, g_ref, beta_ref, o_ref):
    n_slabs = y_ref.shape[0]
    ys = [y_ref[s] for s in range(n_slabs)]
    inv_c = 1.0 / (n_slabs * LANES)
    mu = jnp.sum(sum(ys), axis=-1, keepdims=True) * inv_c
    ycs = [y - mu for y in ys]
    var = jnp.sum(sum(yc * yc for yc in ycs), axis=-1, keepdims=True) * inv_c
    rstd = lax.rsqrt(var + LN_EPS)
    for s, yc in enumerate(ycs):
        lanes = slice(s * LANES, (s + 1) * LANES)
        z = yc * rstd * g_ref[:, lanes] + beta_ref[:, lanes]
        o_ref[:, lanes] = (z * jax.nn.sigmoid(z)).astype(o_ref.dtype)


def _conv_prompt_body(cur_ref, prev_ref, w_ref, b_ref, g_ref, beta_ref, o_ref, ext_ref, y_ref):
    tt = cur_ref.shape[0]
    first = pl.program_id(1) == 0
    for s in range(D_CONV // LANES):
        lanes = slice(s * LANES, (s + 1) * LANES)
        ext_ref[s, 0:CONV_HALO, :] = jnp.where(first, 0.0, prev_ref[:, lanes])
        ext_ref[s, CONV_HALO:, :] = cur_ref[:, lanes]
    for s in range(D_CONV // LANES):
        _depthwise_conv_slab(ext_ref, y_ref, w_ref, b_ref, s, tt)
    _layernorm_silu_slabs(y_ref, g_ref, beta_ref, o_ref)


def _conv_prompt(glu, w_dw8, b_dw8, ln_g, ln_b, i_even):
    n, t, c = glu.shape
    tt = CONV_TIME_TILE
    halo_per_tile = tt // CONV_HALO
    vec = pl.BlockSpec((None, 1, c), lambda b, i: (i_even, 0, 0))
    return pl.pallas_call(
        _conv_prompt_body,
        grid=(n, t // tt),
        in_specs=[pl.BlockSpec((None, tt, c), lambda b, i: (b, i, 0)),
                  pl.BlockSpec((None, CONV_HALO, c), lambda b, i: (b, jnp.maximum(i * halo_per_tile - 1, 0), 0)),
                  pl.BlockSpec((None, CONV_WIDTH, 8, c), lambda b, i: (i_even, 0, 0, 0)),
                  pl.BlockSpec((None, 8, c), lambda b, i: (i_even, 0, 0)),
                  vec, vec],
        out_specs=pl.BlockSpec((None, tt, c), lambda b, i: (b, i, 0)),
        out_shape=jax.ShapeDtypeStruct((n, t, c), BF16),
        scratch_shapes=[pltpu.VMEM((c // LANES, CONV_HALO + tt, LANES), F32), pltpu.VMEM((c // LANES, tt, LANES), F32)],
        compiler_params=_params(2),
        name="conv_prompt",
    )(glu, glu, w_dw8, b_dw8, ln_g, ln_b)


def _attn_prompt_body(sink_ref, q_ref, kc_ref, kp_ref, vc_ref, vp_ref, o_ref, bias_ref, sinkcol_ref, s_ref):
    b = pl.program_id(0)
    i = pl.program_id(1)
    w = WINDOW
    pair = 2 * HEAD_DIM
    n_blocks = q_ref.shape[0] // w
    n_slabs = D_ATTN // pair
    lane = lax.broadcasted_iota(jnp.int32, (1, pair), 1)
    low = lane < HEAD_DIM

    @pl.when((b == 0) & (i == 0))
    def _():
        row = lax.broadcasted_iota(jnp.int32, (w, 2 * w), 0)
        col = lax.broadcasted_iota(jnp.int32, (w, 2 * w), 1)
        dist = w + row - col
        valid = (dist >= 0) & (dist < w)
        distf = dist.astype(F32)
        for h in range(N_HEADS):
            slab, half = h // 2, h % 2
            bias_ref[slab * w:(slab + 1) * w, half * 2 * w:(half + 1) * 2 * w] = (
                jnp.where(valid, -ALIBI_SLOPES[h] * distf, NEG_INF))
            sinkcol_ref[half, slab * w:(slab + 1) * w, :] = jnp.full((w, 1), sink_ref[h], F32)

    kcat = jnp.concatenate([kp_ref[...], kc_ref[...]], axis=0)
    vcat = jnp.concatenate([vp_ref[...], vc_ref[...]], axis=0)
    krot = pltpu.roll(kcat, HEAD_DIM, 1)
    vrot = pltpu.roll(vcat, HEAD_DIM, 1)
    zero = jnp.zeros((), F32)
    k_low = [jnp.where(low, kcat, zero).astype(BF16), jnp.where(low, krot, zero).astype(BF16)]
    k_high = [jnp.where(low, zero, krot).astype(BF16), jnp.where(low, zero, kcat).astype(BF16)]
    v_low = [jnp.where(low, vcat, zero).astype(BF16), jnp.where(low, vrot, zero).astype(BF16)]
    v_high = [jnp.where(low, zero, vrot).astype(BF16), jnp.where(low, zero, vcat).astype(BF16)]
    ones_low = jnp.broadcast_to(jnp.where(low, 1.0, 0.0).astype(BF16), (2 * w, pair))
    ones_high = jnp.broadcast_to(jnp.where(low, 0.0, 1.0).astype(BF16), (2 * w, pair))
    ones_ext = jnp.concatenate([ones_low, ones_high], axis=0)
    col = lax.broadcasted_iota(jnp.int32, (1, 4 * w), 1)
    no_prev = (col % (2 * w) < w) & (i == 0)

    for jb in range(n_blocks):
        rows = slice(jb * w, (jb + 1) * w)
        keys = slice(jb * w, (jb + 2) * w)
        for g in range(N_KV_HEADS):
            q2 = jnp.concatenate([q_ref[rows, (2 * g) * pair:(2 * g + 1) * pair],
                                  q_ref[rows, (2 * g + 1) * pair:(2 * g + 2) * pair]], axis=0)
            kk = jnp.concatenate([k_low[g][keys], k_high[g][keys]], axis=0)
            s_ref[jb, 2 * g * w:(2 * g + 2) * w, :] = lax.dot_general(
                q2, kk, (((1,), (1,)), ((), ())), preferred_element_type=F32)

    for jb in range(n_blocks):
        rows = slice(jb * w, (jb + 1) * w)
        keys = slice(jb * w, (jb + 2) * w)
        bias = bias_ref[...]
        if jb == 0:
            bias = jnp.where(no_prev, NEG_INF, bias)
        s = s_ref[jb] + bias
        ps, sink_terms = [], []
        for half in range(2):
            sh = s[:, half * 2 * w:(half + 1) * 2 * w]
            sink = sinkcol_ref[half]
            m = jnp.maximum(jnp.max(sh, axis=-1, keepdims=True), sink)
            ps.append(jnp.exp(sh - m).astype(BF16))
            sink_terms.append(jnp.exp(sink - m))
        p = jnp.concatenate(ps, axis=1)
        sink_term = jnp.where(low, sink_terms[0], sink_terms[1])
        for g in range(N_KV_HEADS):
            vv = jnp.concatenate([jnp.concatenate([v_low[g][keys], v_high[g][keys]], axis=0), ones_ext], axis=1)
            o = jnp.dot(p[2 * g * w:(2 * g + 2) * w], vv, preferred_element_type=F32)
            o = o[:, :pair] / (o[:, pair:] + sink_term[2 * g * w:(2 * g + 2) * w])
            o_ref[rows, (2 * g) * pair:(2 * g + 1) * pair] = o[:w].astype(o_ref.dtype)
            o_ref[rows, (2 * g + 1) * pair:(2 * g + 2) * pair] = o[w:].astype(o_ref.dtype)


def _attn_prompt(q, k, v, sinks):
    n, t, _ = q.shape
    tt = ATTN_TIME_TILE
    cur = lambda width: pl.BlockSpec((None, tt, width), lambda b, i: (b, i, 0))
    prev = lambda width: pl.BlockSpec((None, WINDOW, width),
                                      lambda b, i: (b, jnp.maximum(i * (tt // WINDOW) - 1, 0), 0))
    return pl.pallas_call(
        _attn_prompt_body,
        grid=(n, t // tt),
        in_specs=[pl.BlockSpec(memory_space=pltpu.SMEM), cur(D_ATTN), cur(D_KV), prev(D_KV), cur(D_KV), prev(D_KV)],
        out_specs=cur(D_ATTN),
        out_shape=jax.ShapeDtypeStruct((n, t, D_ATTN), BF16),
        scratch_shapes=[pltpu.VMEM((N_HEADS // 2 * WINDOW, 4 * WINDOW), F32),
                        pltpu.VMEM((2, N_HEADS // 2 * WINDOW, 1), F32),
                        pltpu.VMEM((tt // WINDOW, N_HEADS // 2 * WINDOW, 4 * WINDOW), F32)],
        compiler_params=_params(2),
        name="attn_prompt",
    )(sinks, q, k, k, v, v)


def _pool_windows(ext_at, xn, pos):
    gdim = xn.shape[-1] // len(POOL_WINDOWS)
    parts = []
    for g, w in enumerate(POOL_WINDOWS):
        c0, c1 = g * gdim, (g + 1) * gdim
        acc = ext_at(0, c0, c1)
        for j in range(1, w):
            acc = acc + ext_at(j, c0, c1)
        cnt = jnp.minimum(w, pos + 1).astype(F32)
        parts.append(acc / cnt - xn[..., c0:c1])
    return parts


def _pool_prompt_body(x_ref, g_ref, w_ref, sc_ref, o_ref, tail_ref, ext_ref):
    tt, d = x_ref.shape
    i = pl.program_id(1)

    @pl.when(i == 0)
    def _():
        ext_ref[0:POOL_HALO, :] = jnp.zeros((POOL_HALO, d), F32)

    @pl.when(i > 0)
    def _():
        ext_ref[0:POOL_HALO, :] = ext_ref[tt:tt + POOL_HALO, :]

    x = x_ref[...]
    xn = _rmsnorm(x, g_ref[...])
    ext_ref[POOL_HALO:, :] = xn
    tail_ref[...] = xn[tt - POOL_HALO:, :]
    pos = i * tt + lax.broadcasted_iota(jnp.int32, (tt, 1), 0)
    parts = _pool_windows(lambda j, c0, c1: ext_ref[POOL_HALO - j:POOL_HALO - j + tt, c0:c1], xn, pos)
    ys = [jnp.dot(p.astype(BF16), w_ref[g], preferred_element_type=F32) for g, p in enumerate(parts)]
    o_ref[...] = x + jnp.concatenate(ys, axis=-1) * sc_ref[...]


def _pool_prompt(x, norm_g, w_pool, pool_scale, layer, i_odd):
    n, t, d = x.shape
    tt = POOL_TIME_TILE
    ng, gd, _ = w_pool.shape[1:]
    tile = pl.BlockSpec((None, tt, d), lambda b, i: (b, i, 0))
    return pl.pallas_call(
        _pool_prompt_body,
        grid=(n, t // tt),
        in_specs=[tile,
                  pl.BlockSpec((None, 1, d), lambda b, i: (layer, 0, 0)),
                  pl.BlockSpec((None, ng, gd, gd), lambda b, i: (i_odd, 0, 0, 0)),
                  pl.BlockSpec((None, 1, d), lambda b, i: (i_odd, 0, 0))],
        out_specs=[tile, pl.BlockSpec((None, POOL_HALO, d), lambda b, i: (b, 0, 0))],
        out_shape=[jax.ShapeDtypeStruct((n, t, d), F32), jax.ShapeDtypeStruct((n, POOL_HALO, d), F32)],
        scratch_shapes=[pltpu.VMEM((POOL_HALO + tt, d), F32)],
        compiler_params=_params(2),
        name="pool_prompt",
    )(x, norm_g, w_pool, pool_scale)


def _conv_sample_body(ext_ref, w_ref, b_ref, g_ref, beta_ref, o_ref):
    nb, _, c = ext_ref.shape
    t = o_ref.shape[1]
    acc = jnp.broadcast_to(b_ref[...].reshape(1, 1, c), (nb, t, c))
    for k in range(CONV_WIDTH):
        acc = acc + w_ref[k:k + 1, :].reshape(1, 1, c) * ext_ref[:, k:k + t, :]
    o_ref[...] = _layernorm_silu(acc, g_ref[...].reshape(1, 1, c), beta_ref[...].reshape(1, 1, c)).astype(o_ref.dtype)


def _conv_sample(ext, w_dw, b_dw, ln_g, ln_b, i_even, t):
    n, rows, c = ext.shape
    nb = SAMPLE_SEQ_TILE
    vec = pl.BlockSpec((None, 1, c), lambda b: (i_even, 0, 0))
    return pl.pallas_call(
        _conv_sample_body,
        grid=(n // nb,),
        in_specs=[pl.BlockSpec((nb, rows, c), lambda b: (b, 0, 0)),
                  pl.BlockSpec((None, CONV_WIDTH, c), lambda b: (i_even, 0, 0)),
                  vec, vec, vec],
        out_specs=pl.BlockSpec((nb, t, c), lambda b: (b, 0, 0)),
        out_shape=jax.ShapeDtypeStruct((n, t, c), BF16),
        compiler_params=_params(1),
        name="conv_sample",
    )(ext, w_dw, b_dw, ln_g, ln_b)


def _attn_sample_body(sink_ref, q_ref, k_ref, v_ref, o_ref, *, t, buf_len):
    kv = pl.program_id(0)
    nb, rows, _ = q_ref.shape
    keys = k_ref.shape[1]
    s = jnp.einsum("bqd,bkd->bqk", q_ref[...], k_ref[...], preferred_element_type=F32)
    row = lax.broadcasted_iota(jnp.int32, (rows, keys), 0)
    col = lax.broadcasted_iota(jnp.int32, (rows, keys), 1)
    dist = buf_len + row % t - col
    valid = (dist >= 0) & (dist < WINDOW)
    head_in_group = lax.broadcasted_iota(jnp.int32, (rows, 1), 0) // t
    slope = jnp.zeros((rows, 1), F32)
    sink = jnp.zeros((rows, 1), F32)
    for j in range(GQA_GROUP):
        slope_j = jnp.where(kv == 0, ALIBI_SLOPES[j], ALIBI_SLOPES[GQA_GROUP + j])
        slope = jnp.where(head_in_group == j, slope_j, slope)
        sink = jnp.where(head_in_group == j, sink_ref[kv * GQA_GROUP + j], sink)
    s = jnp.where(valid, s - slope * dist.astype(F32), NEG_INF)
    p, inv = _sink_softmax_weights(s, sink)
    o = jnp.einsum("bqk,bkd->bqd", p.astype(BF16), v_ref[...], preferred_element_type=F32) * inv
    o_ref[...] = o.astype(o_ref.dtype)


def _attn_sample(q, k2, v2, sinks, t, buf_len):
    nkv, n, rows, hd = q.shape
    keys = k2.shape[2]
    nb = SAMPLE_SEQ_TILE
    spec = lambda r: pl.BlockSpec((None, nb, r, hd), lambda g, b: (g, b, 0, 0))
    return pl.pallas_call(
        functools.partial(_attn_sample_body, t=t, buf_len=buf_len),
        grid=(nkv, n // nb),
        in_specs=[pl.BlockSpec(memory_space=pltpu.SMEM), spec(rows), spec(keys), spec(keys)],
        out_specs=spec(rows),
        out_shape=jax.ShapeDtypeStruct((nkv, n, rows, hd), BF16),
        compiler_params=_params(2),
        name="attn_sample",
    )(sinks, q, k2, v2)


def _pool_sample_body(x_ref, st_ref, g_ref, w_ref, sc_ref, o_ref, new_ref, ext_ref, *, pos0):
    nb, t, d = x_ref.shape
    x = x_ref[...]
    xn = _rmsnorm(x, g_ref[...].reshape(1, 1, d))
    ext_ref[:, POOL_HALO - POOL_BUF:POOL_HALO, :] = st_ref[...]
    ext_ref[:, POOL_HALO:, :] = xn
    new_ref[...] = ext_ref[:, POOL_HALO + t - POOL_BUF:, :]
    pos = pos0 + lax.broadcasted_iota(jnp.int32, (1, t, 1), 1)
    parts = _pool_windows(lambda j, c0, c1: ext_ref[:, POOL_HALO - j:POOL_HALO - j + t, c0:c1], xn, pos)
    ys = [jnp.dot(p.reshape(nb * t, -1).astype(BF16), w_ref[g], preferred_element_type=F32)
          for g, p in enumerate(parts)]
    y = jnp.concatenate(ys, axis=-1).reshape(nb, t, d)
    o_ref[...] = x + y * sc_ref[...].reshape(1, 1, d)


def _pool_sample(x, state, norm_g, w_pool, pool_scale, layer, i_odd, pos0):
    n, t, d = x.shape
    nb = SAMPLE_SEQ_TILE
    ng, gd, _ = w_pool.shape[1:]
    return pl.pallas_call(
        functools.partial(_pool_sample_body, pos0=pos0),
        grid=(n // nb,),
        in_specs=[pl.BlockSpec((nb, t, d), lambda b: (b, 0, 0)),
                  pl.BlockSpec((None, nb, POOL_BUF, d), lambda b: (i_odd, b, 0, 0)),
                  pl.BlockSpec((None, 1, d), lambda b: (layer, 0, 0)),
                  pl.BlockSpec((None, ng, gd, gd), lambda b: (i_odd, 0, 0, 0)),
                  pl.BlockSpec((None, 1, d), lambda b: (i_odd, 0, 0))],
        out_specs=[pl.BlockSpec((nb, t, d), lambda b: (b, 0, 0)),
                   pl.BlockSpec((nb, POOL_BUF, d), lambda b: (b, 0, 0))],
        out_shape=[jax.ShapeDtypeStruct((n, t, d), F32), jax.ShapeDtypeStruct((n, POOL_BUF, d), F32)],
        scratch_shapes=[pltpu.VMEM((nb, POOL_HALO + t, d), F32)],
        compiler_params=_params(1),
        name="pool_sample",
    )(x, state, norm_g, w_pool, pool_scale)


def _prepare_weights(norm_mix, norm_ffn, w_in, q_norm, k_norm, sinks, w_dw, b_dw, conv_norm_g, conv_norm_b, w_out,
                     w_pool, pool_scale, w_gate, w_up, w_down):
    vec = lambda a: a[:, None, :]
    return dict(
        norm_mix=vec(norm_mix), norm_ffn=vec(norm_ffn), w_in=w_in.astype(BF16),
        q_gain=vec(jnp.tile(q_norm, (1, N_HEADS))), k_gain=vec(jnp.tile(k_norm, (1, N_KV_HEADS))), sinks=sinks,
        w_dw=w_dw, b_dw=vec(b_dw),
        w_dw8=jnp.broadcast_to(w_dw[:, :, None, :], w_dw.shape[:2] + (8, w_dw.shape[2])),
        b_dw8=jnp.broadcast_to(b_dw[:, None, :], (b_dw.shape[0], 8, b_dw.shape[1])), ln_g=vec(conv_norm_g), ln_b=vec(conv_norm_b), w_out=w_out.astype(BF16),
        w_pool=w_pool.astype(BF16), pool_scale=vec(pool_scale),
        w_gate=w_gate.astype(BF16), w_up=w_up.astype(BF16), w_down=w_down.astype(BF16))


def _trunk_prompt(x, p, depth):
    n, t, d = x.shape
    xt = x.reshape(n * t, d)
    new_conv, new_k, new_v, new_pool = [], [], [], []
    for layer in range(depth):
        i = layer // 2
        if layer % 2 == 0:
            glu, q, k, v = _inproj(xt, p["norm_mix"], p["w_in"], p["q_gain"], p["k_gain"], layer, i)
            glu, q, k, v = (a.reshape(n, t, -1) for a in (glu, q, k, v))
            conv_out = _conv_prompt(glu, p["w_dw8"], p["b_dw8"], p["ln_g"], p["ln_b"], i)
            attn = _attn_prompt(q, k, v, p["sinks"][i])
            xt = _outproj(xt, conv_out.reshape(n * t, -1), attn.reshape(n * t, -1), p["w_out"], i)
            keep = min(WINDOW, t)
            new_conv.append(glu[:, t - (CONV_WIDTH - 1):])
            new_k.append(k[:, t - keep:].reshape(n, keep, N_KV_HEADS, HEAD_DIM))
            new_v.append(v[:, t - keep:].reshape(n, keep, N_KV_HEADS, HEAD_DIM))
        else:
            y, tail = _pool_prompt(xt.reshape(n, t, d), p["norm_mix"], p["w_pool"], p["pool_scale"], layer, i)
            xt = y.reshape(n * t, d)
            new_pool.append(tail[:, POOL_HALO - POOL_BUF:])
        xt = _ffn(xt, p["norm_ffn"], p["w_gate"], p["w_up"], p["w_down"], layer)
    return xt.reshape(n, t, d), jnp.stack(new_conv), jnp.stack(new_k), jnp.stack(new_v), jnp.stack(new_pool)


def _trunk_sample(x, conv_bufs, k_bufs, v_bufs, pool_bufs, pos0, p, depth):
    n, t, d = x.shape
    buf_len = k_bufs.shape[2]
    xt = x.reshape(n * t, d)
    new_conv, new_k, new_v, new_pool = [], [], [], []
    for layer in range(depth):
        i = layer // 2
        if layer % 2 == 0:
            glu, q, k, v = _inproj(xt, p["norm_mix"], p["w_in"], p["q_gain"], p["k_gain"], layer, i)
            ext = jnp.concatenate([conv_bufs[i], glu.reshape(n, t, D_CONV)], axis=1)
            conv_out = _conv_sample(ext, p["w_dw"], p["b_dw"], p["ln_g"], p["ln_b"], i, t)
            k2 = jnp.concatenate([k_bufs[i], k.reshape(n, t, N_KV_HEADS, HEAD_DIM)], axis=1)
            v2 = jnp.concatenate([v_bufs[i], v.reshape(n, t, N_KV_HEADS, HEAD_DIM)], axis=1)
            qh = q.reshape(n, t, N_KV_HEADS, GQA_GROUP, HEAD_DIM).transpose(2, 0, 3, 1, 4)
            qh = qh.reshape(N_KV_HEADS, n, GQA_GROUP * t, HEAD_DIM)
            kh = k2.astype(BF16).transpose(2, 0, 1, 3)
            vh = v2.astype(BF16).transpose(2, 0, 1, 3)
            oh = _attn_sample(qh, kh, vh, p["sinks"][i], t, buf_len)
            attn = oh.reshape(N_KV_HEADS, n, GQA_GROUP, t, HEAD_DIM).transpose(1, 3, 0, 2, 4).reshape(n * t, D_ATTN)
            xt = _outproj(xt, conv_out.reshape(n * t, D_CONV), attn, p["w_out"], i)
            new_conv.append(ext[:, t:])
            new_k.append(k2[:, t:])
            new_v.append(v2[:, t:])
        else:
            y, pool_new = _pool_sample(xt.reshape(n, t, d), pool_bufs, p["norm_mix"], p["w_pool"], p["pool_scale"],
                                       layer, i, pos0)
            xt = y.reshape(n * t, d)
            new_pool.append(pool_new)
        xt = _ffn(xt, p["norm_ffn"], p["w_gate"], p["w_up"], p["w_down"], layer)
    return xt.reshape(n, t, d), jnp.stack(new_conv), jnp.stack(new_k), jnp.stack(new_v), jnp.stack(new_pool)


def kernel(x_prompt, x_sample, cache_conv, cache_k, cache_v, state_pool, norm_mix, norm_ffn, w_in, q_norm, k_norm,
           sinks, w_dw, b_dw, conv_norm_g, conv_norm_b, w_out, w_pool, pool_scale, w_gate, w_up, w_down):
    depth = norm_mix.shape[0]
    p = _prepare_weights(norm_mix, norm_ffn, w_in, q_norm, k_norm, sinks, w_dw, b_dw, conv_norm_g, conv_norm_b, w_out,
                         w_pool, pool_scale, w_gate, w_up, w_down)
    y_p, conv_p, k_p, v_p, pool_p = _trunk_prompt(x_prompt, p, depth)
    y_s, conv_s, k_s, v_s, pool_s = _trunk_sample(x_sample, cache_conv, cache_k, cache_v, state_pool, PAST_LEN, p, depth)
    return (y_p, y_s, conv_p, k_p, v_p, pool_p, conv_s, k_s, v_s, pool_s)
```

```python
import functools

import jax
import jax.numpy as jnp
from jax import lax
from jax.experimental import pallas as pl
from jax.experimental.pallas import tpu as pltpu

F32 = jnp.float32
BF16 = jnp.bfloat16

HEAD_DIM = 64
N_HEADS = 8
N_KV_HEADS = 2
GQA_GROUP = N_HEADS // N_KV_HEADS
D_ATTN = N_HEADS * HEAD_DIM
D_KV = N_KV_HEADS * HEAD_DIM
WINDOW = 128
PAST_LEN = 8192
D_CONV = 512
CONV_WIDTH = 31
CONV_HALO = 32
POOL_WINDOWS = (2, 4, 8, 16)
POOL_BUF = max(POOL_WINDOWS) - 1
POOL_HALO = 16
RMS_EPS = 1e-6
LN_EPS = 1e-5
NEG_INF = -1e30
LANES = 128
ALIBI_SLOPES = tuple(2.0 ** (-8.0 * (h + 1) / N_HEADS) for h in range(N_HEADS))

TOKEN_TILE = 512
CONV_TIME_TILE = 512
CONV_PHASES = 4
ATTN_TIME_TILE = 512
POOL_TIME_TILE = 512
POOL_PHASES = 4
FFN_CHUNK = 1536
SAMPLE_SEQ_TILE = 32
SAMPLE_FFN_SEQ_TILE = 64
VMEM_LIMIT = 56 * 1024 * 1024


def _params(n_axes):
    return pltpu.CompilerParams(dimension_semantics=("arbitrary",) * n_axes, vmem_limit_bytes=VMEM_LIMIT)


def _rmsnorm(x, g):
    return x * lax.rsqrt(jnp.mean(x * x, axis=-1, keepdims=True) + RMS_EPS) * g


def _head_rmsnorm(x, g):
    lane = lax.broadcasted_iota(jnp.int32, (x.shape[0], LANES), 1)
    low = lane < HEAD_DIM
    outs = []
    for s in range(x.shape[1] // LANES):
        xs = x[:, s * LANES:(s + 1) * LANES]
        sq = xs * xs
        s_low = jnp.sum(jnp.where(low, sq, 0.0), axis=-1, keepdims=True)
        s_high = jnp.sum(jnp.where(low, 0.0, sq), axis=-1, keepdims=True)
        ms = jnp.where(low, s_low, s_high) * (1.0 / HEAD_DIM)
        outs.append(xs * lax.rsqrt(ms + RMS_EPS) * g[:, s * LANES:(s + 1) * LANES])
    return outs[0] if len(outs) == 1 else jnp.concatenate(outs, axis=-1)


def _layernorm_silu(y, g, b):
    mu = jnp.mean(y, axis=-1, keepdims=True)
    yc = y - mu
    var = jnp.mean(yc * yc, axis=-1, keepdims=True)
    z = yc * lax.rsqrt(var + LN_EPS) * g + b
    return z * jax.nn.sigmoid(z)


def _ffn_apply(x, g_ref, wg_ref, wu_ref, wd_ref):
    h = _rmsnorm(x, g_ref[...]).astype(BF16)
    acc = x
    d_ff = wg_ref.shape[1]
    for c0 in range(0, d_ff, FFN_CHUNK):
        c1 = min(c0 + FFN_CHUNK, d_ff)
        gate = jnp.dot(h, wg_ref[:, c0:c1], preferred_element_type=F32)
        up = jnp.dot(h, wu_ref[:, c0:c1], preferred_element_type=F32)
        a = (gate * jax.nn.sigmoid(gate) * up).astype(BF16)
        acc = acc + jnp.dot(a, wd_ref[c0:c1, :], preferred_element_type=F32)
    return acc


def _ffn_specs(w_gate, layer, n_axes):
    _, d, f = w_gate.shape
    at_layer = {1: lambda i: (layer, 0, 0), 2: lambda b, i: (layer, 0, 0)}[n_axes]
    resident = functools.partial(pl.BlockSpec, pipeline_mode=pl.Buffered(1))
    return [pl.BlockSpec((None, 1, d), at_layer), resident((None, d, f), at_layer), resident((None, d, f), at_layer),
            resident((None, f, d), at_layer)]


def _inproj_body(x_ref, g_ref, w_ref, qn_ref, kn_ref, glu_ref, q_ref, k_ref, v_ref):
    h = _rmsnorm(x_ref[...], g_ref[...]).astype(BF16)
    u = jnp.dot(h, w_ref[...], preferred_element_type=F32)
    o_q = 2 * D_CONV
    o_k = o_q + D_ATTN
    o_v = o_k + D_KV
    glu_ref[...] = u[:, :D_CONV] * jax.nn.sigmoid(u[:, D_CONV:o_q])
    q_ref[...] = (_head_rmsnorm(u[:, o_q:o_k], qn_ref[...]) * (HEAD_DIM ** -0.5)).astype(BF16)
    k_ref[...] = _head_rmsnorm(u[:, o_k:o_v], kn_ref[...])
    v_ref[...] = u[:, o_v:]


def _inproj(x, norm_g, w_in, q_gain, k_gain, layer, i_even):
    t, d = x.shape
    n = w_in.shape[2]
    row = lambda width: pl.BlockSpec((TOKEN_TILE, width), lambda i: (i, 0))
    return pl.pallas_call(
        _inproj_body,
        grid=(t // TOKEN_TILE,),
        in_specs=[row(d),
                  pl.BlockSpec((None, 1, d), lambda i: (layer, 0, 0)),
                  pl.BlockSpec((None, d, n), lambda i: (i_even, 0, 0), pipeline_mode=pl.Buffered(1)),
                  pl.BlockSpec((None, 1, D_ATTN), lambda i: (i_even, 0, 0)),
                  pl.BlockSpec((None, 1, D_KV), lambda i: (i_even, 0, 0))],
        out_specs=[row(D_CONV), row(D_ATTN), row(D_KV), row(D_KV)],
        out_shape=[jax.ShapeDtypeStruct((t, D_CONV), F32), jax.ShapeDtypeStruct((t, D_ATTN), BF16),
                   jax.ShapeDtypeStruct((t, D_KV), F32), jax.ShapeDtypeStruct((t, D_KV), F32)],
        compiler_params=_params(1),
        name="inproj",
    )(x, norm_g, w_in, q_gain, k_gain)


def _outproj_ffn_body(x_ref, c_ref, a_ref, w_ref, gf_ref, wg_ref, wu_ref, wd_ref, o_ref):
    y = jnp.dot(c_ref[...], w_ref[:D_CONV, :], preferred_element_type=F32)
    y = y + jnp.dot(a_ref[...], w_ref[D_CONV:, :], preferred_element_type=F32)
    o_ref[...] = _ffn_apply(x_ref[...] + y, gf_ref, wg_ref, wu_ref, wd_ref)


def _outproj_ffn(x, conv_out, attn, w_out, i_even, norm_ffn, w_gate, w_up, w_down, layer):
    t, d = x.shape
    row = lambda width: pl.BlockSpec((TOKEN_TILE, width), lambda i: (i, 0))
    return pl.pallas_call(
        _outproj_ffn_body,
        grid=(t // TOKEN_TILE,),
        in_specs=[row(d), row(D_CONV), row(D_ATTN),
                  pl.BlockSpec((None, D_CONV + D_ATTN, d), lambda i: (i_even, 0, 0), pipeline_mode=pl.Buffered(1))]
        + _ffn_specs(w_gate, layer, 1),
        out_specs=row(d),
        out_shape=jax.ShapeDtypeStruct((t, d), F32),
        compiler_params=_params(1),
        name="outproj_ffn",
    )(x, conv_out, attn, w_out, norm_ffn, w_gate, w_up, w_down)


def _depthwise_conv_slab(ext_ref, y_ref, w_ref, b_ref, slab, n_rows):
    lead = CONV_HALO - (CONV_WIDTH - 1)
    rows = 8 * CONV_PHASES
    lanes = slice(slab * LANES, (slab + 1) * LANES)

    def chunk(c, carry):
        r0 = c * rows
        acc = [b_ref[:, lanes]] * CONV_PHASES
        for e in range(CONV_WIDTH + CONV_PHASES - 1):
            v = ext_ref[slab, pl.ds(r0 + lead + e, 8, stride=CONV_PHASES), :]
            for ph in range(CONV_PHASES):
                k = e - ph
                if 0 <= k < CONV_WIDTH:
                    acc[ph] = acc[ph] + w_ref[k, :, lanes] * v
        for ph in range(CONV_PHASES):
            y_ref[slab, pl.ds(r0 + ph, 8, stride=CONV_PHASES), :] = acc[ph]
        return carry

    lax.fori_loop(0, n_rows // rows, chunk, 0, unroll=2)


def _layernorm_silu_slabs(y_ref, g_ref, beta_ref, o_ref):
    n_slabs = y_ref.shape[0]
    ys = [y_ref[s] for s in range(n_slabs)]
    inv_c = 1.0 / (n_slabs * LANES)
    mu = jnp.sum(sum(ys), axis=-1, keepdims=True) * inv_c
    ycs = [y - mu for y in ys]
    var = jnp.sum(sum(yc * yc for yc in ycs), axis=-1, keepdims=True) * inv_c
    rstd = lax.rsqrt(var + LN_EPS)
    for s, yc in enumerate(ycs):
        lanes = slice(s * LANES, (s + 1) * LANES)
        z = yc * rstd * g_ref[:, lanes] + beta_ref[:, lanes]
        o_ref[:, lanes] = (z * jax.nn.sigmoid(z)).astype(o_ref.dtype)


def _conv_prompt_body(cur_ref, prev_ref, w_ref, b_ref, g_ref, beta_ref, o_ref, ext_ref, y_ref):
    tt = cur_ref.shape[0]
    first = pl.program_id(1) == 0
    for s in range(D_CONV // LANES):
        lanes = slice(s * LANES, (s + 1) * LANES)
        ext_ref[s, 0:CONV_HALO, :] = jnp.where(first, 0.0, prev_ref[:, lanes])
        ext_ref[s, CONV_HALO:, :] = cur_ref[:, lanes]
    for s in range(D_CONV // LANES):
        _depthwise_conv_slab(ext_ref, y_ref, w_ref, b_ref, s, tt)
    _layernorm_silu_slabs(y_ref, g_ref, beta_ref, o_ref)


def _conv_prompt(glu, w_dw8, b_dw8, ln_g, ln_b, i_even):
    n, t, c = glu.shape
    tt = CONV_TIME_TILE
    halo_per_tile = tt // CONV_HALO
    vec = pl.BlockSpec((None, 1, c), lambda b, i: (i_even, 0, 0))
    return pl.pallas_call(
        _conv_prompt_body,
        grid=(n, t // tt),
        in_specs=[pl.BlockSpec((None, tt, c), lambda b, i: (b, i, 0)),
                  pl.BlockSpec((None, CONV_HALO, c), lambda b, i: (b, jnp.maximum(i * halo_per_tile - 1, 0), 0)),
                  pl.BlockSpec((None, CONV_WIDTH, 8, c), lambda b, i: (i_even, 0, 0, 0)),
                  pl.BlockSpec((None, 8, c), lambda b, i: (i_even, 0, 0)),
                  vec, vec],
        out_specs=pl.BlockSpec((None, tt, c), lambda b, i: (b, i, 0)),
        out_shape=jax.ShapeDtypeStruct((n, t, c), BF16),
        scratch_shapes=[pltpu.VMEM((c // LANES, CONV_HALO + tt, LANES), F32), pltpu.VMEM((c // LANES, tt, LANES), F32)],
        compiler_params=_params(2),
        name="conv_prompt",
    )(glu, glu, w_dw8, b_dw8, ln_g, ln_b)


def _attn_prompt_body(sink_ref, q_ref, kc_ref, kp_ref, vc_ref, vp_ref, o_ref, bias_ref, sinkcol_ref, s_ref):
    b = pl.program_id(0)
    i = pl.program_id(1)
    w = WINDOW
    pair = 2 * HEAD_DIM
    n_blocks = q_ref.shape[0] // w
    n_slabs = D_ATTN // pair
    lane = lax.broadcasted_iota(jnp.int32, (1, pair), 1)
    low = lane < HEAD_DIM

    @pl.when((b == 0) & (i == 0))
    def _():
        row = lax.broadcasted_iota(jnp.int32, (w, 2 * w), 0)
        col = lax.broadcasted_iota(jnp.int32, (w, 2 * w), 1)
        dist = w + row - col
        valid = (dist >= 0) & (dist < w)
        distf = dist.astype(F32)
        for h in range(N_HEADS):
            slab, half = h // 2, h % 2
            bias_ref[slab * w:(slab + 1) * w, half * 2 * w:(half + 1) * 2 * w] = (
                jnp.where(valid, -ALIBI_SLOPES[h] * distf, NEG_INF))
            sinkcol_ref[half, slab * w:(slab + 1) * w, :] = jnp.full((w, 1), sink_ref[h], F32)

    kcat = jnp.concatenate([kp_ref[...], kc_ref[...]], axis=0)
    vcat = jnp.concatenate([vp_ref[...], vc_ref[...]], axis=0)
    krot = pltpu.roll(kcat, HEAD_DIM, 1)
    vrot = pltpu.roll(vcat, HEAD_DIM, 1)
    zero = jnp.zeros((), F32)
    k_low = [jnp.where(low, kcat, zero).astype(BF16), jnp.where(low, krot, zero).astype(BF16)]
    k_high = [jnp.where(low, zero, krot).astype(BF16), jnp.where(low, zero, kcat).astype(BF16)]
    v_low = [jnp.where(low, vcat, zero).astype(BF16), jnp.where(low, vrot, zero).astype(BF16)]
    v_high = [jnp.where(low, zero, vrot).astype(BF16), jnp.where(low, zero, vcat).astype(BF16)]
    ones_low = jnp.broadcast_to(jnp.where(low, 1.0, 0.0).astype(BF16), (2 * w, pair))
    ones_high = jnp.broadcast_to(jnp.where(low, 0.0, 1.0).astype(BF16), (2 * w, pair))
    ones_ext = jnp.concatenate([ones_low, ones_high], axis=0)
    col = lax.broadcasted_iota(jnp.int32, (1, 4 * w), 1)
    no_prev = (col % (2 * w) < w) & (i == 0)

    for jb in range(n_blocks):
        rows = slice(jb * w, (jb + 1) * w)
        keys = slice(jb * w, (jb + 2) * w)
        for g in range(N_KV_HEADS):
            q2 = jnp.concatenate([q_ref[rows, (2 * g) * pair:(2 * g + 1) * pair],
                                  q_ref[rows, (2 * g + 1) * pair:(2 * g + 2) * pair]], axis=0)
            kk = jnp.concatenate([k_low[g][keys], k_high[g][keys]], axis=0)
            s_ref[jb, 2 * g * w:(2 * g + 2) * w, :] = lax.dot_general(
                q2, kk, (((1,), (1,)), ((), ())), preferred_element_type=F32)

    for jb in range(n_blocks):
        rows = slice(jb * w, (jb + 1) * w)
        keys = slice(jb * w, (jb + 2) * w)
        bias = bias_ref[...]
        if jb == 0:
            bias = jnp.where(no_prev, NEG_INF, bias)
        s = s_ref[jb] + bias
        ps, sink_terms = [], []
        for half in range(2):
            sh = s[:, half * 2 * w:(half + 1) * 2 * w]
            sink = sinkcol_ref[half]
            m = jnp.maximum(jnp.max(sh, axis=-1, keepdims=True), sink)
            ps.append(jnp.exp(sh - m).astype(BF16))
            sink_terms.append(jnp.exp(sink - m))
        p = jnp.concatenate(ps, axis=1)
        sink_term = jnp.where(low, sink_terms[0], sink_terms[1])
        for g in range(N_KV_HEADS):
            vv = jnp.concatenate([jnp.concatenate([v_low[g][keys], v_high[g][keys]], axis=0), ones_ext], axis=1)
            o = jnp.dot(p[2 * g * w:(2 * g + 2) * w], vv, preferred_element_type=F32)
            o = o[:, :pair] / (o[:, pair:] + sink_term[2 * g * w:(2 * g + 2) * w])
            o_ref[rows, (2 * g) * pair:(2 * g + 1) * pair] = o[:w].astype(o_ref.dtype)
            o_ref[rows, (2 * g + 1) * pair:(2 * g + 2) * pair] = o[w:].astype(o_ref.dtype)


def _attn_prompt(q, k, v, sinks):
    n, t, _ = q.shape
    tt = ATTN_TIME_TILE
    cur = lambda width: pl.BlockSpec((None, tt, width), lambda b, i: (b, i, 0))
    prev = lambda width: pl.BlockSpec((None, WINDOW, width),
                                      lambda b, i: (b, jnp.maximum(i * (tt // WINDOW) - 1, 0), 0))
    return pl.pallas_call(
        _attn_prompt_body,
        grid=(n, t // tt),
        in_specs=[pl.BlockSpec(memory_space=pltpu.SMEM), cur(D_ATTN), cur(D_KV), prev(D_KV), cur(D_KV), prev(D_KV)],
        out_specs=cur(D_ATTN),
        out_shape=jax.ShapeDtypeStruct((n, t, D_ATTN), BF16),
        scratch_shapes=[pltpu.VMEM((N_HEADS // 2 * WINDOW, 4 * WINDOW), F32),
                        pltpu.VMEM((2, N_HEADS // 2 * WINDOW, 1), F32),
                        pltpu.VMEM((tt // WINDOW, N_HEADS // 2 * WINDOW, 4 * WINDOW), F32)],
        compiler_params=_params(2),
        name="attn_prompt",
    )(sinks, q, k, k, v, v)


def _window_sums(vals, w, n_out):
    shared = list(range(n_out - 1, w))
    base = None
    for e in shared:
        base = vals[e] if base is None else base + vals[e]
    sums = []
    for i in range(n_out):
        acc = base
        for e in range(i, i + w):
            if e not in shared:
                acc = vals[e] if acc is None else acc + vals[e]
        sums.append(acc)
    return sums


def _pool_deltas_strided(ext_ref, d_ref, pos0, n_rows):
    n_slabs = ext_ref.shape[0]
    slabs_per_group = n_slabs // len(POOL_WINDOWS)
    rows = 8 * POOL_PHASES

    def chunk(c, carry):
        r0 = c * rows
        sub = lax.broadcasted_iota(jnp.int32, (8, LANES), 0)
        pos = [pos0 + r0 + ph + POOL_PHASES * sub for ph in range(POOL_PHASES)]
        inv = {w: [1.0 / jnp.minimum(w, p + 1).astype(F32) for p in pos] for w in POOL_WINDOWS}
        for slab in range(n_slabs):
            w = POOL_WINDOWS[slab // slabs_per_group]
            first = POOL_HALO - (w - 1)
            vals = [ext_ref[slab, pl.ds(r0 + first + e, 8, stride=POOL_PHASES), :] for e in range(w + POOL_PHASES - 1)]
            sums = _window_sums(vals, w, POOL_PHASES)
            for ph in range(POOL_PHASES):
                d_ref[slab, pl.ds(r0 + ph, 8, stride=POOL_PHASES), :] = sums[ph] * inv[w][ph] - vals[ph + w - 1]
        return carry

    lax.fori_loop(0, n_rows // rows, chunk, 0)


def _pool_ffn_prompt_body(x_ref, g_ref, w_ref, sc_ref, gf_ref, wg_ref, wu_ref, wd_ref, o_ref, tail_ref, ext_ref, d_ref):
    tt, d = x_ref.shape
    n_slabs = d // LANES
    slabs_per_group = n_slabs // len(POOL_WINDOWS)
    i = pl.program_id(1)

    @pl.when(i == 0)
    def _():
        ext_ref[:, 0:POOL_HALO, :] = jnp.zeros((n_slabs, POOL_HALO, LANES), F32)

    @pl.when(i > 0)
    def _():
        ext_ref[:, 0:POOL_HALO, :] = ext_ref[:, tt:tt + POOL_HALO, :]

    x = x_ref[...]
    xn = _rmsnorm(x, g_ref[...])
    for s in range(n_slabs):
        ext_ref[s, POOL_HALO:, :] = xn[:, s * LANES:(s + 1) * LANES]
    tail_ref[...] = xn[tt - POOL_HALO:, :]
    _pool_deltas_strided(ext_ref, d_ref, i * tt, tt)
    ys = []
    for g in range(len(POOL_WINDOWS)):
        dg = jnp.concatenate([d_ref[s] for s in range(g * slabs_per_group, (g + 1) * slabs_per_group)], axis=-1)
        ys.append(jnp.dot(dg.astype(BF16), w_ref[g], preferred_element_type=F32))
    x1 = x + jnp.concatenate(ys, axis=-1) * sc_ref[...]
    o_ref[...] = _ffn_apply(x1, gf_ref, wg_ref, wu_ref, wd_ref)


def _pool_ffn_prompt(x, norm_g, w_pool, pool_scale, layer, i_odd, norm_ffn, w_gate, w_up, w_down):
    n, t, d = x.shape
    tt = POOL_TIME_TILE
    ng, gd, _ = w_pool.shape[1:]
    tile = pl.BlockSpec((None, tt, d), lambda b, i: (b, i, 0))
    return pl.pallas_call(
        _pool_ffn_prompt_body,
        grid=(n, t // tt),
        in_specs=[tile,
                  pl.BlockSpec((None, 1, d), lambda b, i: (layer, 0, 0)),
                  pl.BlockSpec((None, ng, gd, gd), lambda b, i: (i_odd, 0, 0, 0)),
                  pl.BlockSpec((None, 1, d), lambda b, i: (i_odd, 0, 0))] + _ffn_specs(w_gate, layer, 2),
        out_specs=[tile, pl.BlockSpec((None, POOL_HALO, d), lambda b, i: (b, 0, 0))],
        out_shape=[jax.ShapeDtypeStruct((n, t, d), F32), jax.ShapeDtypeStruct((n, POOL_HALO, d), F32)],
        scratch_shapes=[pltpu.VMEM((d // LANES, POOL_HALO + tt, LANES), F32), pltpu.VMEM((d // LANES, tt, LANES), F32)],
        compiler_params=_params(2),
        name="pool_ffn_prompt",
    )(x, norm_g, w_pool, pool_scale, norm_ffn, w_gate, w_up, w_down)


def _conv_sample_body(cache_ref, glu_ref, w_ref, b_ref, g_ref, beta_ref, o_ref, new_ref):
    hist = cache_ref.shape[0]
    t = glu_ref.shape[0]
    ext = lambda j: cache_ref[j] if j < hist else glu_ref[j - hist]
    for j in range(hist):
        new_ref[j] = ext(t + j)
    for step in range(t):
        acc = w_ref[0:1, :] * ext(step) + b_ref[...]
        for k in range(1, CONV_WIDTH):
            acc = acc + w_ref[k:k + 1, :] * ext(step + k)
        o_ref[step] = _layernorm_silu(acc, g_ref[...], beta_ref[...]).astype(o_ref.dtype)


def _conv_sample(cache_tm, glu_tm, w_dw, b_dw, ln_g, ln_b, i_even):
    _, hist, n, c = cache_tm.shape
    t = glu_tm.shape[0]
    nb = SAMPLE_SEQ_TILE
    vec = pl.BlockSpec((None, 1, c), lambda b: (i_even, 0, 0))
    return pl.pallas_call(
        _conv_sample_body,
        grid=(n // nb,),
        in_specs=[pl.BlockSpec((None, hist, nb, c), lambda b: (i_even, 0, b, 0)),
                  pl.BlockSpec((t, nb, c), lambda b: (0, b, 0)),
                  pl.BlockSpec((None, CONV_WIDTH, c), lambda b: (i_even, 0, 0)),
                  vec, vec, vec],
        out_specs=[pl.BlockSpec((t, nb, c), lambda b: (0, b, 0)), pl.BlockSpec((hist, nb, c), lambda b: (0, b, 0))],
        out_shape=[jax.ShapeDtypeStruct((t, n, c), BF16), jax.ShapeDtypeStruct((hist, n, c), F32)],
        compiler_params=_params(1),
        name="conv_sample",
    )(cache_tm, glu_tm, w_dw, b_dw, ln_g, ln_b)


def _attn_sample_body(sink_ref, q_ref, kc_ref, kn_ref, vc_ref, vn_ref, o_ref, knew_ref, vnew_ref):
    nb, nkv, rows, hd = q_ref.shape
    buf = kc_ref.shape[-1]
    t = kn_ref.shape[-1]
    batch = nb * nkv
    q = q_ref[...].reshape(batch, rows, hd)
    kc = kc_ref[...].reshape(batch, hd, buf)
    kn = kn_ref[...].reshape(batch, hd, t)
    vc = vc_ref[...].reshape(batch, hd, buf)
    vn = vn_ref[...].reshape(batch, hd, t)

    lane = lax.broadcasted_iota(jnp.int32, (1, 1, buf), 2)
    pad = jnp.zeros((batch, hd, buf - t), F32)
    k_new = jnp.where(lane < buf - t, pltpu.roll(kc, buf - t, 2), jnp.concatenate([pad, kn], axis=-1))
    v_new = jnp.where(lane < buf - t, pltpu.roll(vc, buf - t, 2), jnp.concatenate([pad, vn], axis=-1))
    knew_ref[...] = k_new.reshape(nb, nkv, hd, buf)
    vnew_ref[...] = v_new.reshape(nb, nkv, hd, buf)

    s_c = jnp.einsum("bqd,bdk->bqk", q, kc.astype(BF16), preferred_element_type=F32)
    s_n = jnp.einsum("bqd,bdk->bqk", q, kn.astype(BF16), preferred_element_type=F32)
    step = lax.broadcasted_iota(jnp.int32, (rows, 1), 0) % t
    head_in_group = lax.broadcasted_iota(jnp.int32, (rows, 1), 0) // t
    dist_c = buf + step - lax.broadcasted_iota(jnp.int32, (rows, buf), 1)
    dist_n = step - lax.broadcasted_iota(jnp.int32, (rows, t), 1)
    s_c = s_c.reshape(nb, nkv, rows, buf)
    s_n = s_n.reshape(nb, nkv, rows, t)
    ps_c, ps_n, invs = [], [], []
    for kv in range(nkv):
        slope = jnp.zeros((rows, 1), F32)
        sink = jnp.zeros((rows, 1), F32)
        for j in range(GQA_GROUP):
            slope = jnp.where(head_in_group == j, ALIBI_SLOPES[kv * GQA_GROUP + j], slope)
            sink = jnp.where(head_in_group == j, sink_ref[kv * GQA_GROUP + j], sink)
        sc = jnp.where((dist_c >= 0) & (dist_c < WINDOW), s_c[:, kv] - slope * dist_c.astype(F32), NEG_INF)
        sn = jnp.where((dist_n >= 0) & (dist_n < WINDOW), s_n[:, kv] - slope * dist_n.astype(F32), NEG_INF)
        m = jnp.maximum(jnp.maximum(jnp.max(sc, axis=-1, keepdims=True), jnp.max(sn, axis=-1, keepdims=True)), sink)
        pc = jnp.exp(sc - m)
        pn = jnp.exp(sn - m)
        denom = jnp.sum(pc, axis=-1, keepdims=True) + jnp.sum(pn, axis=-1, keepdims=True) + jnp.exp(sink - m)
        ps_c.append(pc.astype(BF16))
        ps_n.append(pn.astype(BF16))
        invs.append(jnp.broadcast_to(1.0 / denom, (nb, rows, 1)))
    p_c = jnp.stack(ps_c, axis=1).reshape(batch, rows, buf)
    p_n = jnp.stack(ps_n, axis=1).reshape(batch, rows, t)
    inv = jnp.stack(invs, axis=1).reshape(batch, rows, 1)
    o = jnp.einsum("bqk,bdk->bqd", p_c, vc.astype(BF16), preferred_element_type=F32)
    o = o + jnp.einsum("bqk,bdk->bqd", p_n, vn.astype(BF16), preferred_element_type=F32)
    o_ref[...] = (o * inv).reshape(nb, nkv, rows, hd).astype(o_ref.dtype)


def _attn_sample(q, kc, kn, vc, vn, sinks, i_even):
    n, nkv, rows, hd = q.shape
    buf = kc.shape[-1]
    t = kn.shape[-1]
    nb = SAMPLE_SEQ_TILE
    cache = pl.BlockSpec((None, nb, nkv, hd, buf), lambda b: (i_even, b, 0, 0, 0))
    new = pl.BlockSpec((nb, nkv, hd, t), lambda b: (b, 0, 0, 0))
    state = pl.BlockSpec((nb, nkv, hd, buf), lambda b: (b, 0, 0, 0))
    qspec = pl.BlockSpec((nb, nkv, rows, hd), lambda b: (b, 0, 0, 0))
    return pl.pallas_call(
        _attn_sample_body,
        grid=(n // nb,),
        in_specs=[pl.BlockSpec(memory_space=pltpu.SMEM), qspec, cache, new, cache, new],
        out_specs=[qspec, state, state],
        out_shape=[jax.ShapeDtypeStruct((n, nkv, rows, hd), BF16), jax.ShapeDtypeStruct((n, nkv, hd, buf), F32),
                   jax.ShapeDtypeStruct((n, nkv, hd, buf), F32)],
        compiler_params=_params(1),
        name="attn_sample",
    )(sinks, q, kc, kn, vc, vn)


def _pool_ffn_sample_body(x_ref, st_ref, g_ref, w_ref, sc_ref, gf_ref, wg_ref, wu_ref, wd_ref, o_ref, new_ref, *, pos0):
    t, nb, d = x_ref.shape
    hist = st_ref.shape[0]
    gdim = d // len(POOL_WINDOWS)
    x = x_ref[...].reshape(t * nb, d)
    xn = _rmsnorm(x, g_ref[...])
    ext = lambda j: st_ref[j] if j < hist else xn[(j - hist) * nb:(j - hist + 1) * nb]
    for j in range(hist):
        new_ref[j] = ext(t + j)
    ys = []
    for g, w in enumerate(POOL_WINDOWS):
        lanes = slice(g * gdim, (g + 1) * gdim)
        steps = []
        for step in range(t):
            acc = ext(hist + step)[:, lanes]
            for j in range(1, w):
                acc = acc + ext(hist + step - j)[:, lanes]
            cnt = float(min(w, pos0 + step + 1))
            steps.append(acc / cnt - xn[step * nb:(step + 1) * nb, lanes])
        dg = jnp.concatenate(steps, axis=0)
        ys.append(jnp.dot(dg.astype(BF16), w_ref[g], preferred_element_type=F32))
    x1 = x + jnp.concatenate(ys, axis=-1) * sc_ref[...]
    o_ref[...] = _ffn_apply(x1, gf_ref, wg_ref, wu_ref, wd_ref).reshape(t, nb, d)


def _pool_ffn_sample(x_tm, state_tm, norm_g, w_pool, pool_scale, layer, i_odd, pos0, norm_ffn, w_gate, w_up, w_down):
    t, n, d = x_tm.shape
    hist = state_tm.shape[1]
    nb = SAMPLE_FFN_SEQ_TILE
    ng, gd, _ = w_pool.shape[1:]
    return pl.pallas_call(
        functools.partial(_pool_ffn_sample_body, pos0=pos0),
        grid=(n // nb,),
        in_specs=[pl.BlockSpec((t, nb, d), lambda b: (0, b, 0)),
                  pl.BlockSpec((None, hist, nb, d), lambda b: (i_odd, 0, b, 0)),
                  pl.BlockSpec((None, 1, d), lambda b: (layer, 0, 0)),
                  pl.BlockSpec((None, ng, gd, gd), lambda b: (i_odd, 0, 0, 0)),
                  pl.BlockSpec((None, 1, d), lambda b: (i_odd, 0, 0))] + _ffn_specs(w_gate, layer, 1),
        out_specs=[pl.BlockSpec((t, nb, d), lambda b: (0, b, 0)), pl.BlockSpec((hist, nb, d), lambda b: (0, b, 0))],
        out_shape=[jax.ShapeDtypeStruct((t, n, d), F32), jax.ShapeDtypeStruct((hist, n, d), F32)],
        compiler_params=_params(1),
        name="pool_ffn_sample",
    )(x_tm, state_tm, norm_g, w_pool, pool_scale, norm_ffn, w_gate, w_up, w_down)


def _prepare_weights(norm_mix, norm_ffn, w_in, q_norm, k_norm, sinks, w_dw, b_dw, conv_norm_g, conv_norm_b, w_out,
                     w_pool, pool_scale, w_gate, w_up, w_down):
    vec = lambda a: a[:, None, :]
    return dict(
        norm_mix=vec(norm_mix), norm_ffn=vec(norm_ffn), w_in=w_in.astype(BF16),
        q_gain=vec(jnp.tile(q_norm, (1, N_HEADS))), k_gain=vec(jnp.tile(k_norm, (1, N_KV_HEADS))), sinks=sinks,
        w_dw=w_dw, b_dw=vec(b_dw),
        w_dw8=jnp.broadcast_to(w_dw[:, :, None, :], w_dw.shape[:2] + (8, w_dw.shape[2])),
        b_dw8=jnp.broadcast_to(b_dw[:, None, :], (b_dw.shape[0], 8, b_dw.shape[1])), ln_g=vec(conv_norm_g), ln_b=vec(conv_norm_b), w_out=w_out.astype(BF16),
        w_pool=w_pool.astype(BF16), pool_scale=vec(pool_scale),
        w_gate=w_gate.astype(BF16), w_up=w_up.astype(BF16), w_down=w_down.astype(BF16))


def _trunk_prompt(x, p, depth):
    n, t, d = x.shape
    xt = x.reshape(n * t, d)
    ffn = lambda layer: (p["norm_ffn"], p["w_gate"], p["w_up"], p["w_down"], layer)
    new_conv, new_k, new_v, new_pool = [], [], [], []
    for layer in range(depth):
        i = layer // 2
        if layer % 2 == 0:
            glu, q, k, v = _inproj(xt, p["norm_mix"], p["w_in"], p["q_gain"], p["k_gain"], layer, i)
            glu, q, k, v = (a.reshape(n, t, -1) for a in (glu, q, k, v))
            conv_out = _conv_prompt(glu, p["w_dw8"], p["b_dw8"], p["ln_g"], p["ln_b"], i)
            attn = _attn_prompt(q, k, v, p["sinks"][i])
            xt = _outproj_ffn(xt, conv_out.reshape(n * t, -1), attn.reshape(n * t, -1), p["w_out"], i, *ffn(layer))
            keep = min(WINDOW, t)
            new_conv.append(glu[:, t - (CONV_WIDTH - 1):])
            new_k.append(k[:, t - keep:].reshape(n, keep, N_KV_HEADS, HEAD_DIM))
            new_v.append(v[:, t - keep:].reshape(n, keep, N_KV_HEADS, HEAD_DIM))
        else:
            y, tail = _pool_ffn_prompt(xt.reshape(n, t, d), p["norm_mix"], p["w_pool"], p["pool_scale"], layer, i,
                                       *ffn(layer)[:-1])
            xt = y.reshape(n * t, d)
            new_pool.append(tail[:, POOL_HALO - POOL_BUF:])
    return xt.reshape(n, t, d), jnp.stack(new_conv), jnp.stack(new_k), jnp.stack(new_v), jnp.stack(new_pool)


def _trunk_sample(x, conv_bufs, k_bufs, v_bufs, pool_bufs, pos0, p, depth):
    n, t, d = x.shape
    x_tm = x.transpose(1, 0, 2)
    conv_tm = conv_bufs.transpose(0, 2, 1, 3)
    pool_tm = pool_bufs.transpose(0, 2, 1, 3)
    k_t = k_bufs.transpose(0, 1, 3, 4, 2)
    v_t = v_bufs.transpose(0, 1, 3, 4, 2)
    ffn = lambda layer: (p["norm_ffn"], p["w_gate"], p["w_up"], p["w_down"], layer)
    new_conv, new_k, new_v, new_pool = [], [], [], []
    for layer in range(depth):
        i = layer // 2
        if layer % 2 == 0:
            xt = x_tm.reshape(t * n, d)
            glu, q, k, v = _inproj(xt, p["norm_mix"], p["w_in"], p["q_gain"], p["k_gain"], layer, i)
            conv_out, conv_new = _conv_sample(conv_tm, glu.reshape(t, n, D_CONV), p["w_dw"], p["b_dw"], p["ln_g"],
                                              p["ln_b"], i)
            qh = q.reshape(t, n, N_KV_HEADS, GQA_GROUP, HEAD_DIM).transpose(1, 2, 3, 0, 4)
            qh = qh.reshape(n, N_KV_HEADS, GQA_GROUP * t, HEAD_DIM)
            kn = k.reshape(t, n, N_KV_HEADS, HEAD_DIM).transpose(1, 2, 3, 0)
            vn = v.reshape(t, n, N_KV_HEADS, HEAD_DIM).transpose(1, 2, 3, 0)
            oh, k_new, v_new = _attn_sample(qh, k_t, kn, v_t, vn, p["sinks"][i], i)
            attn = oh.reshape(n, N_KV_HEADS, GQA_GROUP, t, HEAD_DIM).transpose(3, 0, 1, 2, 4).reshape(t * n, D_ATTN)
            xt = _outproj_ffn(xt, conv_out.reshape(t * n, D_CONV), attn, p["w_out"], i, *ffn(layer))
            x_tm = xt.reshape(t, n, d)
            new_conv.append(conv_new)
            new_k.append(k_new)
            new_v.append(v_new)
        else:
            x_tm, pool_new = _pool_ffn_sample(x_tm, pool_tm, p["norm_mix"], p["w_pool"], p["pool_scale"], layer, i, pos0,
                                              *ffn(layer)[:-1])
            new_pool.append(pool_new)
    return (x_tm.transpose(1, 0, 2), jnp.stack(new_conv).transpose(0, 2, 1, 3),
            jnp.stack(new_k).transpose(0, 1, 4, 2, 3), jnp.stack(new_v).transpose(0, 1, 4, 2, 3),
            jnp.stack(new_pool).transpose(0, 2, 1, 3))


def kernel(x_prompt, x_sample, cache_conv, cache_k, cache_v, state_pool, norm_mix, norm_ffn, w_in, q_norm, k_norm,
           sinks, w_dw, b_dw, conv_norm_g, conv_norm_b, w_out, w_pool, pool_scale, w_gate, w_up, w_down):
    depth = norm_mix.shape[0]
    p = _prepare_weights(norm_mix, norm_ffn, w_in, q_norm, k_norm, sinks, w_dw, b_dw, conv_norm_g, conv_norm_b, w_out,
                         w_pool, pool_scale, w_gate, w_up, w_down)
    y_p, conv_p, k_p, v_p, pool_p = _trunk_prompt(x_prompt, p, depth)
    y_s, conv_s, k_s, v_s, pool_s = _trunk_sample(x_sample, cache_conv, cache_k, cache_v, state_pool, PAST_LEN, p, depth)
    return (y_p, y_s, conv_p, k_p, v_p, pool_p, conv_s, k_s, v_s, pool_s)
```

```python
import functools

import jax
import jax.numpy as jnp
from jax import lax
from jax.experimental import pallas as pl
from jax.experimental.pallas import tpu as pltpu

F32 = jnp.float32
BF16 = jnp.bfloat16

HEAD_DIM = 64
N_HEADS = 8
N_KV_HEADS = 2
GQA_GROUP = N_HEADS // N_KV_HEADS
D_ATTN = N_HEADS * HEAD_DIM
D_KV = N_KV_HEADS * HEAD_DIM
WINDOW = 128
PAST_LEN = 8192
D_CONV = 512
CONV_WIDTH = 31
CONV_HALO = 32
POOL_WINDOWS = (2, 4, 8, 16)
POOL_BUF = max(POOL_WINDOWS) - 1
POOL_HALO = 16
RMS_EPS = 1e-6
LN_EPS = 1e-5
NEG_INF = -1e30
LANES = 128
ALIBI_SLOPES = tuple(2.0 ** (-8.0 * (h + 1) / N_HEADS) for h in range(N_HEADS))

TOKEN_TILE = 512
CONV_TIME_TILE = 512
CONV_PHASES = 4
ATTN_TIME_TILE = 512
POOL_TIME_TILE = 512
POOL_PHASES = 4
FFN_CHUNK = 1536
SAMPLE_SEQ_TILE = 32
SAMPLE_FFN_SEQ_TILE = 64
VMEM_LIMIT = 56 * 1024 * 1024


def _params(n_axes):
    return pltpu.CompilerParams(dimension_semantics=("arbitrary",) * n_axes, vmem_limit_bytes=VMEM_LIMIT)


def _rmsnorm(x, g):
    return x * lax.rsqrt(jnp.mean(x * x, axis=-1, keepdims=True) + RMS_EPS) * g


def _head_rmsnorm(x, g):
    lane = lax.broadcasted_iota(jnp.int32, (x.shape[0], LANES), 1)
    low = lane < HEAD_DIM
    outs = []
    for s in range(x.shape[1] // LANES):
        xs = x[:, s * LANES:(s + 1) * LANES]
        sq = xs * xs
        s_low = jnp.sum(jnp.where(low, sq, 0.0), axis=-1, keepdims=True)
        s_high = jnp.sum(jnp.where(low, 0.0, sq), axis=-1, keepdims=True)
        ms = jnp.where(low, s_low, s_high) * (1.0 / HEAD_DIM)
        outs.append(xs * lax.rsqrt(ms + RMS_EPS) * g[:, s * LANES:(s + 1) * LANES])
    return outs[0] if len(outs) == 1 else jnp.concatenate(outs, axis=-1)


def _layernorm_silu(y, g, b):
    mu = jnp.mean(y, axis=-1, keepdims=True)
    yc = y - mu
    var = jnp.mean(yc * yc, axis=-1, keepdims=True)
    z = yc * lax.rsqrt(var + LN_EPS) * g + b
    return z * jax.nn.sigmoid(z)


def _ffn_apply(x, g_ref, wg_ref, wu_ref, wd_ref):
    h = _rmsnorm(x, g_ref[...]).astype(BF16)
    acc = x
    d_ff = wg_ref.shape[1]
    for c0 in range(0, d_ff, FFN_CHUNK):
        c1 = min(c0 + FFN_CHUNK, d_ff)
        gate = jnp.dot(h, wg_ref[:, c0:c1], preferred_element_type=F32)
        up = jnp.dot(h, wu_ref[:, c0:c1], preferred_element_type=F32)
        a = (gate * jax.nn.sigmoid(gate) * up).astype(BF16)
        acc = acc + jnp.dot(a, wd_ref[c0:c1, :], preferred_element_type=F32)
    return acc


def _ffn_specs(w_gate, layer, n_axes):
    _, d, f = w_gate.shape
    at_layer = {1: lambda i: (layer, 0, 0), 2: lambda b, i: (layer, 0, 0)}[n_axes]
    resident = functools.partial(pl.BlockSpec, pipeline_mode=pl.Buffered(1))
    return [pl.BlockSpec((None, 1, d), at_layer), resident((None, d, f), at_layer), resident((None, d, f), at_layer),
            resident((None, f, d), at_layer)]


def _inproj_body(x_ref, g_ref, w_ref, qn_ref, kn_ref, glu_ref, q_ref, k_ref, v_ref):
    h = _rmsnorm(x_ref[...], g_ref[...]).astype(BF16)
    u = jnp.dot(h, w_ref[...], preferred_element_type=F32)
    o_q = 2 * D_CONV
    o_k = o_q + D_ATTN
    o_v = o_k + D_KV
    glu_ref[...] = u[:, :D_CONV] * jax.nn.sigmoid(u[:, D_CONV:o_q])
    q_ref[...] = (_head_rmsnorm(u[:, o_q:o_k], qn_ref[...]) * (HEAD_DIM ** -0.5)).astype(BF16)
    k_ref[...] = _head_rmsnorm(u[:, o_k:o_v], kn_ref[...])
    v_ref[...] = u[:, o_v:]


def _inproj(x, norm_g, w_in, q_gain, k_gain, layer, i_even):
    t, d = x.shape
    n = w_in.shape[2]
    row = lambda width: pl.BlockSpec((TOKEN_TILE, width), lambda i: (i, 0))
    return pl.pallas_call(
        _inproj_body,
        grid=(t // TOKEN_TILE,),
        in_specs=[row(d),
                  pl.BlockSpec((None, 1, d), lambda i: (layer, 0, 0)),
                  pl.BlockSpec((None, d, n), lambda i: (i_even, 0, 0), pipeline_mode=pl.Buffered(1)),
                  pl.BlockSpec((None, 1, D_ATTN), lambda i: (i_even, 0, 0)),
                  pl.BlockSpec((None, 1, D_KV), lambda i: (i_even, 0, 0))],
        out_specs=[row(D_CONV), row(D_ATTN), row(D_KV), row(D_KV)],
        out_shape=[jax.ShapeDtypeStruct((t, D_CONV), F32), jax.ShapeDtypeStruct((t, D_ATTN), BF16),
                   jax.ShapeDtypeStruct((t, D_KV), F32), jax.ShapeDtypeStruct((t, D_KV), F32)],
        compiler_params=_params(1),
        name="inproj",
    )(x, norm_g, w_in, q_gain, k_gain)


def _outproj_ffn_body(x_ref, c_ref, a_ref, w_ref, gf_ref, wg_ref, wu_ref, wd_ref, o_ref):
    y = jnp.dot(c_ref[...], w_ref[:D_CONV, :], preferred_element_type=F32)
    y = y + jnp.dot(a_ref[...], w_ref[D_CONV:, :], preferred_element_type=F32)
    o_ref[...] = _ffn_apply(x_ref[...] + y, gf_ref, wg_ref, wu_ref, wd_ref)


def _outproj_ffn(x, conv_out, attn, w_out, i_even, norm_ffn, w_gate, w_up, w_down, layer):
    t, d = x.shape
    row = lambda width: pl.BlockSpec((TOKEN_TILE, width), lambda i: (i, 0))
    return pl.pallas_call(
        _outproj_ffn_body,
        grid=(t // TOKEN_TILE,),
        in_specs=[row(d), row(D_CONV), row(D_ATTN),
                  pl.BlockSpec((None, D_CONV + D_ATTN, d), lambda i: (i_even, 0, 0), pipeline_mode=pl.Buffered(1))]
        + _ffn_specs(w_gate, layer, 1),
        out_specs=row(d),
        out_shape=jax.ShapeDtypeStruct((t, d), F32),
        compiler_params=_params(1),
        name="outproj_ffn",
    )(x, conv_out, attn, w_out, norm_ffn, w_gate, w_up, w_down)


def _depthwise_conv_halves(ext_ref, y_ref, w_ref, b_ref, slab, half_rows):
    lead = CONV_HALO - (CONV_WIDTH - 1)
    rows = 8 * CONV_PHASES
    lanes = slice(slab * LANES, (slab + 1) * LANES)

    def chunk(c, carry):
        r0 = c * rows
        acc = [jnp.zeros((16, LANES), F32) for _ in range(CONV_PHASES)]
        for e in range(CONV_WIDTH + CONV_PHASES - 1):
            v = jnp.concatenate([ext_ref[slab, pl.ds(r0 + lead + e, 8, stride=CONV_PHASES), :],
                                 ext_ref[slab, pl.ds(half_rows + r0 + lead + e, 8, stride=CONV_PHASES), :]],
                                axis=0).astype(BF16)
            for ph in range(CONV_PHASES):
                k = e - ph
                if 0 <= k < CONV_WIDTH:
                    acc[ph] = acc[ph] + v.astype(F32) * w_ref[k, :, lanes].astype(F32)
        bias = jnp.broadcast_to(b_ref[0:1, lanes], (16, LANES))
        for ph in range(CONV_PHASES):
            out = acc[ph] + bias
            y_ref[slab, pl.ds(r0 + ph, 8, stride=CONV_PHASES), :] = out[:8]
            y_ref[slab, pl.ds(half_rows + r0 + ph, 8, stride=CONV_PHASES), :] = out[8:]
        return carry

    lax.fori_loop(0, half_rows // rows, chunk, 0)


def _layernorm_silu_slabs(y_ref, g_ref, beta_ref, o_ref):
    n_slabs = y_ref.shape[0]
    ys = [y_ref[s] for s in range(n_slabs)]
    inv_c = 1.0 / (n_slabs * LANES)
    mu = jnp.sum(sum(ys), axis=-1, keepdims=True) * inv_c
    ycs = [y - mu for y in ys]
    var = jnp.sum(sum(yc * yc for yc in ycs), axis=-1, keepdims=True) * inv_c
    rstd = lax.rsqrt(var + LN_EPS)
    for s, yc in enumerate(ycs):
        lanes = slice(s * LANES, (s + 1) * LANES)
        z = yc * rstd * g_ref[:, lanes] + beta_ref[:, lanes]
        o_ref[:, lanes] = (z * jax.nn.sigmoid(z)).astype(o_ref.dtype)


def _conv_prompt_body(cur_ref, prev_ref, w_ref, b_ref, g_ref, beta_ref, o_ref, ext_ref, y_ref):
    tt = cur_ref.shape[0]
    first = pl.program_id(1) == 0
    for s in range(D_CONV // LANES):
        lanes = slice(s * LANES, (s + 1) * LANES)
        ext_ref[s, 0:CONV_HALO, :] = jnp.where(first, 0.0, prev_ref[:, lanes])
        ext_ref[s, CONV_HALO:, :] = cur_ref[:, lanes]
    for s in range(D_CONV // LANES):
        _depthwise_conv_halves(ext_ref, y_ref, w_ref, b_ref, s, tt // 2)
    _layernorm_silu_slabs(y_ref, g_ref, beta_ref, o_ref)


def _conv_prompt(glu, w_taps, b_dw, ln_g, ln_b, i_even):
    n, t, c = glu.shape
    tt = CONV_TIME_TILE
    halo_per_tile = tt // CONV_HALO
    vec = pl.BlockSpec((None, 1, c), lambda b, i: (i_even, 0, 0))
    return pl.pallas_call(
        _conv_prompt_body,
        grid=(n, t // tt),
        in_specs=[pl.BlockSpec((None, tt, c), lambda b, i: (b, i, 0)),
                  pl.BlockSpec((None, CONV_HALO, c), lambda b, i: (b, jnp.maximum(i * halo_per_tile - 1, 0), 0)),
                  pl.BlockSpec((None, CONV_WIDTH, 16, c), lambda b, i: (i_even, 0, 0, 0)),
                  vec, vec, vec],
        out_specs=pl.BlockSpec((None, tt, c), lambda b, i: (b, i, 0)),
        out_shape=jax.ShapeDtypeStruct((n, t, c), BF16),
        scratch_shapes=[pltpu.VMEM((c // LANES, CONV_HALO + tt, LANES), F32), pltpu.VMEM((c // LANES, tt, LANES), F32)],
        compiler_params=_params(2),
        name="conv_prompt",
    )(glu, glu, w_taps, b_dw, ln_g, ln_b)


def _attn_prompt_body(sink_ref, q_ref, kc_ref, kp_ref, vc_ref, vp_ref, o_ref, bias_ref, sinkcol_ref, s_ref):
    b = pl.program_id(0)
    i = pl.program_id(1)
    w = WINDOW
    pair = 2 * HEAD_DIM
    n_blocks = q_ref.shape[0] // w
    lane = lax.broadcasted_iota(jnp.int32, (1, pair), 1)
    low = lane < HEAD_DIM

    @pl.when((b == 0) & (i == 0))
    def _():
        row = lax.broadcasted_iota(jnp.int32, (w, 2 * w), 0)
        col = lax.broadcasted_iota(jnp.int32, (w, 2 * w), 1)
        dist = w + row - col
        valid = (dist >= 0) & (dist < w)
        distf = dist.astype(F32)
        for h in range(N_HEADS):
            slab, half = h // 2, h % 2
            bias_ref[slab * w:(slab + 1) * w, half * 2 * w:(half + 1) * 2 * w] = (
                jnp.where(valid, -ALIBI_SLOPES[h] * distf, NEG_INF))
            sinkcol_ref[half, slab * w:(slab + 1) * w, :] = jnp.full((w, 1), sink_ref[h], F32)

    kcat = jnp.concatenate([kp_ref[...], kc_ref[...]], axis=0)
    vcat = jnp.concatenate([vp_ref[...], vc_ref[...]], axis=0)
    krot = pltpu.roll(kcat, HEAD_DIM, 1)
    vrot = pltpu.roll(vcat, HEAD_DIM, 1)
    zero = jnp.zeros((), F32)
    k_low = [jnp.where(low, kcat, zero).astype(BF16), jnp.where(low, krot, zero).astype(BF16)]
    k_high = [jnp.where(low, zero, krot).astype(BF16), jnp.where(low, zero, kcat).astype(BF16)]
    v_low = [jnp.where(low, vcat, zero).astype(BF16), jnp.where(low, vrot, zero).astype(BF16)]
    v_high = [jnp.where(low, zero, vrot).astype(BF16), jnp.where(low, zero, vcat).astype(BF16)]
    ones_low = jnp.broadcast_to(jnp.where(low, 1.0, 0.0).astype(BF16), (2 * w, pair))
    ones_high = jnp.broadcast_to(jnp.where(low, 0.0, 1.0).astype(BF16), (2 * w, pair))
    ones_ext = jnp.concatenate([ones_low, ones_high], axis=0)
    col = lax.broadcasted_iota(jnp.int32, (1, 4 * w), 1)
    no_prev = (col % (2 * w) < w) & (i == 0)

    for jb in range(n_blocks):
        rows = slice(jb * w, (jb + 1) * w)
        keys = slice(jb * w, (jb + 2) * w)
        for g in range(N_KV_HEADS):
            q2 = jnp.concatenate([q_ref[rows, (2 * g) * pair:(2 * g + 1) * pair],
                                  q_ref[rows, (2 * g + 1) * pair:(2 * g + 2) * pair]], axis=0)
            kk = jnp.concatenate([k_low[g][keys], k_high[g][keys]], axis=0)
            s_ref[jb, 2 * g * w:(2 * g + 2) * w, :] = lax.dot_general(
                q2, kk, (((1,), (1,)), ((), ())), preferred_element_type=F32)

    for jb in range(n_blocks):
        rows = slice(jb * w, (jb + 1) * w)
        keys = slice(jb * w, (jb + 2) * w)
        bias = bias_ref[...]
        if jb == 0:
            bias = jnp.where(no_prev, NEG_INF, bias)
        s = s_ref[jb] + bias
        ps, sink_terms = [], []
        for half in range(2):
            sh = s[:, half * 2 * w:(half + 1) * 2 * w]
            sink = sinkcol_ref[half]
            m = jnp.maximum(jnp.max(sh, axis=-1, keepdims=True), sink)
            ps.append(jnp.exp(sh - m).astype(BF16))
            sink_terms.append(jnp.exp(sink - m))
        p = jnp.concatenate(ps, axis=1)
        sink_term = jnp.where(low, sink_terms[0], sink_terms[1])
        for g in range(N_KV_HEADS):
            vv = jnp.concatenate([jnp.concatenate([v_low[g][keys], v_high[g][keys]], axis=0), ones_ext], axis=1)
            o = jnp.dot(p[2 * g * w:(2 * g + 2) * w], vv, preferred_element_type=F32)
            o = o[:, :pair] / (o[:, pair:] + sink_term[2 * g * w:(2 * g + 2) * w])
            o_ref[rows, (2 * g) * pair:(2 * g + 1) * pair] = o[:w].astype(o_ref.dtype)
            o_ref[rows, (2 * g + 1) * pair:(2 * g + 2) * pair] = o[w:].astype(o_ref.dtype)


def _attn_prompt(q, k, v, sinks):
    n, t, _ = q.shape
    tt = ATTN_TIME_TILE
    cur = lambda width: pl.BlockSpec((None, tt, width), lambda b, i: (b, i, 0))
    prev = lambda width: pl.BlockSpec((None, WINDOW, width),
                                      lambda b, i: (b, jnp.maximum(i * (tt // WINDOW) - 1, 0), 0))
    return pl.pallas_call(
        _attn_prompt_body,
        grid=(n, t // tt),
        in_specs=[pl.BlockSpec(memory_space=pltpu.SMEM), cur(D_ATTN), cur(D_KV), prev(D_KV), cur(D_KV), prev(D_KV)],
        out_specs=cur(D_ATTN),
        out_shape=jax.ShapeDtypeStruct((n, t, D_ATTN), BF16),
        scratch_shapes=[pltpu.VMEM((N_HEADS // 2 * WINDOW, 4 * WINDOW), F32),
                        pltpu.VMEM((2, N_HEADS // 2 * WINDOW, 1), F32),
                        pltpu.VMEM((tt // WINDOW, N_HEADS // 2 * WINDOW, 4 * WINDOW), F32)],
        compiler_params=_params(2),
        name="attn_prompt",
    )(sinks, q, k, k, v, v)


def _window_sums(vals, w, n_out):
    shared = list(range(n_out - 1, w))
    base = None
    for e in shared:
        base = vals[e] if base is None else base + vals[e]
    sums = []
    for i in range(n_out):
        acc = base
        for e in range(i, i + w):
            if e not in shared:
                acc = vals[e] if acc is None else acc + vals[e]
        sums.append(acc)
    return sums


def _pool_deltas_strided(ext_ref, d_ref, pos0, n_rows):
    n_slabs = ext_ref.shape[0]
    slabs_per_group = n_slabs // len(POOL_WINDOWS)
    rows = 8 * POOL_PHASES

    def chunk(c, carry):
        r0 = c * rows
        sub = lax.broadcasted_iota(jnp.int32, (8, LANES), 0)
        pos = [pos0 + r0 + ph + POOL_PHASES * sub for ph in range(POOL_PHASES)]
        inv = {w: [1.0 / jnp.minimum(w, p + 1).astype(F32) for p in pos] for w in POOL_WINDOWS}
        for slab in range(n_slabs):
            w = POOL_WINDOWS[slab // slabs_per_group]
            first = POOL_HALO - (w - 1)
            vals = [ext_ref[slab, pl.ds(r0 + first + e, 8, stride=POOL_PHASES), :] for e in range(w + POOL_PHASES - 1)]
            sums = _window_sums(vals, w, POOL_PHASES)
            for ph in range(POOL_PHASES):
                d_ref[slab, pl.ds(r0 + ph, 8, stride=POOL_PHASES), :] = sums[ph] * inv[w][ph] - vals[ph + w - 1]
        return carry

    lax.fori_loop(0, n_rows // rows, chunk, 0)


def _pool_ffn_prompt_body(x_ref, g_ref, w_ref, sc_ref, gf_ref, wg_ref, wu_ref, wd_ref, o_ref, tail_ref, ext_ref, d_ref):
    tt, d = x_ref.shape
    n_slabs = d // LANES
    slabs_per_group = n_slabs // len(POOL_WINDOWS)
    i = pl.program_id(1)

    @pl.when(i == 0)
    def _():
        ext_ref[:, 0:POOL_HALO, :] = jnp.zeros((n_slabs, POOL_HALO, LANES), F32)

    @pl.when(i > 0)
    def _():
        ext_ref[:, 0:POOL_HALO, :] = ext_ref[:, tt:tt + POOL_HALO, :]

    x = x_ref[...]
    xn = _rmsnorm(x, g_ref[...])
    for s in range(n_slabs):
        ext_ref[s, POOL_HALO:, :] = xn[:, s * LANES:(s + 1) * LANES]
    tail_ref[...] = xn[tt - POOL_HALO:, :]
    _pool_deltas_strided(ext_ref, d_ref, i * tt, tt)
    ys = []
    for g in range(len(POOL_WINDOWS)):
        dg = jnp.concatenate([d_ref[s] for s in range(g * slabs_per_group, (g + 1) * slabs_per_group)], axis=-1)
        ys.append(jnp.dot(dg.astype(BF16), w_ref[g], preferred_element_type=F32))
    x1 = x + jnp.concatenate(ys, axis=-1) * sc_ref[...]
    o_ref[...] = _ffn_apply(x1, gf_ref, wg_ref, wu_ref, wd_ref)


def _pool_ffn_prompt(x, norm_g, w_pool, pool_scale, layer, i_odd, norm_ffn, w_gate, w_up, w_down):
    n, t, d = x.shape
    tt = POOL_TIME_TILE
    ng, gd, _ = w_pool.shape[1:]
    tile = pl.BlockSpec((None, tt, d), lambda b, i: (b, i, 0))
    return pl.pallas_call(
        _pool_ffn_prompt_body,
        grid=(n, t // tt),
        in_specs=[tile,
                  pl.BlockSpec((None, 1, d), lambda b, i: (layer, 0, 0)),
                  pl.BlockSpec((None, ng, gd, gd), lambda b, i: (i_odd, 0, 0, 0)),
                  pl.BlockSpec((None, 1, d), lambda b, i: (i_odd, 0, 0))] + _ffn_specs(w_gate, layer, 2),
        out_specs=[tile, pl.BlockSpec((None, POOL_HALO, d), lambda b, i: (b, 0, 0))],
        out_shape=[jax.ShapeDtypeStruct((n, t, d), F32), jax.ShapeDtypeStruct((n, POOL_HALO, d), F32)],
        scratch_shapes=[pltpu.VMEM((d // LANES, POOL_HALO + tt, LANES), F32), pltpu.VMEM((d // LANES, tt, LANES), F32)],
        compiler_params=_params(2),
        name="pool_ffn_prompt",
    )(x, norm_g, w_pool, pool_scale, norm_ffn, w_gate, w_up, w_down)


def _conv_sample_body(cache_ref, glu_ref, w_ref, b_ref, g_ref, beta_ref, o_ref, new_ref):
    hist = cache_ref.shape[0]
    t = glu_ref.shape[0]
    ext = lambda j: cache_ref[j] if j < hist else glu_ref[j - hist]
    rounded = lambda a: a.astype(BF16).astype(F32)
    for j in range(hist):
        new_ref[j] = ext(t + j)
    for step in range(t):
        acc = rounded(w_ref[0:1, :]) * rounded(ext(step)) + b_ref[...]
        for k in range(1, CONV_WIDTH):
            acc = acc + rounded(w_ref[k:k + 1, :]) * rounded(ext(step + k))
        o_ref[step] = _layernorm_silu(acc, g_ref[...], beta_ref[...]).astype(o_ref.dtype)


def _conv_sample(cache_tm, glu_tm, w_dw, b_dw, ln_g, ln_b, i_even):
    _, hist, n, c = cache_tm.shape
    t = glu_tm.shape[0]
    nb = SAMPLE_SEQ_TILE
    vec = pl.BlockSpec((None, 1, c), lambda b: (i_even, 0, 0))
    return pl.pallas_call(
        _conv_sample_body,
        grid=(n // nb,),
        in_specs=[pl.BlockSpec((None, hist, nb, c), lambda b: (i_even, 0, b, 0)),
                  pl.BlockSpec((t, nb, c), lambda b: (0, b, 0)),
                  pl.BlockSpec((None, CONV_WIDTH, c), lambda b: (i_even, 0, 0)),
                  vec, vec, vec],
        out_specs=[pl.BlockSpec((t, nb, c), lambda b: (0, b, 0)), pl.BlockSpec((hist, nb, c), lambda b: (0, b, 0))],
        out_shape=[jax.ShapeDtypeStruct((t, n, c), BF16), jax.ShapeDtypeStruct((hist, n, c), F32)],
        compiler_params=_params(1),
        name="conv_sample",
    )(cache_tm, glu_tm, w_dw, b_dw, ln_g, ln_b)


def _attn_sample_body(sink_ref, q_ref, kc_ref, kn_ref, vc_ref, vn_ref, o_ref, knew_ref, vnew_ref):
    nb, nkv, rows, hd = q_ref.shape
    buf = kc_ref.shape[-1]
    t = kn_ref.shape[-1]
    batch = nb * nkv
    q = q_ref[...].reshape(batch, rows, hd)
    kc = kc_ref[...].reshape(batch, hd, buf)
    kn = kn_ref[...].reshape(batch, hd, t)
    vc = vc_ref[...].reshape(batch, hd, buf)
    vn = vn_ref[...].reshape(batch, hd, t)

    lane = lax.broadcasted_iota(jnp.int32, (1, 1, buf), 2)
    pad = jnp.zeros((batch, hd, buf - t), F32)
    k_new = jnp.where(lane < buf - t, pltpu.roll(kc, buf - t, 2), jnp.concatenate([pad, kn], axis=-1))
    v_new = jnp.where(lane < buf - t, pltpu.roll(vc, buf - t, 2), jnp.concatenate([pad, vn], axis=-1))
    knew_ref[...] = k_new.reshape(nb, nkv, hd, buf)
    vnew_ref[...] = v_new.reshape(nb, nkv, hd, buf)

    s_c = jnp.einsum("bqd,bdk->bqk", q, kc.astype(BF16), preferred_element_type=F32)
    s_n = jnp.einsum("bqd,bdk->bqk", q, kn.astype(BF16), preferred_element_type=F32)
    step = lax.broadcasted_iota(jnp.int32, (rows, 1), 0) % t
    head_in_group = lax.broadcasted_iota(jnp.int32, (rows, 1), 0) // t
    dist_c = buf + step - lax.broadcasted_iota(jnp.int32, (rows, buf), 1)
    dist_n = step - lax.broadcasted_iota(jnp.int32, (rows, t), 1)
    s_c = s_c.reshape(nb, nkv, rows, buf)
    s_n = s_n.reshape(nb, nkv, rows, t)
    ps_c, ps_n, invs = [], [], []
    for kv in range(nkv):
        slope = jnp.zeros((rows, 1), F32)
        sink = jnp.zeros((rows, 1), F32)
        for j in range(GQA_GROUP):
            slope = jnp.where(head_in_group == j, ALIBI_SLOPES[kv * GQA_GROUP + j], slope)
            sink = jnp.where(head_in_group == j, sink_ref[kv * GQA_GROUP + j], sink)
        sc = jnp.where((dist_c >= 0) & (dist_c < WINDOW), s_c[:, kv] - slope * dist_c.astype(F32), NEG_INF)
        sn = jnp.where((dist_n >= 0) & (dist_n < WINDOW), s_n[:, kv] - slope * dist_n.astype(F32), NEG_INF)
        m = jnp.maximum(jnp.maximum(jnp.max(sc, axis=-1, keepdims=True), jnp.max(sn, axis=-1, keepdims=True)), sink)
        pc = jnp.exp(sc - m)
        pn = jnp.exp(sn - m)
        denom = jnp.sum(pc, axis=-1, keepdims=True) + jnp.sum(pn, axis=-1, keepdims=True) + jnp.exp(sink - m)
        ps_c.append(pc.astype(BF16))
        ps_n.append(pn.astype(BF16))
        invs.append(1.0 / denom)
    p_c = jnp.stack(ps_c, axis=1).reshape(batch, rows, buf)
    p_n = jnp.stack(ps_n, axis=1).reshape(batch, rows, t)
    inv = jnp.stack(invs, axis=1).reshape(batch, rows, 1)
    o = jnp.einsum("bqk,bdk->bqd", p_c, vc.astype(BF16), preferred_element_type=F32)
    o = o + jnp.einsum("bqk,bdk->bqd", p_n, vn.astype(BF16), preferred_element_type=F32)
    o_ref[...] = (o * inv).reshape(nb, nkv, rows, hd).astype(o_ref.dtype)


def _attn_sample(q, kc, kn, vc, vn, sinks, i_even):
    n, nkv, rows, hd = q.shape
    buf = kc.shape[-1]
    t = kn.shape[-1]
    nb = SAMPLE_SEQ_TILE
    cache = pl.BlockSpec((None, nb, nkv, hd, buf), lambda b: (i_even, b, 0, 0, 0))
    new = pl.BlockSpec((nb, nkv, hd, t), lambda b: (b, 0, 0, 0))
    state = pl.BlockSpec((nb, nkv, hd, buf), lambda b: (b, 0, 0, 0))
    qspec = pl.BlockSpec((nb, nkv, rows, hd), lambda b: (b, 0, 0, 0))
    return pl.pallas_call(
        _attn_sample_body,
        grid=(n // nb,),
        in_specs=[pl.BlockSpec(memory_space=pltpu.SMEM), qspec, cache, new, cache, new],
        out_specs=[qspec, state, state],
        out_shape=[jax.ShapeDtypeStruct((n, nkv, rows, hd), BF16), jax.ShapeDtypeStruct((n, nkv, hd, buf), F32),
                   jax.ShapeDtypeStruct((n, nkv, hd, buf), F32)],
        compiler_params=_params(1),
        name="attn_sample",
    )(sinks, q, kc, kn, vc, vn)


def _pool_ffn_sample_body(x_ref, st_ref, g_ref, w_ref, sc_ref, gf_ref, wg_ref, wu_ref, wd_ref, o_ref, new_ref, *, pos0):
    t, nb, d = x_ref.shape
    hist = st_ref.shape[0]
    gdim = d // len(POOL_WINDOWS)
    x = x_ref[...].reshape(t * nb, d)
    xn = _rmsnorm(x, g_ref[...])
    ext = lambda j: st_ref[j] if j < hist else xn[(j - hist) * nb:(j - hist + 1) * nb]
    for j in range(hist):
        new_ref[j] = ext(t + j)
    ys = []
    for g, w in enumerate(POOL_WINDOWS):
        lanes = slice(g * gdim, (g + 1) * gdim)
        steps = []
        for step in range(t):
            acc = ext(hist + step)[:, lanes]
            for j in range(1, w):
                acc = acc + ext(hist + step - j)[:, lanes]
            cnt = float(min(w, pos0 + step + 1))
            steps.append(acc / cnt - xn[step * nb:(step + 1) * nb, lanes])
        dg = jnp.concatenate(steps, axis=0)
        ys.append(jnp.dot(dg.astype(BF16), w_ref[g], preferred_element_type=F32))
    x1 = x + jnp.concatenate(ys, axis=-1) * sc_ref[...]
    o_ref[...] = _ffn_apply(x1, gf_ref, wg_ref, wu_ref, wd_ref).reshape(t, nb, d)


def _pool_ffn_sample(x_tm, state_tm, norm_g, w_pool, pool_scale, layer, i_odd, pos0, norm_ffn, w_gate, w_up, w_down):
    t, n, d = x_tm.shape
    hist = state_tm.shape[1]
    nb = SAMPLE_FFN_SEQ_TILE
    ng, gd, _ = w_pool.shape[1:]
    return pl.pallas_call(
        functools.partial(_pool_ffn_sample_body, pos0=pos0),
        grid=(n // nb,),
        in_specs=[pl.BlockSpec((t, nb, d), lambda b: (0, b, 0)),
                  pl.BlockSpec((None, hist, nb, d), lambda b: (i_odd, 0, b, 0)),
                  pl.BlockSpec((None, 1, d), lambda b: (layer, 0, 0)),
                  pl.BlockSpec((None, ng, gd, gd), lambda b: (i_odd, 0, 0, 0)),
                  pl.BlockSpec((None, 1, d), lambda b: (i_odd, 0, 0))] + _ffn_specs(w_gate, layer, 1),
        out_specs=[pl.BlockSpec((t, nb, d), lambda b: (0, b, 0)), pl.BlockSpec((hist, nb, d), lambda b: (0, b, 0))],
        out_shape=[jax.ShapeDtypeStruct((t, n, d), F32), jax.ShapeDtypeStruct((hist, n, d), F32)],
        compiler_params=_params(1),
        name="pool_ffn_sample",
    )(x_tm, state_tm, norm_g, w_pool, pool_scale, norm_ffn, w_gate, w_up, w_down)


def _prepare_weights(norm_mix, norm_ffn, w_in, q_norm, k_norm, sinks, w_dw, b_dw, conv_norm_g, conv_norm_b, w_out,
                     w_pool, pool_scale, w_gate, w_up, w_down):
    vec = lambda a: a[:, None, :]
    return dict(
        norm_mix=vec(norm_mix), norm_ffn=vec(norm_ffn), w_in=w_in.astype(BF16),
        q_gain=vec(jnp.tile(q_norm, (1, N_HEADS))), k_gain=vec(jnp.tile(k_norm, (1, N_KV_HEADS))), sinks=sinks,
        w_dw=w_dw, b_dw=vec(b_dw),
        w_taps=jnp.broadcast_to(w_dw.astype(BF16)[:, :, None, :], w_dw.shape[:2] + (16, w_dw.shape[2])),
        ln_g=vec(conv_norm_g), ln_b=vec(conv_norm_b), w_out=w_out.astype(BF16),
        w_pool=w_pool.astype(BF16), pool_scale=vec(pool_scale),
        w_gate=w_gate.astype(BF16), w_up=w_up.astype(BF16), w_down=w_down.astype(BF16))


def _trunk_prompt(x, p, depth):
    n, t, d = x.shape
    xt = x.reshape(n * t, d)
    ffn = lambda layer: (p["norm_ffn"], p["w_gate"], p["w_up"], p["w_down"], layer)
    new_conv, new_k, new_v, new_pool = [], [], [], []
    for layer in range(depth):
        i = layer // 2
        if layer % 2 == 0:
            glu, q, k, v = _inproj(xt, p["norm_mix"], p["w_in"], p["q_gain"], p["k_gain"], layer, i)
            glu, q, k, v = (a.reshape(n, t, -1) for a in (glu, q, k, v))
            conv_out = _conv_prompt(glu, p["w_taps"], p["b_dw"], p["ln_g"], p["ln_b"], i)
            attn = _attn_prompt(q, k, v, p["sinks"][i])
            xt = _outproj_ffn(xt, conv_out.reshape(n * t, -1), attn.reshape(n * t, -1), p["w_out"], i, *ffn(layer))
            keep = min(WINDOW, t)
            new_conv.append(glu[:, t - (CONV_WIDTH - 1):])
            new_k.append(k[:, t - keep:].reshape(n, keep, N_KV_HEADS, HEAD_DIM))
            new_v.append(v[:, t - keep:].reshape(n, keep, N_KV_HEADS, HEAD_DIM))
        else:
            y, tail = _pool_ffn_prompt(xt.reshape(n, t, d), p["norm_mix"], p["w_pool"], p["pool_scale"], layer, i,
                                       *ffn(layer)[:-1])
            xt = y.reshape(n * t, d)
            new_pool.append(tail[:, POOL_HALO - POOL_BUF:])
    return xt.reshape(n, t, d), jnp.stack(new_conv), jnp.stack(new_k), jnp.stack(new_v), jnp.stack(new_pool)


def _trunk_sample(x, conv_bufs, k_bufs, v_bufs, pool_bufs, pos0, p, depth):
    n, t, d = x.shape
    x_tm = x.transpose(1, 0, 2)
    conv_tm = conv_bufs.transpose(0, 2, 1, 3)
    pool_tm = pool_bufs.transpose(0, 2, 1, 3)
    k_t = k_bufs.transpose(0, 1, 3, 4, 2)
    v_t = v_bufs.transpose(0, 1, 3, 4, 2)
    ffn = lambda layer: (p["norm_ffn"], p["w_gate"], p["w_up"], p["w_down"], layer)
    new_conv, new_k, new_v, new_pool = [], [], [], []
    for layer in range(depth):
        i = layer // 2
        if layer % 2 == 0:
            xt = x_tm.reshape(t * n, d)
            glu, q, k, v = _inproj(xt, p["norm_mix"], p["w_in"], p["q_gain"], p["k_gain"], layer, i)
            conv_out, conv_new = _conv_sample(conv_tm, glu.reshape(t, n, D_CONV), p["w_dw"], p["b_dw"], p["ln_g"],
                                              p["ln_b"], i)
            qh = q.reshape(t, n, N_KV_HEADS, GQA_GROUP, HEAD_DIM).transpose(1, 2, 3, 0, 4)
            qh = qh.reshape(n, N_KV_HEADS, GQA_GROUP * t, HEAD_DIM)
            kn = k.reshape(t, n, N_KV_HEADS, HEAD_DIM).transpose(1, 2, 3, 0)
            vn = v.reshape(t, n, N_KV_HEADS, HEAD_DIM).transpose(1, 2, 3, 0)
            oh, k_new, v_new = _attn_sample(qh, k_t, kn, v_t, vn, p["sinks"][i], i)
            attn = oh.reshape(n, N_KV_HEADS, GQA_GROUP, t, HEAD_DIM).transpose(3, 0, 1, 2, 4).reshape(t * n, D_ATTN)
            xt = _outproj_ffn(xt, conv_out.reshape(t * n, D_CONV), attn, p["w_out"], i, *ffn(layer))
            x_tm = xt.reshape(t, n, d)
            new_conv.append(conv_new)
            new_k.append(k_new)
            new_v.append(v_new)
        else:
            x_tm, pool_new = _pool_ffn_sample(x_tm, pool_tm, p["norm_mix"], p["w_pool"], p["pool_scale"], layer, i, pos0,
                                              *ffn(layer)[:-1])
            new_pool.append(pool_new)
    return (x_tm.transpose(1, 0, 2), jnp.stack(new_conv).transpose(0, 2, 1, 3),
            jnp.stack(new_k).transpose(0, 1, 4, 2, 3), jnp.stack(new_v).transpose(0, 1, 4, 2, 3),
            jnp.stack(new_pool).transpose(0, 2, 1, 3))


def kernel(x_prompt, x_sample, cache_conv, cache_k, cache_v, state_pool, norm_mix, norm_ffn, w_in, q_norm, k_norm,
           sinks, w_dw, b_dw, conv_norm_g, conv_norm_b, w_out, w_pool, pool_scale, w_gate, w_up, w_down):
    depth = norm_mix.shape[0]
    p = _prepare_weights(norm_mix, norm_ffn, w_in, q_norm, k_norm, sinks, w_dw, b_dw, conv_norm_g, conv_norm_b, w_out,
                         w_pool, pool_scale, w_gate, w_up, w_down)
    y_p, conv_p, k_p, v_p, pool_p = _trunk_prompt(x_prompt, p, depth)
    y_s, conv_s, k_s, v_s, pool_s = _trunk_sample(x_sample, cache_conv, cache_k, cache_v, state_pool, PAST_LEN, p, depth)
    return (y_p, y_s, conv_p, k_p, v_p, pool_p, conv_s, k_s, v_s, pool_s)
```

```python
import functools

import jax
import jax.numpy as jnp
from jax import lax
from jax.experimental import pallas as pl
from jax.experimental.pallas import tpu as pltpu

F32 = jnp.float32
BF16 = jnp.bfloat16

HEAD_DIM = 64
N_HEADS = 8
N_KV_HEADS = 2
GQA_GROUP = N_HEADS // N_KV_HEADS
D_ATTN = N_HEADS * HEAD_DIM
D_KV = N_KV_HEADS * HEAD_DIM
WINDOW = 128
PAST_LEN = 8192
D_CONV = 512
CONV_WIDTH = 31
CONV_HALO = 32
POOL_WINDOWS = (2, 4, 8, 16)
POOL_BUF = max(POOL_WINDOWS) - 1
POOL_HALO = 16
RMS_EPS = 1e-6
LN_EPS = 1e-5
NEG_INF = -1e30
LANES = 128
ALIBI_SLOPES = tuple(2.0 ** (-8.0 * (h + 1) / N_HEADS) for h in range(N_HEADS))

TOKEN_TILE = 512
CONV_TIME_TILE = 512
CONV_PHASES = 4
ATTN_TIME_TILE = 512
POOL_TIME_TILE = 512
POOL_PHASES = 4
FFN_CHUNK = 1536
SAMPLE_SEQ_TILE = 32
SAMPLE_FFN_SEQ_TILE = 64
VMEM_LIMIT = 56 * 1024 * 1024


def _params(n_axes):
    return pltpu.CompilerParams(dimension_semantics=("arbitrary",) * n_axes, vmem_limit_bytes=VMEM_LIMIT)


def _rmsnorm(x, g):
    return x * lax.rsqrt(jnp.mean(x * x, axis=-1, keepdims=True) + RMS_EPS) * g


def _head_rmsnorm(x, g):
    lane = lax.broadcasted_iota(jnp.int32, (x.shape[0], LANES), 1)
    low = lane < HEAD_DIM
    outs = []
    for s in range(x.shape[1] // LANES):
        xs = x[:, s * LANES:(s + 1) * LANES]
        sq = xs * xs
        s_low = jnp.sum(jnp.where(low, sq, 0.0), axis=-1, keepdims=True)
        s_high = jnp.sum(jnp.where(low, 0.0, sq), axis=-1, keepdims=True)
        ms = jnp.where(low, s_low, s_high) * (1.0 / HEAD_DIM)
        outs.append(xs * lax.rsqrt(ms + RMS_EPS) * g[:, s * LANES:(s + 1) * LANES])
    return outs[0] if len(outs) == 1 else jnp.concatenate(outs, axis=-1)


def _layernorm_silu(y, g, b):
    mu = jnp.mean(y, axis=-1, keepdims=True)
    yc = y - mu
    var = jnp.mean(yc * yc, axis=-1, keepdims=True)
    z = yc * lax.rsqrt(var + LN_EPS) * g + b
    return z * jax.nn.sigmoid(z)


def _ffn_apply(x, g_ref, wg_ref, wu_ref, wd_ref):
    h = _rmsnorm(x, g_ref[...]).astype(BF16)
    acc = x
    d_ff = wg_ref.shape[1]
    for c0 in range(0, d_ff, FFN_CHUNK):
        c1 = min(c0 + FFN_CHUNK, d_ff)
        gate = jnp.dot(h, wg_ref[:, c0:c1], preferred_element_type=F32)
        up = jnp.dot(h, wu_ref[:, c0:c1], preferred_element_type=F32)
        a = (gate * jax.nn.sigmoid(gate) * up).astype(BF16)
        acc = acc + jnp.dot(a, wd_ref[c0:c1, :], preferred_element_type=F32)
    return acc


def _ffn_specs(w_gate, layer, n_axes):
    _, d, f = w_gate.shape
    at_layer = {1: lambda i: (layer, 0, 0), 2: lambda b, i: (layer, 0, 0)}[n_axes]
    resident = functools.partial(pl.BlockSpec, pipeline_mode=pl.Buffered(1))
    return [pl.BlockSpec((None, 1, d), at_layer), resident((None, d, f), at_layer), resident((None, d, f), at_layer),
            resident((None, f, d), at_layer)]


def _inproj_body(x_ref, g_ref, w_ref, qn_ref, kn_ref, glu_ref, q_ref, k_ref, v_ref):
    h = _rmsnorm(x_ref[...], g_ref[...]).astype(BF16)
    o_q = 2 * D_CONV
    ua = jnp.dot(h, w_ref[:, o_q:], preferred_element_type=F32)
    uc = jnp.dot(h, w_ref[:, :o_q], preferred_element_type=F32)
    q_ref[...] = (_head_rmsnorm(ua[:, :D_ATTN], qn_ref[...]) * (HEAD_DIM ** -0.5)).astype(BF16)
    k_ref[...] = _head_rmsnorm(ua[:, D_ATTN:D_ATTN + D_KV], kn_ref[...])
    v_ref[...] = ua[:, D_ATTN + D_KV:]
    glu_ref[...] = uc[:, :D_CONV] * jax.nn.sigmoid(uc[:, D_CONV:])


def _inproj(x, norm_g, w_in, q_gain, k_gain, layer, i_even):
    t, d = x.shape
    n = w_in.shape[2]
    row = lambda width: pl.BlockSpec((TOKEN_TILE, width), lambda i: (i, 0))
    return pl.pallas_call(
        _inproj_body,
        grid=(t // TOKEN_TILE,),
        in_specs=[row(d),
                  pl.BlockSpec((None, 1, d), lambda i: (layer, 0, 0)),
                  pl.BlockSpec((None, d, n), lambda i: (i_even, 0, 0), pipeline_mode=pl.Buffered(1)),
                  pl.BlockSpec((None, 1, D_ATTN), lambda i: (i_even, 0, 0)),
                  pl.BlockSpec((None, 1, D_KV), lambda i: (i_even, 0, 0))],
        out_specs=[row(D_CONV), row(D_ATTN), row(D_KV), row(D_KV)],
        out_shape=[jax.ShapeDtypeStruct((t, D_CONV), F32), jax.ShapeDtypeStruct((t, D_ATTN), BF16),
                   jax.ShapeDtypeStruct((t, D_KV), F32), jax.ShapeDtypeStruct((t, D_KV), F32)],
        compiler_params=_params(1),
        name="inproj",
    )(x, norm_g, w_in, q_gain, k_gain)


def _outproj_ffn_body(x_ref, c_ref, a_ref, w_ref, gf_ref, wg_ref, wu_ref, wd_ref, o_ref):
    y = jnp.dot(c_ref[...], w_ref[:D_CONV, :], preferred_element_type=F32)
    y = y + jnp.dot(a_ref[...], w_ref[D_CONV:, :], preferred_element_type=F32)
    o_ref[...] = _ffn_apply(x_ref[...] + y, gf_ref, wg_ref, wu_ref, wd_ref)


def _outproj_ffn(x, conv_out, attn, w_out, i_even, norm_ffn, w_gate, w_up, w_down, layer):
    t, d = x.shape
    row = lambda width: pl.BlockSpec((TOKEN_TILE, width), lambda i: (i, 0))
    return pl.pallas_call(
        _outproj_ffn_body,
        grid=(t // TOKEN_TILE,),
        in_specs=[row(d), row(D_CONV), row(D_ATTN),
                  pl.BlockSpec((None, D_CONV + D_ATTN, d), lambda i: (i_even, 0, 0), pipeline_mode=pl.Buffered(1))]
        + _ffn_specs(w_gate, layer, 1),
        out_specs=row(d),
        out_shape=jax.ShapeDtypeStruct((t, d), F32),
        compiler_params=_params(1),
        name="outproj_ffn",
    )(x, conv_out, attn, w_out, norm_ffn, w_gate, w_up, w_down)


def _depthwise_conv_halves(ext_ref, y_ref, w_ref, b_ref, slab, half_rows):
    lead = CONV_HALO - (CONV_WIDTH - 1)
    rows = 8 * CONV_PHASES
    lanes = slice(slab * LANES, (slab + 1) * LANES)

    def chunk(c, carry):
        r0 = c * rows
        acc = [jnp.zeros((16, LANES), F32) for _ in range(CONV_PHASES)]
        for e in range(CONV_WIDTH + CONV_PHASES - 1):
            v = jnp.concatenate([ext_ref[slab, pl.ds(r0 + lead + e, 8, stride=CONV_PHASES), :],
                                 ext_ref[slab, pl.ds(half_rows + r0 + lead + e, 8, stride=CONV_PHASES), :]],
                                axis=0).astype(BF16)
            for ph in range(CONV_PHASES):
                k = e - ph
                if 0 <= k < CONV_WIDTH:
                    acc[ph] = acc[ph] + v.astype(F32) * w_ref[k, :, lanes].astype(F32)
        bias = jnp.broadcast_to(b_ref[0:1, lanes], (16, LANES))
        for ph in range(CONV_PHASES):
            out = acc[ph] + bias
            y_ref[slab, pl.ds(r0 + ph, 8, stride=CONV_PHASES), :] = out[:8]
            y_ref[slab, pl.ds(half_rows + r0 + ph, 8, stride=CONV_PHASES), :] = out[8:]
        return carry

    lax.fori_loop(0, half_rows // rows, chunk, 0)


def _layernorm_silu_slabs(y_ref, g_ref, beta_ref, o_ref):
    n_slabs = y_ref.shape[0]
    ys = [y_ref[s] for s in range(n_slabs)]
    inv_c = 1.0 / (n_slabs * LANES)
    mu = jnp.sum(sum(ys), axis=-1, keepdims=True) * inv_c
    ycs = [y - mu for y in ys]
    var = jnp.sum(sum(yc * yc for yc in ycs), axis=-1, keepdims=True) * inv_c
    rstd = lax.rsqrt(var + LN_EPS)
    for s, yc in enumerate(ycs):
        lanes = slice(s * LANES, (s + 1) * LANES)
        z = yc * rstd * g_ref[:, lanes] + beta_ref[:, lanes]
        o_ref[:, lanes] = (z * jax.nn.sigmoid(z)).astype(o_ref.dtype)


def _conv_prompt_body(cur_ref, prev_ref, w_ref, b_ref, g_ref, beta_ref, o_ref, ext_ref, y_ref):
    tt = cur_ref.shape[0]
    first = pl.program_id(1) == 0
    for s in range(D_CONV // LANES):
        lanes = slice(s * LANES, (s + 1) * LANES)
        ext_ref[s, 0:CONV_HALO, :] = jnp.where(first, 0.0, prev_ref[:, lanes])
        ext_ref[s, CONV_HALO:, :] = cur_ref[:, lanes]
    for s in range(D_CONV // LANES):
        _depthwise_conv_halves(ext_ref, y_ref, w_ref, b_ref, s, tt // 2)
    _layernorm_silu_slabs(y_ref, g_ref, beta_ref, o_ref)


def _conv_prompt(glu, w_taps, b_dw, ln_g, ln_b, i_even):
    n, t, c = glu.shape
    tt = CONV_TIME_TILE
    halo_per_tile = tt // CONV_HALO
    vec = pl.BlockSpec((None, 1, c), lambda b, i: (i_even, 0, 0))
    return pl.pallas_call(
        _conv_prompt_body,
        grid=(n, t // tt),
        in_specs=[pl.BlockSpec((None, tt, c), lambda b, i: (b, i, 0)),
                  pl.BlockSpec((None, CONV_HALO, c), lambda b, i: (b, jnp.maximum(i * halo_per_tile - 1, 0), 0)),
                  pl.BlockSpec((None, CONV_WIDTH, 16, c), lambda b, i: (i_even, 0, 0, 0)),
                  vec, vec, vec],
        out_specs=pl.BlockSpec((None, tt, c), lambda b, i: (b, i, 0)),
        out_shape=jax.ShapeDtypeStruct((n, t, c), BF16),
        scratch_shapes=[pltpu.VMEM((c // LANES, CONV_HALO + tt, LANES), F32), pltpu.VMEM((c // LANES, tt, LANES), F32)],
        compiler_params=_params(2),
        name="conv_prompt",
    )(glu, glu, w_taps, b_dw, ln_g, ln_b)


def _attn_prompt_body(sink_ref, q_ref, kc_ref, kp_ref, vc_ref, vp_ref, o_ref, bias_ref, sinkcol_ref, s_ref):
    b = pl.program_id(0)
    i = pl.program_id(1)
    w = WINDOW
    pair = 2 * HEAD_DIM
    n_blocks = q_ref.shape[0] // w
    lane = lax.broadcasted_iota(jnp.int32, (1, pair), 1)
    low = lane < HEAD_DIM

    @pl.when((b == 0) & (i == 0))
    def _():
        row = lax.broadcasted_iota(jnp.int32, (w, 2 * w), 0)
        col = lax.broadcasted_iota(jnp.int32, (w, 2 * w), 1)
        dist = w + row - col
        valid = (dist >= 0) & (dist < w)
        distf = dist.astype(F32)
        for h in range(N_HEADS):
            slab, half = h // 2, h % 2
            bias_ref[slab * w:(slab + 1) * w, half * 2 * w:(half + 1) * 2 * w] = (
                jnp.where(valid, -ALIBI_SLOPES[h] * distf, NEG_INF))
            sinkcol_ref[half, slab * w:(slab + 1) * w, :] = jnp.full((w, 1), sink_ref[h], F32)

    kcat = jnp.concatenate([kp_ref[...], kc_ref[...]], axis=0)
    vcat = jnp.concatenate([vp_ref[...], vc_ref[...]], axis=0)
    krot = pltpu.roll(kcat, HEAD_DIM, 1)
    vrot = pltpu.roll(vcat, HEAD_DIM, 1)
    zero = jnp.zeros((), F32)
    k_low = [jnp.where(low, kcat, zero).astype(BF16), jnp.where(low, krot, zero).astype(BF16)]
    k_high = [jnp.where(low, zero, krot).astype(BF16), jnp.where(low, zero, kcat).astype(BF16)]
    v_low = [jnp.where(low, vcat, zero).astype(BF16), jnp.where(low, vrot, zero).astype(BF16)]
    v_high = [jnp.where(low, zero, vrot).astype(BF16), jnp.where(low, zero, vcat).astype(BF16)]
    ones_low = jnp.broadcast_to(jnp.where(low, 1.0, 0.0).astype(BF16), (2 * w, pair))
    ones_high = jnp.broadcast_to(jnp.where(low, 0.0, 1.0).astype(BF16), (2 * w, pair))
    ones_ext = jnp.concatenate([ones_low, ones_high], axis=0)
    col = lax.broadcasted_iota(jnp.int32, (1, 4 * w), 1)
    no_prev = (col % (2 * w) < w) & (i == 0)

    for jb in range(n_blocks):
        rows = slice(jb * w, (jb + 1) * w)
        keys = slice(jb * w, (jb + 2) * w)
        for g in range(N_KV_HEADS):
            q2 = jnp.concatenate([q_ref[rows, (2 * g) * pair:(2 * g + 1) * pair],
                                  q_ref[rows, (2 * g + 1) * pair:(2 * g + 2) * pair]], axis=0)
            kk = jnp.concatenate([k_low[g][keys], k_high[g][keys]], axis=0)
            s_ref[jb, 2 * g * w:(2 * g + 2) * w, :] = lax.dot_general(
                q2, kk, (((1,), (1,)), ((), ())), preferred_element_type=F32)

    for jb in range(n_blocks):
        rows = slice(jb * w, (jb + 1) * w)
        keys = slice(jb * w, (jb + 2) * w)
        bias = bias_ref[...]
        if jb == 0:
            bias = jnp.where(no_prev, NEG_INF, bias)
        s = s_ref[jb] + bias
        ps, sink_terms = [], []
        for half in range(2):
            sh = s[:, half * 2 * w:(half + 1) * 2 * w]
            sink = sinkcol_ref[half]
            m = jnp.maximum(jnp.max(sh, axis=-1, keepdims=True), sink)
            ps.append(jnp.exp(sh - m).astype(BF16))
            sink_terms.append(jnp.exp(sink - m))
        p = jnp.concatenate(ps, axis=1)
        sink_term = jnp.where(low, sink_terms[0], sink_terms[1])
        for g in range(N_KV_HEADS):
            vv = jnp.concatenate([jnp.concatenate([v_low[g][keys], v_high[g][keys]], axis=0), ones_ext], axis=1)
            o = jnp.dot(p[2 * g * w:(2 * g + 2) * w], vv, preferred_element_type=F32)
            o = o[:, :pair] / (o[:, pair:] + sink_term[2 * g * w:(2 * g + 2) * w])
            o_ref[rows, (2 * g) * pair:(2 * g + 1) * pair] = o[:w].astype(o_ref.dtype)
            o_ref[rows, (2 * g + 1) * pair:(2 * g + 2) * pair] = o[w:].astype(o_ref.dtype)


def _attn_prompt(q, k, v, sinks):
    n, t, _ = q.shape
    tt = ATTN_TIME_TILE
    cur = lambda width: pl.BlockSpec((None, tt, width), lambda b, i: (b, i, 0))
    prev = lambda width: pl.BlockSpec((None, WINDOW, width),
                                      lambda b, i: (b, jnp.maximum(i * (tt // WINDOW) - 1, 0), 0))
    return pl.pallas_call(
        _attn_prompt_body,
        grid=(n, t // tt),
        in_specs=[pl.BlockSpec(memory_space=pltpu.SMEM), cur(D_ATTN), cur(D_KV), prev(D_KV), cur(D_KV), prev(D_KV)],
        out_specs=cur(D_ATTN),
        out_shape=jax.ShapeDtypeStruct((n, t, D_ATTN), BF16),
        scratch_shapes=[pltpu.VMEM((N_HEADS // 2 * WINDOW, 4 * WINDOW), F32),
                        pltpu.VMEM((2, N_HEADS // 2 * WINDOW, 1), F32),
                        pltpu.VMEM((tt // WINDOW, N_HEADS // 2 * WINDOW, 4 * WINDOW), F32)],
        compiler_params=_params(2),
        name="attn_prompt",
    )(sinks, q, k, k, v, v)


def _window_sums(vals, w, n_out):
    shared = list(range(n_out - 1, w))
    base = None
    for e in shared:
        base = vals[e] if base is None else base + vals[e]
    sums = []
    for i in range(n_out):
        acc = base
        for e in range(i, i + w):
            if e not in shared:
                acc = vals[e] if acc is None else acc + vals[e]
        sums.append(acc)
    return sums


def _pool_deltas_strided(ext_ref, d_ref, pos0, n_rows):
    n_slabs = ext_ref.shape[0]
    slabs_per_group = n_slabs // len(POOL_WINDOWS)
    rows = 8 * POOL_PHASES

    def chunk(c, carry):
        r0 = c * rows
        sub = lax.broadcasted_iota(jnp.int32, (8, LANES), 0)
        pos = [pos0 + r0 + ph + POOL_PHASES * sub for ph in range(POOL_PHASES)]
        inv = {w: [1.0 / jnp.minimum(w, p + 1).astype(F32) for p in pos] for w in POOL_WINDOWS}
        for slab in range(n_slabs):
            w = POOL_WINDOWS[slab // slabs_per_group]
            first = POOL_HALO - (w - 1)
            vals = [ext_ref[slab, pl.ds(r0 + first + e, 8, stride=POOL_PHASES), :] for e in range(w + POOL_PHASES - 1)]
            sums = _window_sums(vals, w, POOL_PHASES)
            for ph in range(POOL_PHASES):
                d_ref[slab, pl.ds(r0 + ph, 8, stride=POOL_PHASES), :] = sums[ph] * inv[w][ph] - vals[ph + w - 1]
        return carry

    lax.fori_loop(0, n_rows // rows, chunk, 0)


def _pool_ffn_prompt_body(x_ref, g_ref, w_ref, sc_ref, gf_ref, wg_ref, wu_ref, wd_ref, o_ref, tail_ref, ext_ref, d_ref):
    tt, d = x_ref.shape
    n_slabs = d // LANES
    slabs_per_group = n_slabs // len(POOL_WINDOWS)
    i = pl.program_id(1)

    @pl.when(i == 0)
    def _():
        ext_ref[:, 0:POOL_HALO, :] = jnp.zeros((n_slabs, POOL_HALO, LANES), F32)

    @pl.when(i > 0)
    def _():
        ext_ref[:, 0:POOL_HALO, :] = ext_ref[:, tt:tt + POOL_HALO, :]

    x = x_ref[...]
    xn = _rmsnorm(x, g_ref[...])
    for s in range(n_slabs):
        ext_ref[s, POOL_HALO:, :] = xn[:, s * LANES:(s + 1) * LANES]
    tail_ref[...] = xn[tt - POOL_HALO:, :]
    _pool_deltas_strided(ext_ref, d_ref, i * tt, tt)
    ys = []
    for g in range(len(POOL_WINDOWS)):
        dg = jnp.concatenate([d_ref[s] for s in range(g * slabs_per_group, (g + 1) * slabs_per_group)], axis=-1)
        ys.append(jnp.dot(dg.astype(BF16), w_ref[g], preferred_element_type=F32))
    x1 = x + jnp.concatenate(ys, axis=-1) * sc_ref[...]
    o_ref[...] = _ffn_apply(x1, gf_ref, wg_ref, wu_ref, wd_ref)


def _pool_ffn_prompt(x, norm_g, w_pool, pool_scale, layer, i_odd, norm_ffn, w_gate, w_up, w_down):
    n, t, d = x.shape
    tt = POOL_TIME_TILE
    ng, gd, _ = w_pool.shape[1:]
    tile = pl.BlockSpec((None, tt, d), lambda b, i: (b, i, 0))
    return pl.pallas_call(
        _pool_ffn_prompt_body,
        grid=(n, t // tt),
        in_specs=[tile,
                  pl.BlockSpec((None, 1, d), lambda b, i: (layer, 0, 0)),
                  pl.BlockSpec((None, ng, gd, gd), lambda b, i: (i_odd, 0, 0, 0)),
                  pl.BlockSpec((None, 1, d), lambda b, i: (i_odd, 0, 0))] + _ffn_specs(w_gate, layer, 2),
        out_specs=[tile, pl.BlockSpec((None, POOL_HALO, d), lambda b, i: (b, 0, 0))],
        out_shape=[jax.ShapeDtypeStruct((n, t, d), F32), jax.ShapeDtypeStruct((n, POOL_HALO, d), F32)],
        scratch_shapes=[pltpu.VMEM((d // LANES, POOL_HALO + tt, LANES), F32), pltpu.VMEM((d // LANES, tt, LANES), F32)],
        compiler_params=_params(2),
        name="pool_ffn_prompt",
    )(x, norm_g, w_pool, pool_scale, norm_ffn, w_gate, w_up, w_down)


def _state_alias(states, first_input, first_output):
    if states is None:
        return [], [], {}
    aliases = {first_input + j: first_output + j for j in range(len(states))}
    return list(states), [pl.BlockSpec(memory_space=pl.ANY)] * len(states), aliases


def _conv_sample_body(cache_ref, glu_ref, w_ref, b_ref, g_ref, beta_ref, *refs):
    o_ref, new_ref = refs[-2:]
    hist = cache_ref.shape[0]
    t = glu_ref.shape[0]
    ext = lambda j: cache_ref[j] if j < hist else glu_ref[j - hist]
    rounded = lambda a: a.astype(BF16).astype(F32)
    for j in range(hist):
        new_ref[j] = ext(t + j)
    for step in range(t):
        acc = rounded(w_ref[0:1, :]) * rounded(ext(step)) + b_ref[...]
        for k in range(1, CONV_WIDTH):
            acc = acc + rounded(w_ref[k:k + 1, :]) * rounded(ext(step + k))
        o_ref[step] = _layernorm_silu(acc, g_ref[...], beta_ref[...]).astype(o_ref.dtype)


def _conv_sample(cache_tm, glu_tm, w_dw, b_dw, ln_g, ln_b, i_even, states):
    n_even, hist, n, c = cache_tm.shape
    t = glu_tm.shape[0]
    nb = SAMPLE_SEQ_TILE
    vec = pl.BlockSpec((None, 1, c), lambda b: (i_even, 0, 0))
    slab = pl.BlockSpec((None, hist, nb, c), lambda b: (i_even, 0, b, 0))
    alias_in, alias_specs, aliases = _state_alias(states, 6, 1)
    return pl.pallas_call(
        _conv_sample_body,
        grid=(n // nb,),
        in_specs=[slab,
                  pl.BlockSpec((t, nb, c), lambda b: (0, b, 0)),
                  pl.BlockSpec((None, CONV_WIDTH, c), lambda b: (i_even, 0, 0)),
                  vec, vec, vec] + alias_specs,
        out_specs=[pl.BlockSpec((t, nb, c), lambda b: (0, b, 0)), slab],
        out_shape=[jax.ShapeDtypeStruct((t, n, c), BF16), jax.ShapeDtypeStruct((n_even, hist, n, c), F32)],
        input_output_aliases=aliases,
        compiler_params=_params(1),
        name="conv_sample",
    )(cache_tm, glu_tm, w_dw, b_dw, ln_g, ln_b, *alias_in)


def _attn_sample_body(sink_ref, q_ref, kc_ref, kn_ref, vc_ref, vn_ref, *refs):
    o_ref, knew_ref, vnew_ref = refs[-3:]
    nb, nkv, rows, hd = q_ref.shape
    buf = kc_ref.shape[-1]
    t = kn_ref.shape[-1]
    batch = nb * nkv
    q = q_ref[...].reshape(batch, rows, hd)
    kc = kc_ref[...].reshape(batch, hd, buf)
    kn = kn_ref[...].reshape(batch, hd, t)
    vc = vc_ref[...].reshape(batch, hd, buf)
    vn = vn_ref[...].reshape(batch, hd, t)

    lane = lax.broadcasted_iota(jnp.int32, (1, 1, buf), 2)
    pad = jnp.zeros((batch, hd, buf - t), F32)
    k_new = jnp.where(lane < buf - t, pltpu.roll(kc, buf - t, 2), jnp.concatenate([pad, kn], axis=-1))
    v_new = jnp.where(lane < buf - t, pltpu.roll(vc, buf - t, 2), jnp.concatenate([pad, vn], axis=-1))
    knew_ref[...] = k_new.reshape(nb, nkv, hd, buf)
    vnew_ref[...] = v_new.reshape(nb, nkv, hd, buf)

    s_c = jnp.einsum("bqd,bdk->bqk", q, kc.astype(BF16), preferred_element_type=F32)
    s_n = jnp.einsum("bqd,bdk->bqk", q, kn.astype(BF16), preferred_element_type=F32)
    step = lax.broadcasted_iota(jnp.int32, (rows, 1), 0) % t
    head_in_group = lax.broadcasted_iota(jnp.int32, (rows, 1), 0) // t
    dist_c = buf + step - lax.broadcasted_iota(jnp.int32, (rows, buf), 1)
    dist_n = step - lax.broadcasted_iota(jnp.int32, (rows, t), 1)
    s_c = s_c.reshape(nb, nkv, rows, buf)
    s_n = s_n.reshape(nb, nkv, rows, t)
    ps_c, ps_n, invs = [], [], []
    for kv in range(nkv):
        slope = jnp.zeros((rows, 1), F32)
        sink = jnp.zeros((rows, 1), F32)
        for j in range(GQA_GROUP):
            slope = jnp.where(head_in_group == j, ALIBI_SLOPES[kv * GQA_GROUP + j], slope)
            sink = jnp.where(head_in_group == j, sink_ref[kv * GQA_GROUP + j], sink)
        sc = jnp.where((dist_c >= 0) & (dist_c < WINDOW), s_c[:, kv] - slope * dist_c.astype(F32), NEG_INF)
        sn = jnp.where((dist_n >= 0) & (dist_n < WINDOW), s_n[:, kv] - slope * dist_n.astype(F32), NEG_INF)
        m = jnp.maximum(jnp.maximum(jnp.max(sc, axis=-1, keepdims=True), jnp.max(sn, axis=-1, keepdims=True)), sink)
        pc = jnp.exp(sc - m)
        pn = jnp.exp(sn - m)
        denom = jnp.sum(pc, axis=-1, keepdims=True) + jnp.sum(pn, axis=-1, keepdims=True) + jnp.exp(sink - m)
        ps_c.append(pc.astype(BF16))
        ps_n.append(pn.astype(BF16))
        invs.append(1.0 / denom)
    p_c = jnp.stack(ps_c, axis=1).reshape(batch, rows, buf)
    p_n = jnp.stack(ps_n, axis=1).reshape(batch, rows, t)
    inv = jnp.stack(invs, axis=1).reshape(batch, rows, 1)
    o = jnp.einsum("bqk,bdk->bqd", p_c, vc.astype(BF16), preferred_element_type=F32)
    o = o + jnp.einsum("bqk,bdk->bqd", p_n, vn.astype(BF16), preferred_element_type=F32)
    o_ref[...] = (o * inv).reshape(nb, nkv, rows, hd).astype(o_ref.dtype)


def _attn_sample(q, kc, kn, vc, vn, sinks, i_even, states):
    n, nkv, rows, hd = q.shape
    n_even = kc.shape[0]
    buf = kc.shape[-1]
    t = kn.shape[-1]
    nb = SAMPLE_SEQ_TILE
    cache = pl.BlockSpec((None, nb, nkv, hd, buf), lambda b: (i_even, b, 0, 0, 0))
    new = pl.BlockSpec((nb, nkv, hd, t), lambda b: (b, 0, 0, 0))
    qspec = pl.BlockSpec((nb, nkv, rows, hd), lambda b: (b, 0, 0, 0))
    alias_in, alias_specs, aliases = _state_alias(states, 6, 1)
    state_shape = jax.ShapeDtypeStruct((n_even, n, nkv, hd, buf), F32)
    return pl.pallas_call(
        _attn_sample_body,
        grid=(n // nb,),
        in_specs=[pl.BlockSpec(memory_space=pltpu.SMEM), qspec, cache, new, cache, new] + alias_specs,
        out_specs=[qspec, cache, cache],
        out_shape=[jax.ShapeDtypeStruct((n, nkv, rows, hd), BF16), state_shape, state_shape],
        input_output_aliases=aliases,
        compiler_params=_params(1),
        name="attn_sample",
    )(sinks, q, kc, kn, vc, vn, *alias_in)


def _pool_ffn_sample_body(x_ref, st_ref, g_ref, w_ref, sc_ref, gf_ref, wg_ref, wu_ref, wd_ref, *refs, pos0):
    o_ref, new_ref = refs[-2:]
    t, nb, d = x_ref.shape
    hist = st_ref.shape[0]
    gdim = d // len(POOL_WINDOWS)
    x = x_ref[...].reshape(t * nb, d)
    xn = _rmsnorm(x, g_ref[...])
    ext = lambda j: st_ref[j] if j < hist else xn[(j - hist) * nb:(j - hist + 1) * nb]
    for j in range(hist):
        new_ref[j] = ext(t + j)
    ys = []
    for g, w in enumerate(POOL_WINDOWS):
        lanes = slice(g * gdim, (g + 1) * gdim)
        steps = []
        for step in range(t):
            acc = ext(hist + step)[:, lanes]
            for j in range(1, w):
                acc = acc + ext(hist + step - j)[:, lanes]
            cnt = float(min(w, pos0 + step + 1))
            steps.append(acc / cnt - xn[step * nb:(step + 1) * nb, lanes])
        dg = jnp.concatenate(steps, axis=0)
        ys.append(jnp.dot(dg.astype(BF16), w_ref[g], preferred_element_type=F32))
    x1 = x + jnp.concatenate(ys, axis=-1) * sc_ref[...]
    o_ref[...] = _ffn_apply(x1, gf_ref, wg_ref, wu_ref, wd_ref).reshape(t, nb, d)


def _pool_ffn_sample(x_tm, state_tm, norm_g, w_pool, pool_scale, layer, i_odd, pos0, norm_ffn, w_gate, w_up, w_down,
                     states):
    t, n, d = x_tm.shape
    n_odd, hist = state_tm.shape[:2]
    nb = SAMPLE_FFN_SEQ_TILE
    ng, gd, _ = w_pool.shape[1:]
    slab = pl.BlockSpec((None, hist, nb, d), lambda b: (i_odd, 0, b, 0))
    alias_in, alias_specs, aliases = _state_alias(states, 9, 1)
    return pl.pallas_call(
        functools.partial(_pool_ffn_sample_body, pos0=pos0),
        grid=(n // nb,),
        in_specs=[pl.BlockSpec((t, nb, d), lambda b: (0, b, 0)),
                  slab,
                  pl.BlockSpec((None, 1, d), lambda b: (layer, 0, 0)),
                  pl.BlockSpec((None, ng, gd, gd), lambda b: (i_odd, 0, 0, 0)),
                  pl.BlockSpec((None, 1, d), lambda b: (i_odd, 0, 0))] + _ffn_specs(w_gate, layer, 1) + alias_specs,
        out_specs=[pl.BlockSpec((t, nb, d), lambda b: (0, b, 0)), slab],
        out_shape=[jax.ShapeDtypeStruct((t, n, d), F32), jax.ShapeDtypeStruct((n_odd, hist, n, d), F32)],
        input_output_aliases=aliases,
        compiler_params=_params(1),
        name="pool_ffn_sample",
    )(x_tm, state_tm, norm_g, w_pool, pool_scale, norm_ffn, w_gate, w_up, w_down, *alias_in)


def _prepare_weights(norm_mix, norm_ffn, w_in, q_norm, k_norm, sinks, w_dw, b_dw, conv_norm_g, conv_norm_b, w_out,
                     w_pool, pool_scale, w_gate, w_up, w_down):
    vec = lambda a: a[:, None, :]
    return dict(
        norm_mix=vec(norm_mix), norm_ffn=vec(norm_ffn), w_in=w_in.astype(BF16),
        q_gain=vec(jnp.tile(q_norm, (1, N_HEADS))), k_gain=vec(jnp.tile(k_norm, (1, N_KV_HEADS))), sinks=sinks,
        w_dw=w_dw, b_dw=vec(b_dw),
        w_taps=jnp.broadcast_to(w_dw.astype(BF16)[:, :, None, :], w_dw.shape[:2] + (16, w_dw.shape[2])),
        ln_g=vec(conv_norm_g), ln_b=vec(conv_norm_b), w_out=w_out.astype(BF16),
        w_pool=w_pool.astype(BF16), pool_scale=vec(pool_scale),
        w_gate=w_gate.astype(BF16), w_up=w_up.astype(BF16), w_down=w_down.astype(BF16))


def _trunk_prompt(x, p, depth):
    n, t, d = x.shape
    xt = x.reshape(n * t, d)
    ffn = lambda layer: (p["norm_ffn"], p["w_gate"], p["w_up"], p["w_down"], layer)
    new_conv, new_k, new_v, new_pool = [], [], [], []
    for layer in range(depth):
        i = layer // 2
        if layer % 2 == 0:
            glu, q, k, v = _inproj(xt, p["norm_mix"], p["w_in"], p["q_gain"], p["k_gain"], layer, i)
            glu, q, k, v = (a.reshape(n, t, -1) for a in (glu, q, k, v))
            conv_out = _conv_prompt(glu, p["w_taps"], p["b_dw"], p["ln_g"], p["ln_b"], i)
            attn = _attn_prompt(q, k, v, p["sinks"][i])
            xt = _outproj_ffn(xt, conv_out.reshape(n * t, -1), attn.reshape(n * t, -1), p["w_out"], i, *ffn(layer))
            keep = min(WINDOW, t)
            new_conv.append(glu[:, t - (CONV_WIDTH - 1):])
            new_k.append(k[:, t - keep:].reshape(n, keep, N_KV_HEADS, HEAD_DIM))
            new_v.append(v[:, t - keep:].reshape(n, keep, N_KV_HEADS, HEAD_DIM))
        else:
            y, tail = _pool_ffn_prompt(xt.reshape(n, t, d), p["norm_mix"], p["w_pool"], p["pool_scale"], layer, i,
                                       *ffn(layer)[:-1])
            xt = y.reshape(n * t, d)
            new_pool.append(tail[:, POOL_HALO - POOL_BUF:])
    return xt.reshape(n, t, d), jnp.stack(new_conv), jnp.stack(new_k), jnp.stack(new_v), jnp.stack(new_pool)


def _trunk_sample(x, conv_bufs, k_bufs, v_bufs, pool_bufs, pos0, p, depth):
    n, t, d = x.shape
    x_tm = x.transpose(1, 0, 2)
    conv_tm = conv_bufs.transpose(0, 2, 1, 3)
    pool_tm = pool_bufs.transpose(0, 2, 1, 3)
    k_t = k_bufs.transpose(0, 1, 3, 4, 2)
    v_t = v_bufs.transpose(0, 1, 3, 4, 2)
    ffn = lambda layer: (p["norm_ffn"], p["w_gate"], p["w_up"], p["w_down"], layer)
    conv_new = kv_new = pool_new = None
    for layer in range(depth):
        i = layer // 2
        if layer % 2 == 0:
            xt = x_tm.reshape(t * n, d)
            glu, q, k, v = _inproj(xt, p["norm_mix"], p["w_in"], p["q_gain"], p["k_gain"], layer, i)
            conv_out, conv_new = _conv_sample(conv_tm, glu.reshape(t, n, D_CONV), p["w_dw"], p["b_dw"], p["ln_g"],
                                              p["ln_b"], i, None if conv_new is None else [conv_new])
            qh = q.reshape(t, n, N_KV_HEADS, GQA_GROUP, HEAD_DIM).transpose(1, 2, 3, 0, 4)
            qh = qh.reshape(n, N_KV_HEADS, GQA_GROUP * t, HEAD_DIM)
            kn = k.reshape(t, n, N_KV_HEADS, HEAD_DIM).transpose(1, 2, 3, 0)
            vn = v.reshape(t, n, N_KV_HEADS, HEAD_DIM).transpose(1, 2, 3, 0)
            oh, *kv_new = _attn_sample(qh, k_t, kn, v_t, vn, p["sinks"][i], i, kv_new)
            attn = oh.reshape(n, N_KV_HEADS, GQA_GROUP, t, HEAD_DIM).transpose(3, 0, 1, 2, 4).reshape(t * n, D_ATTN)
            xt = _outproj_ffn(xt, conv_out.reshape(t * n, D_CONV), attn, p["w_out"], i, *ffn(layer))
            x_tm = xt.reshape(t, n, d)
        else:
            x_tm, pool_new = _pool_ffn_sample(x_tm, pool_tm, p["norm_mix"], p["w_pool"], p["pool_scale"], layer, i, pos0,
                                              *ffn(layer)[:-1], None if pool_new is None else [pool_new])
    return (x_tm.transpose(1, 0, 2), conv_new.transpose(0, 2, 1, 3), kv_new[0].transpose(0, 1, 4, 2, 3),
            kv_new[1].transpose(0, 1, 4, 2, 3), pool_new.transpose(0, 2, 1, 3))


def kernel(x_prompt, x_sample, cache_conv, cache_k, cache_v, state_pool, norm_mix, norm_ffn, w_in, q_norm, k_norm,
           sinks, w_dw, b_dw, conv_norm_g, conv_norm_b, w_out, w_pool, pool_scale, w_gate, w_up, w_down):
    depth = norm_mix.shape[0]
    p = _prepare_weights(norm_mix, norm_ffn, w_in, q_norm, k_norm, sinks, w_dw, b_dw, conv_norm_g, conv_norm_b, w_out,
                         w_pool, pool_scale, w_gate, w_up, w_down)
    y_p, conv_p, k_p, v_p, pool_p = _trunk_prompt(x_prompt, p, depth)
    y_s, conv_s, k_s, v_s, pool_s = _trunk_sample(x_sample, cache_conv, cache_k, cache_v, state_pool, PAST_LEN, p, depth)
    return (y_p, y_s, conv_p, k_p, v_p, pool_p, conv_s, k_s, v_s, pool_s)
```

```python
import functools

import jax
import jax.numpy as jnp
from jax import lax
from jax.experimental import pallas as pl
from jax.experimental.pallas import tpu as pltpu

F32 = jnp.float32
BF16 = jnp.bfloat16

HEAD_DIM = 64
N_HEADS = 8
N_KV_HEADS = 2
GQA_GROUP = N_HEADS // N_KV_HEADS
D_ATTN = N_HEADS * HEAD_DIM
D_KV = N_KV_HEADS * HEAD_DIM
WINDOW = 128
PAST_LEN = 8192
D_CONV = 512
CONV_WIDTH = 31
CONV_HALO = 32
POOL_WINDOWS = (2, 4, 8, 16)
POOL_BUF = max(POOL_WINDOWS) - 1
POOL_HALO = 16
RMS_EPS = 1e-6
LN_EPS = 1e-5
NEG_INF = -1e30
LANES = 128
ALIBI_SLOPES = tuple(2.0 ** (-8.0 * (h + 1) / N_HEADS) for h in range(N_HEADS))

TOKEN_TILE = 512
CONV_PHASES = 4
ATTN_TIME_TILE = 512
POOL_TIME_TILE = 512
POOL_PHASES = 4
FFN_CHUNK = 1536
SAMPLE_SEQ_TILE = 32
SAMPLE_FFN_SEQ_TILE = 64
VMEM_LIMIT = 56 * 1024 * 1024


def _params(n_axes):
    return pltpu.CompilerParams(dimension_semantics=("arbitrary",) * n_axes, vmem_limit_bytes=VMEM_LIMIT)


def _rmsnorm(x, g):
    return x * lax.rsqrt(jnp.mean(x * x, axis=-1, keepdims=True) + RMS_EPS) * g


def _head_rmsnorm(x, g):
    lane = lax.broadcasted_iota(jnp.int32, (x.shape[0], LANES), 1)
    low = lane < HEAD_DIM
    outs = []
    for s in range(x.shape[1] // LANES):
        xs = x[:, s * LANES:(s + 1) * LANES]
        sq = xs * xs
        s_low = jnp.sum(jnp.where(low, sq, 0.0), axis=-1, keepdims=True)
        s_high = jnp.sum(jnp.where(low, 0.0, sq), axis=-1, keepdims=True)
        ms = jnp.where(low, s_low, s_high) * (1.0 / HEAD_DIM)
        outs.append(xs * lax.rsqrt(ms + RMS_EPS) * g[:, s * LANES:(s + 1) * LANES])
    return outs[0] if len(outs) == 1 else jnp.concatenate(outs, axis=-1)


def _layernorm_silu(y, g, b):
    mu = jnp.mean(y, axis=-1, keepdims=True)
    yc = y - mu
    var = jnp.mean(yc * yc, axis=-1, keepdims=True)
    z = yc * lax.rsqrt(var + LN_EPS) * g + b
    return z * jax.nn.sigmoid(z)


def _ffn_apply(x, g_ref, wg_ref, wu_ref, wd_ref):
    h = _rmsnorm(x, g_ref[...]).astype(BF16)
    acc = x
    d_ff = wg_ref.shape[1]
    for c0 in range(0, d_ff, FFN_CHUNK):
        c1 = min(c0 + FFN_CHUNK, d_ff)
        gate = jnp.dot(h, wg_ref[:, c0:c1], preferred_element_type=F32)
        up = jnp.dot(h, wu_ref[:, c0:c1], preferred_element_type=F32)
        a = (gate * jax.nn.sigmoid(gate) * up).astype(BF16)
        acc = acc + jnp.dot(a, wd_ref[c0:c1, :], preferred_element_type=F32)
    return acc


def _ffn_specs(w_gate, layer, n_axes):
    _, d, f = w_gate.shape
    at_layer = {1: lambda i: (layer, 0, 0), 2: lambda b, i: (layer, 0, 0)}[n_axes]
    resident = functools.partial(pl.BlockSpec, pipeline_mode=pl.Buffered(1))
    return [pl.BlockSpec((None, 1, d), at_layer), resident((None, d, f), at_layer), resident((None, d, f), at_layer),
            resident((None, f, d), at_layer)]


def _inproj_body(x_ref, g_ref, w_ref, qn_ref, kn_ref, glu_ref, q_ref, k_ref, v_ref):
    h = _rmsnorm(x_ref[...], g_ref[...]).astype(BF16)
    o_q = 2 * D_CONV
    ua = jnp.dot(h, w_ref[:, o_q:], preferred_element_type=F32)
    uc = jnp.dot(h, w_ref[:, :o_q], preferred_element_type=F32)
    q_ref[...] = (_head_rmsnorm(ua[:, :D_ATTN], qn_ref[...]) * (HEAD_DIM ** -0.5)).astype(BF16)
    k_ref[...] = _head_rmsnorm(ua[:, D_ATTN:D_ATTN + D_KV], kn_ref[...])
    v_ref[...] = ua[:, D_ATTN + D_KV:]
    glu_ref[...] = uc[:, :D_CONV] * jax.nn.sigmoid(uc[:, D_CONV:])


def _inproj(x, norm_g, w_in, q_gain, k_gain, layer, i_even):
    t, d = x.shape
    n = w_in.shape[2]
    row = lambda width: pl.BlockSpec((TOKEN_TILE, width), lambda i: (i, 0))
    return pl.pallas_call(
        _inproj_body,
        grid=(t // TOKEN_TILE,),
        in_specs=[row(d),
                  pl.BlockSpec((None, 1, d), lambda i: (layer, 0, 0)),
                  pl.BlockSpec((None, d, n), lambda i: (i_even, 0, 0), pipeline_mode=pl.Buffered(1)),
                  pl.BlockSpec((None, 1, D_ATTN), lambda i: (i_even, 0, 0)),
                  pl.BlockSpec((None, 1, D_KV), lambda i: (i_even, 0, 0))],
        out_specs=[row(D_CONV), row(D_ATTN), row(D_KV), row(D_KV)],
        out_shape=[jax.ShapeDtypeStruct((t, D_CONV), F32), jax.ShapeDtypeStruct((t, D_ATTN), BF16),
                   jax.ShapeDtypeStruct((t, D_KV), F32), jax.ShapeDtypeStruct((t, D_KV), F32)],
        compiler_params=_params(1),
        name="inproj",
    )(x, norm_g, w_in, q_gain, k_gain)


def _outproj_ffn_body(x_ref, c_ref, a_ref, w_ref, gf_ref, wg_ref, wu_ref, wd_ref, o_ref):
    y = jnp.dot(c_ref[...], w_ref[:D_CONV, :], preferred_element_type=F32)
    y = y + jnp.dot(a_ref[...], w_ref[D_CONV:, :], preferred_element_type=F32)
    o_ref[...] = _ffn_apply(x_ref[...] + y, gf_ref, wg_ref, wu_ref, wd_ref)


def _outproj_ffn(x, conv_out, attn, w_out, i_even, norm_ffn, w_gate, w_up, w_down, layer):
    t, d = x.shape
    row = lambda width: pl.BlockSpec((TOKEN_TILE, width), lambda i: (i, 0))
    return pl.pallas_call(
        _outproj_ffn_body,
        grid=(t // TOKEN_TILE,),
        in_specs=[row(d), row(D_CONV), row(D_ATTN),
                  pl.BlockSpec((None, D_CONV + D_ATTN, d), lambda i: (i_even, 0, 0), pipeline_mode=pl.Buffered(1))]
        + _ffn_specs(w_gate, layer, 1),
        out_specs=row(d),
        out_shape=jax.ShapeDtypeStruct((t, d), F32),
        compiler_params=_params(1),
        name="outproj_ffn",
    )(x, conv_out, attn, w_out, norm_ffn, w_gate, w_up, w_down)


def _depthwise_conv_halves(ext_ref, y_ref, w_ref, b_ref, slab, half_rows):
    lead = CONV_HALO - (CONV_WIDTH - 1)
    rows = 8 * CONV_PHASES
    lanes = slice(slab * LANES, (slab + 1) * LANES)

    def chunk(c, carry):
        r0 = c * rows
        acc = [jnp.zeros((16, LANES), F32) for _ in range(CONV_PHASES)]
        for e in range(CONV_WIDTH + CONV_PHASES - 1):
            v = jnp.concatenate([ext_ref[slab, pl.ds(r0 + lead + e, 8, stride=CONV_PHASES), :],
                                 ext_ref[slab, pl.ds(half_rows + r0 + lead + e, 8, stride=CONV_PHASES), :]],
                                axis=0).astype(BF16)
            for ph in range(CONV_PHASES):
                k = e - ph
                if 0 <= k < CONV_WIDTH:
                    acc[ph] = acc[ph] + v.astype(F32) * w_ref[k, :, lanes].astype(F32)
        bias = jnp.broadcast_to(b_ref[0:1, lanes], (16, LANES))
        for ph in range(CONV_PHASES):
            out = acc[ph] + bias
            y_ref[slab, pl.ds(r0 + ph, 8, stride=CONV_PHASES), :] = out[:8]
            y_ref[slab, pl.ds(half_rows + r0 + ph, 8, stride=CONV_PHASES), :] = out[8:]
        return carry

    lax.fori_loop(0, half_rows // rows, chunk, 0)


def _layernorm_silu_slabs(y_ref, g_ref, beta_ref, o_ref):
    n_slabs = y_ref.shape[0]
    ys = [y_ref[s] for s in range(n_slabs)]
    inv_c = 1.0 / (n_slabs * LANES)
    mu = jnp.sum(sum(ys), axis=-1, keepdims=True) * inv_c
    ycs = [y - mu for y in ys]
    var = jnp.sum(sum(yc * yc for yc in ycs), axis=-1, keepdims=True) * inv_c
    rstd = lax.rsqrt(var + LN_EPS)
    for s, yc in enumerate(ycs):
        lanes = slice(s * LANES, (s + 1) * LANES)
        z = yc * rstd * g_ref[:, lanes] + beta_ref[:, lanes]
        o_ref[:, lanes] = (z * jax.nn.sigmoid(z)).astype(o_ref.dtype)


def _banded_attention(sink_ref, q_ref, kcat, vcat, o_ref, bias_ref, sinkcol_ref, s_ref):
    b = pl.program_id(0)
    i = pl.program_id(1)
    w = WINDOW
    pair = 2 * HEAD_DIM
    n_blocks = q_ref.shape[0] // w
    lane = lax.broadcasted_iota(jnp.int32, (1, pair), 1)
    low = lane < HEAD_DIM

    @pl.when((b == 0) & (i == 0))
    def _():
        row = lax.broadcasted_iota(jnp.int32, (w, 2 * w), 0)
        col = lax.broadcasted_iota(jnp.int32, (w, 2 * w), 1)
        dist = w + row - col
        valid = (dist >= 0) & (dist < w)
        distf = dist.astype(F32)
        for h in range(N_HEADS):
            slab, half = h // 2, h % 2
            bias_ref[slab * w:(slab + 1) * w, half * 2 * w:(half + 1) * 2 * w] = (
                jnp.where(valid, -ALIBI_SLOPES[h] * distf, NEG_INF))
            sinkcol_ref[half, slab * w:(slab + 1) * w, :] = jnp.full((w, 1), sink_ref[h], F32)

    krot = pltpu.roll(kcat, HEAD_DIM, 1)
    vrot = pltpu.roll(vcat, HEAD_DIM, 1)
    zero = jnp.zeros((), F32)
    k_low = [jnp.where(low, kcat, zero).astype(BF16), jnp.where(low, krot, zero).astype(BF16)]
    k_high = [jnp.where(low, zero, krot).astype(BF16), jnp.where(low, zero, kcat).astype(BF16)]
    v_low = [jnp.where(low, vcat, zero).astype(BF16), jnp.where(low, vrot, zero).astype(BF16)]
    v_high = [jnp.where(low, zero, vrot).astype(BF16), jnp.where(low, zero, vcat).astype(BF16)]
    ones_low = jnp.broadcast_to(jnp.where(low, 1.0, 0.0).astype(BF16), (2 * w, pair))
    ones_high = jnp.broadcast_to(jnp.where(low, 0.0, 1.0).astype(BF16), (2 * w, pair))
    ones_ext = jnp.concatenate([ones_low, ones_high], axis=0)
    col = lax.broadcasted_iota(jnp.int32, (1, 4 * w), 1)
    no_prev = (col % (2 * w) < w) & (i == 0)

    for jb in range(n_blocks):
        rows = slice(jb * w, (jb + 1) * w)
        keys = slice(jb * w, (jb + 2) * w)
        for g in range(N_KV_HEADS):
            q2 = jnp.concatenate([q_ref[rows, (2 * g) * pair:(2 * g + 1) * pair],
                                  q_ref[rows, (2 * g + 1) * pair:(2 * g + 2) * pair]], axis=0)
            kk = jnp.concatenate([k_low[g][keys], k_high[g][keys]], axis=0)
            s_ref[jb, 2 * g * w:(2 * g + 2) * w, :] = lax.dot_general(
                q2, kk, (((1,), (1,)), ((), ())), preferred_element_type=F32)

    for jb in range(n_blocks):
        rows = slice(jb * w, (jb + 1) * w)
        keys = slice(jb * w, (jb + 2) * w)
        bias = bias_ref[...]
        if jb == 0:
            bias = jnp.where(no_prev, NEG_INF, bias)
        s = s_ref[jb] + bias
        ps, sink_terms = [], []
        for half in range(2):
            sh = s[:, half * 2 * w:(half + 1) * 2 * w]
            sink = sinkcol_ref[half]
            m = jnp.maximum(jnp.max(sh, axis=-1, keepdims=True), sink)
            ps.append(jnp.exp(sh - m).astype(BF16))
            sink_terms.append(jnp.exp(sink - m))
        p = jnp.concatenate(ps, axis=1)
        sink_term = jnp.where(low, sink_terms[0], sink_terms[1])
        for g in range(N_KV_HEADS):
            vv = jnp.concatenate([jnp.concatenate([v_low[g][keys], v_high[g][keys]], axis=0), ones_ext], axis=1)
            o = jnp.dot(p[2 * g * w:(2 * g + 2) * w], vv, preferred_element_type=F32)
            o = o[:, :pair] / (o[:, pair:] + sink_term[2 * g * w:(2 * g + 2) * w])
            o_ref[rows, (2 * g) * pair:(2 * g + 1) * pair] = o[:w].astype(o_ref.dtype)
            o_ref[rows, (2 * g + 1) * pair:(2 * g + 2) * pair] = o[w:].astype(o_ref.dtype)


def _attn_prompt_body(sink_ref, q_ref, kc_ref, kp_ref, vc_ref, vp_ref, o_ref, bias_ref, sinkcol_ref, s_ref):
    kcat = jnp.concatenate([kp_ref[...], kc_ref[...]], axis=0)
    vcat = jnp.concatenate([vp_ref[...], vc_ref[...]], axis=0)
    _banded_attention(sink_ref, q_ref, kcat, vcat, o_ref, bias_ref, sinkcol_ref, s_ref)


def _attn_prompt(q, k, v, sinks):
    n, t, _ = q.shape
    tt = ATTN_TIME_TILE
    cur = lambda width: pl.BlockSpec((None, tt, width), lambda b, i: (b, i, 0))
    prev = lambda width: pl.BlockSpec((None, WINDOW, width),
                                      lambda b, i: (b, jnp.maximum(i * (tt // WINDOW) - 1, 0), 0))
    return pl.pallas_call(
        _attn_prompt_body,
        grid=(n, t // tt),
        in_specs=[pl.BlockSpec(memory_space=pltpu.SMEM), cur(D_ATTN), cur(D_KV), prev(D_KV), cur(D_KV), prev(D_KV)],
        out_specs=cur(D_ATTN),
        out_shape=jax.ShapeDtypeStruct((n, t, D_ATTN), BF16),
        scratch_shapes=[pltpu.VMEM((N_HEADS // 2 * WINDOW, 4 * WINDOW), F32),
                        pltpu.VMEM((2, N_HEADS // 2 * WINDOW, 1), F32),
                        pltpu.VMEM((tt // WINDOW, N_HEADS // 2 * WINDOW, 4 * WINDOW), F32)],
        compiler_params=_params(2),
        name="attn_prompt",
    )(sinks, q, k, k, v, v)


def _inproj_conv_prompt_body(x_ref, g_ref, w_ref, qn_ref, kn_ref, wt_ref, b_ref, lg_ref, lb_ref,
                             conv_ref, q_ref, k_ref, v_ref, glu_tail_ref, ext_ref, y_ref):
    tt = x_ref.shape[0]
    n_slabs = D_CONV // LANES
    i = pl.program_id(1)

    @pl.when(i == 0)
    def _():
        ext_ref[:, 0:CONV_HALO, :] = jnp.zeros((n_slabs, CONV_HALO, LANES), F32)

    @pl.when(i > 0)
    def _():
        ext_ref[:, 0:CONV_HALO, :] = ext_ref[:, tt:tt + CONV_HALO, :]

    h = _rmsnorm(x_ref[...], g_ref[...]).astype(BF16)
    o_q = 2 * D_CONV
    ua = jnp.dot(h, w_ref[:, o_q:], preferred_element_type=F32)
    uc = jnp.dot(h, w_ref[:, :o_q], preferred_element_type=F32)
    q_ref[...] = (_head_rmsnorm(ua[:, :D_ATTN], qn_ref[...]) * (HEAD_DIM ** -0.5)).astype(BF16)
    k_ref[...] = _head_rmsnorm(ua[:, D_ATTN:D_ATTN + D_KV], kn_ref[...])
    v_ref[...] = ua[:, D_ATTN + D_KV:]
    glu = uc[:, :D_CONV] * jax.nn.sigmoid(uc[:, D_CONV:])
    glu_tail_ref[...] = glu[tt - CONV_HALO:, :]
    for s in range(n_slabs):
        ext_ref[s, CONV_HALO:, :] = glu[:, s * LANES:(s + 1) * LANES]
    for s in range(n_slabs):
        _depthwise_conv_halves(ext_ref, y_ref, wt_ref, b_ref, s, tt // 2)
    _layernorm_silu_slabs(y_ref, lg_ref, lb_ref, conv_ref)


def _inproj_conv_prompt(x, norm_g, w_in, q_gain, k_gain, w_taps, b_dw, ln_g, ln_b, layer, i_even):
    n, t, d = x.shape
    n_in = w_in.shape[2]
    tt = ATTN_TIME_TILE
    tile = lambda width: pl.BlockSpec((None, tt, width), lambda b, i: (b, i, 0))
    even = lambda width: pl.BlockSpec((None, 1, width), lambda b, i: (i_even, 0, 0))
    return pl.pallas_call(
        _inproj_conv_prompt_body,
        grid=(n, t // tt),
        in_specs=[tile(d),
                  pl.BlockSpec((None, 1, d), lambda b, i: (layer, 0, 0)),
                  pl.BlockSpec((None, d, n_in), lambda b, i: (i_even, 0, 0), pipeline_mode=pl.Buffered(1)),
                  even(D_ATTN), even(D_KV),
                  pl.BlockSpec((None, CONV_WIDTH, 16, D_CONV), lambda b, i: (i_even, 0, 0, 0)),
                  even(D_CONV), even(D_CONV), even(D_CONV)],
        out_specs=[tile(D_CONV), tile(D_ATTN), tile(D_KV), tile(D_KV),
                   pl.BlockSpec((None, CONV_HALO, D_CONV), lambda b, i: (b, 0, 0))],
        out_shape=[jax.ShapeDtypeStruct((n, t, D_CONV), BF16), jax.ShapeDtypeStruct((n, t, D_ATTN), BF16),
                   jax.ShapeDtypeStruct((n, t, D_KV), F32), jax.ShapeDtypeStruct((n, t, D_KV), F32),
                   jax.ShapeDtypeStruct((n, CONV_HALO, D_CONV), F32)],
        scratch_shapes=[pltpu.VMEM((D_CONV // LANES, CONV_HALO + tt, LANES), F32),
                        pltpu.VMEM((D_CONV // LANES, tt, LANES), F32)],
        compiler_params=_params(2),
        name="inproj_conv_prompt",
    )(x, norm_g, w_in, q_gain, k_gain, w_taps, b_dw, ln_g, ln_b)


def _window_sums(vals, w, n_out):
    shared = list(range(n_out - 1, w))
    base = None
    for e in shared:
        base = vals[e] if base is None else base + vals[e]
    sums = []
    for i in range(n_out):
        acc = base
        for e in range(i, i + w):
            if e not in shared:
                acc = vals[e] if acc is None else acc + vals[e]
        sums.append(acc)
    return sums


def _pool_deltas_strided(ext_ref, d_ref, pos0, n_rows):
    n_slabs = ext_ref.shape[0]
    slabs_per_group = n_slabs // len(POOL_WINDOWS)
    rows = 8 * POOL_PHASES

    def chunk(c, carry):
        r0 = c * rows
        sub = lax.broadcasted_iota(jnp.int32, (8, LANES), 0)
        pos = [pos0 + r0 + ph + POOL_PHASES * sub for ph in range(POOL_PHASES)]
        inv = {w: [1.0 / jnp.minimum(w, p + 1).astype(F32) for p in pos] for w in POOL_WINDOWS}
        for slab in range(n_slabs):
            w = POOL_WINDOWS[slab // slabs_per_group]
            first = POOL_HALO - (w - 1)
            vals = [ext_ref[slab, pl.ds(r0 + first + e, 8, stride=POOL_PHASES), :] for e in range(w + POOL_PHASES - 1)]
            sums = _window_sums(vals, w, POOL_PHASES)
            for ph in range(POOL_PHASES):
                d_ref[slab, pl.ds(r0 + ph, 8, stride=POOL_PHASES), :] = sums[ph] * inv[w][ph] - vals[ph + w - 1]
        return carry

    lax.fori_loop(0, n_rows // rows, chunk, 0)


def _pool_ffn_prompt_body(x_ref, g_ref, w_ref, sc_ref, gf_ref, wg_ref, wu_ref, wd_ref, o_ref, tail_ref, ext_ref, d_ref):
    tt, d = x_ref.shape
    n_slabs = d // LANES
    slabs_per_group = n_slabs // len(POOL_WINDOWS)
    i = pl.program_id(1)

    @pl.when(i == 0)
    def _():
        ext_ref[:, 0:POOL_HALO, :] = jnp.zeros((n_slabs, POOL_HALO, LANES), F32)

    @pl.when(i > 0)
    def _():
        ext_ref[:, 0:POOL_HALO, :] = ext_ref[:, tt:tt + POOL_HALO, :]

    x = x_ref[...]
    xn = _rmsnorm(x, g_ref[...])
    for s in range(n_slabs):
        ext_ref[s, POOL_HALO:, :] = xn[:, s * LANES:(s + 1) * LANES]
    tail_ref[...] = xn[tt - POOL_HALO:, :]
    _pool_deltas_strided(ext_ref, d_ref, i * tt, tt)
    ys = []
    for g in range(len(POOL_WINDOWS)):
        dg = jnp.concatenate([d_ref[s] for s in range(g * slabs_per_group, (g + 1) * slabs_per_group)], axis=-1)
        ys.append(jnp.dot(dg.astype(BF16), w_ref[g], preferred_element_type=F32))
    x1 = x + jnp.concatenate(ys, axis=-1) * sc_ref[...]
    o_ref[...] = _ffn_apply(x1, gf_ref, wg_ref, wu_ref, wd_ref)


def _pool_ffn_prompt(x, norm_g, w_pool, pool_scale, layer, i_odd, norm_ffn, w_gate, w_up, w_down):
    n, t, d = x.shape
    tt = POOL_TIME_TILE
    ng, gd, _ = w_pool.shape[1:]
    tile = pl.BlockSpec((None, tt, d), lambda b, i: (b, i, 0))
    return pl.pallas_call(
        _pool_ffn_prompt_body,
        grid=(n, t // tt),
        in_specs=[tile,
                  pl.BlockSpec((None, 1, d), lambda b, i: (layer, 0, 0)),
                  pl.BlockSpec((None, ng, gd, gd), lambda b, i: (i_odd, 0, 0, 0)),
                  pl.BlockSpec((None, 1, d), lambda b, i: (i_odd, 0, 0))] + _ffn_specs(w_gate, layer, 2),
        out_specs=[tile, pl.BlockSpec((None, POOL_HALO, d), lambda b, i: (b, 0, 0))],
        out_shape=[jax.ShapeDtypeStruct((n, t, d), F32), jax.ShapeDtypeStruct((n, POOL_HALO, d), F32)],
        scratch_shapes=[pltpu.VMEM((d // LANES, POOL_HALO + tt, LANES), F32), pltpu.VMEM((d // LANES, tt, LANES), F32)],
        compiler_params=_params(2),
        name="pool_ffn_prompt",
    )(x, norm_g, w_pool, pool_scale, norm_ffn, w_gate, w_up, w_down)


def _state_alias(states, first_input, first_output):
    if states is None:
        return [], [], {}
    aliases = {first_input + j: first_output + j for j in range(len(states))}
    return list(states), [pl.BlockSpec(memory_space=pl.ANY)] * len(states), aliases


def _conv_sample_body(cache_ref, glu_ref, w_ref, b_ref, g_ref, beta_ref, *refs):
    o_ref, new_ref = refs[-2:]
    hist = cache_ref.shape[0]
    t = glu_ref.shape[0]
    ext = lambda j: cache_ref[j] if j < hist else glu_ref[j - hist]
    rounded = lambda a: a.astype(BF16).astype(F32)
    for j in range(hist):
        new_ref[j] = ext(t + j)
    for step in range(t):
        acc = rounded(w_ref[0:1, :]) * rounded(ext(step)) + b_ref[...]
        for k in range(1, CONV_WIDTH):
            acc = acc + rounded(w_ref[k:k + 1, :]) * rounded(ext(step + k))
        o_ref[step] = _layernorm_silu(acc, g_ref[...], beta_ref[...]).astype(o_ref.dtype)


def _conv_sample(cache_tm, glu_tm, w_dw, b_dw, ln_g, ln_b, i_even, states):
    n_even, hist, n, c = cache_tm.shape
    t = glu_tm.shape[0]
    nb = SAMPLE_SEQ_TILE
    vec = pl.BlockSpec((None, 1, c), lambda b: (i_even, 0, 0))
    slab = pl.BlockSpec((None, hist, nb, c), lambda b: (i_even, 0, b, 0))
    alias_in, alias_specs, aliases = _state_alias(states, 6, 1)
    return pl.pallas_call(
        _conv_sample_body,
        grid=(n // nb,),
        in_specs=[slab,
                  pl.BlockSpec((t, nb, c), lambda b: (0, b, 0)),
                  pl.BlockSpec((None, CONV_WIDTH, c), lambda b: (i_even, 0, 0)),
                  vec, vec, vec] + alias_specs,
        out_specs=[pl.BlockSpec((t, nb, c), lambda b: (0, b, 0)), slab],
        out_shape=[jax.ShapeDtypeStruct((t, n, c), BF16), jax.ShapeDtypeStruct((n_even, hist, n, c), F32)],
        input_output_aliases=aliases,
        compiler_params=_params(1),
        name="conv_sample",
    )(cache_tm, glu_tm, w_dw, b_dw, ln_g, ln_b, *alias_in)


def _attn_sample_body(sink_ref, q_ref, kc_ref, kn_ref, vc_ref, vn_ref, *refs):
    o_ref, knew_ref, vnew_ref = refs[-3:]
    nb, nkv, rows, hd = q_ref.shape
    buf = kc_ref.shape[-1]
    t = kn_ref.shape[-1]
    batch = nb * nkv
    q = q_ref[...].reshape(batch, rows, hd)
    kc = kc_ref[...].reshape(batch, hd, buf)
    kn = kn_ref[...].reshape(batch, hd, t)
    vc = vc_ref[...].reshape(batch, hd, buf)
    vn = vn_ref[...].reshape(batch, hd, t)

    lane = lax.broadcasted_iota(jnp.int32, (1, 1, buf), 2)
    pad = jnp.zeros((batch, hd, buf - t), F32)
    k_new = jnp.where(lane < buf - t, pltpu.roll(kc, buf - t, 2), jnp.concatenate([pad, kn], axis=-1))
    v_new = jnp.where(lane < buf - t, pltpu.roll(vc, buf - t, 2), jnp.concatenate([pad, vn], axis=-1))
    knew_ref[...] = k_new.reshape(nb, nkv, hd, buf)
    vnew_ref[...] = v_new.reshape(nb, nkv, hd, buf)

    s_c = jnp.einsum("bqd,bdk->bqk", q, kc.astype(BF16), preferred_element_type=F32)
    s_n = jnp.einsum("bqd,bdk->bqk", q, kn.astype(BF16), preferred_element_type=F32)
    step = lax.broadcasted_iota(jnp.int32, (rows, 1), 0) % t
    head_in_group = lax.broadcasted_iota(jnp.int32, (rows, 1), 0) // t
    dist_c = buf + step - lax.broadcasted_iota(jnp.int32, (rows, buf), 1)
    dist_n = step - lax.broadcasted_iota(jnp.int32, (rows, t), 1)
    s_c = s_c.reshape(nb, nkv, rows, buf)
    s_n = s_n.reshape(nb, nkv, rows, t)
    ps_c, ps_n, invs = [], [], []
    for kv in range(nkv):
        slope = jnp.zeros((rows, 1), F32)
        sink = jnp.zeros((rows, 1), F32)
        for j in range(GQA_GROUP):
            slope = jnp.where(head_in_group == j, ALIBI_SLOPES[kv * GQA_GROUP + j], slope)
            sink = jnp.where(head_in_group == j, sink_ref[kv * GQA_GROUP + j], sink)
        sc = jnp.where((dist_c >= 0) & (dist_c < WINDOW), s_c[:, kv] - slope * dist_c.astype(F32), NEG_INF)
        sn = jnp.where((dist_n >= 0) & (dist_n < WINDOW), s_n[:, kv] - slope * dist_n.astype(F32), NEG_INF)
        m = jnp.maximum(jnp.maximum(jnp.max(sc, axis=-1, keepdims=True), jnp.max(sn, axis=-1, keepdims=True)), sink)
        pc = jnp.exp(sc - m)
        pn = jnp.exp(sn - m)
        denom = jnp.sum(pc, axis=-1, keepdims=True) + jnp.sum(pn, axis=-1, keepdims=True) + jnp.exp(sink - m)
        ps_c.append(pc.astype(BF16))
        ps_n.append(pn.astype(BF16))
        invs.append(1.0 / denom)
    p_c = jnp.stack(ps_c, axis=1).reshape(batch, rows, buf)
    p_n = jnp.stack(ps_n, axis=1).reshape(batch, rows, t)
    inv = jnp.stack(invs, axis=1).reshape(batch, rows, 1)
    o = jnp.einsum("bqk,bdk->bqd", p_c, vc.astype(BF16), preferred_element_type=F32)
    o = o + jnp.einsum("bqk,bdk->bqd", p_n, vn.astype(BF16), preferred_element_type=F32)
    o_ref[...] = (o * inv).reshape(nb, nkv, rows, hd).astype(o_ref.dtype)


def _attn_sample(q, kc, kn, vc, vn, sinks, i_even, states):
    n, nkv, rows, hd = q.shape
    n_even = kc.shape[0]
    buf = kc.shape[-1]
    t = kn.shape[-1]
    nb = SAMPLE_SEQ_TILE
    cache = pl.BlockSpec((None, nb, nkv, hd, buf), lambda b: (i_even, b, 0, 0, 0))
    new = pl.BlockSpec((nb, nkv, hd, t), lambda b: (b, 0, 0, 0))
    qspec = pl.BlockSpec((nb, nkv, rows, hd), lambda b: (b, 0, 0, 0))
    alias_in, alias_specs, aliases = _state_alias(states, 6, 1)
    state_shape = jax.ShapeDtypeStruct((n_even, n, nkv, hd, buf), F32)
    return pl.pallas_call(
        _attn_sample_body,
        grid=(n // nb,),
        in_specs=[pl.BlockSpec(memory_space=pltpu.SMEM), qspec, cache, new, cache, new] + alias_specs,
        out_specs=[qspec, cache, cache],
        out_shape=[jax.ShapeDtypeStruct((n, nkv, rows, hd), BF16), state_shape, state_shape],
        input_output_aliases=aliases,
        compiler_params=_params(1),
        name="attn_sample",
    )(sinks, q, kc, kn, vc, vn, *alias_in)


def _pool_ffn_sample_body(x_ref, st_ref, g_ref, w_ref, sc_ref, gf_ref, wg_ref, wu_ref, wd_ref, *refs, pos0):
    o_ref, new_ref = refs[-2:]
    t, nb, d = x_ref.shape
    hist = st_ref.shape[0]
    gdim = d // len(POOL_WINDOWS)
    x = x_ref[...].reshape(t * nb, d)
    xn = _rmsnorm(x, g_ref[...])
    ext = lambda j: st_ref[j] if j < hist else xn[(j - hist) * nb:(j - hist + 1) * nb]
    for j in range(hist):
        new_ref[j] = ext(t + j)
    ys = []
    for g, w in enumerate(POOL_WINDOWS):
        lanes = slice(g * gdim, (g + 1) * gdim)
        steps = []
        for step in range(t):
            acc = ext(hist + step)[:, lanes]
            for j in range(1, w):
                acc = acc + ext(hist + step - j)[:, lanes]
            cnt = float(min(w, pos0 + step + 1))
            steps.append(acc / cnt - xn[step * nb:(step + 1) * nb, lanes])
        dg = jnp.concatenate(steps, axis=0)
        ys.append(jnp.dot(dg.astype(BF16), w_ref[g], preferred_element_type=F32))
    x1 = x + jnp.concatenate(ys, axis=-1) * sc_ref[...]
    o_ref[...] = _ffn_apply(x1, gf_ref, wg_ref, wu_ref, wd_ref).reshape(t, nb, d)


def _pool_ffn_sample(x_tm, state_tm, norm_g, w_pool, pool_scale, layer, i_odd, pos0, norm_ffn, w_gate, w_up, w_down,
                     states):
    t, n, d = x_tm.shape
    n_odd, hist = state_tm.shape[:2]
    nb = SAMPLE_FFN_SEQ_TILE
    ng, gd, _ = w_pool.shape[1:]
    slab = pl.BlockSpec((None, hist, nb, d), lambda b: (i_odd, 0, b, 0))
    alias_in, alias_specs, aliases = _state_alias(states, 9, 1)
    return pl.pallas_call(
        functools.partial(_pool_ffn_sample_body, pos0=pos0),
        grid=(n // nb,),
        in_specs=[pl.BlockSpec((t, nb, d), lambda b: (0, b, 0)),
                  slab,
                  pl.BlockSpec((None, 1, d), lambda b: (layer, 0, 0)),
                  pl.BlockSpec((None, ng, gd, gd), lambda b: (i_odd, 0, 0, 0)),
                  pl.BlockSpec((None, 1, d), lambda b: (i_odd, 0, 0))] + _ffn_specs(w_gate, layer, 1) + alias_specs,
        out_specs=[pl.BlockSpec((t, nb, d), lambda b: (0, b, 0)), slab],
        out_shape=[jax.ShapeDtypeStruct((t, n, d), F32), jax.ShapeDtypeStruct((n_odd, hist, n, d), F32)],
        input_output_aliases=aliases,
        compiler_params=_params(1),
        name="pool_ffn_sample",
    )(x_tm, state_tm, norm_g, w_pool, pool_scale, norm_ffn, w_gate, w_up, w_down, *alias_in)


def _prepare_weights(norm_mix, norm_ffn, w_in, q_norm, k_norm, sinks, w_dw, b_dw, conv_norm_g, conv_norm_b, w_out,
                     w_pool, pool_scale, w_gate, w_up, w_down):
    vec = lambda a: a[:, None, :]
    return dict(
        norm_mix=vec(norm_mix), norm_ffn=vec(norm_ffn), w_in=w_in.astype(BF16),
        q_gain=vec(jnp.tile(q_norm, (1, N_HEADS))), k_gain=vec(jnp.tile(k_norm, (1, N_KV_HEADS))), sinks=sinks,
        w_dw=w_dw, b_dw=vec(b_dw),
        w_taps=jnp.broadcast_to(w_dw.astype(BF16)[:, :, None, :], w_dw.shape[:2] + (16, w_dw.shape[2])),
        ln_g=vec(conv_norm_g), ln_b=vec(conv_norm_b), w_out=w_out.astype(BF16),
        w_pool=w_pool.astype(BF16), pool_scale=vec(pool_scale),
        w_gate=w_gate.astype(BF16), w_up=w_up.astype(BF16), w_down=w_down.astype(BF16))


def _trunk_prompt(x, p, depth):
    n, t, d = x.shape
    xt = x.reshape(n * t, d)
    ffn = lambda layer: (p["norm_ffn"], p["w_gate"], p["w_up"], p["w_down"], layer)
    new_conv, new_k, new_v, new_pool = [], [], [], []
    for layer in range(depth):
        i = layer // 2
        if layer % 2 == 0:
            conv_out, q, k, v, glu_tail = _inproj_conv_prompt(
                xt.reshape(n, t, d), p["norm_mix"], p["w_in"], p["q_gain"], p["k_gain"], p["w_taps"], p["b_dw"],
                p["ln_g"], p["ln_b"], layer, i)
            attn = _attn_prompt(q, k, v, p["sinks"][i])
            xt = _outproj_ffn(xt, conv_out.reshape(n * t, -1), attn.reshape(n * t, -1), p["w_out"], i, *ffn(layer))
            keep = min(WINDOW, t)
            new_conv.append(glu_tail[:, CONV_HALO - (CONV_WIDTH - 1):])
            new_k.append(k[:, t - keep:].reshape(n, keep, N_KV_HEADS, HEAD_DIM))
            new_v.append(v[:, t - keep:].reshape(n, keep, N_KV_HEADS, HEAD_DIM))
        else:
            y, tail = _pool_ffn_prompt(xt.reshape(n, t, d), p["norm_mix"], p["w_pool"], p["pool_scale"], layer, i,
                                       *ffn(layer)[:-1])
            xt = y.reshape(n * t, d)
            new_pool.append(tail[:, POOL_HALO - POOL_BUF:])
    return xt.reshape(n, t, d), jnp.stack(new_conv), jnp.stack(new_k), jnp.stack(new_v), jnp.stack(new_pool)


def _trunk_sample(x, conv_bufs, k_bufs, v_bufs, pool_bufs, pos0, p, depth):
    n, t, d = x.shape
    x_tm = x.transpose(1, 0, 2)
    conv_tm = conv_bufs.transpose(0, 2, 1, 3)
    pool_tm = pool_bufs.transpose(0, 2, 1, 3)
    k_t = k_bufs.transpose(0, 1, 3, 4, 2)
    v_t = v_bufs.transpose(0, 1, 3, 4, 2)
    ffn = lambda layer: (p["norm_ffn"], p["w_gate"], p["w_up"], p["w_down"], layer)
    conv_new = kv_new = pool_new = None
    for layer in range(depth):
        i = layer // 2
        if layer % 2 == 0:
            xt = x_tm.reshape(t * n, d)
            glu, q, k, v = _inproj(xt, p["norm_mix"], p["w_in"], p["q_gain"], p["k_gain"], layer, i)
            conv_out, conv_new = _conv_sample(conv_tm, glu.reshape(t, n, D_CONV), p["w_dw"], p["b_dw"], p["ln_g"],
                                              p["ln_b"], i, None if conv_new is None else [conv_new])
            qh = q.reshape(t, n, N_KV_HEADS, GQA_GROUP, HEAD_DIM).transpose(1, 2, 3, 0, 4)
            qh = qh.reshape(n, N_KV_HEADS, GQA_GROUP * t, HEAD_DIM)
            kn = k.reshape(t, n, N_KV_HEADS, HEAD_DIM).transpose(1, 2, 3, 0)
            vn = v.reshape(t, n, N_KV_HEADS, HEAD_DIM).transpose(1, 2, 3, 0)
            oh, *kv_new = _attn_sample(qh, k_t, kn, v_t, vn, p["sinks"][i], i, kv_new)
            attn = oh.reshape(n, N_KV_HEADS, GQA_GROUP, t, HEAD_DIM).transpose(3, 0, 1, 2, 4).reshape(t * n, D_ATTN)
            xt = _outproj_ffn(xt, conv_out.reshape(t * n, D_CONV), attn, p["w_out"], i, *ffn(layer))
            x_tm = xt.reshape(t, n, d)
        else:
            x_tm, pool_new = _pool_ffn_sample(x_tm, pool_tm, p["norm_mix"], p["w_pool"], p["pool_scale"], layer, i, pos0,
                                              *ffn(layer)[:-1], None if pool_new is None else [pool_new])
    return (x_tm.transpose(1, 0, 2), conv_new.transpose(0, 2, 1, 3), kv_new[0].transpose(0, 1, 4, 2, 3),
            kv_new[1].transpose(0, 1, 4, 2, 3), pool_new.transpose(0, 2, 1, 3))


def kernel(x_prompt, x_sample, cache_conv, cache_k, cache_v, state_pool, norm_mix, norm_ffn, w_in, q_norm, k_norm,
           sinks, w_dw, b_dw, conv_norm_g, conv_norm_b, w_out, w_pool, pool_scale, w_gate, w_up, w_down):
    depth = norm_mix.shape[0]
    p = _prepare_weights(norm_mix, norm_ffn, w_in, q_norm, k_norm, sinks, w_dw, b_dw, conv_norm_g, conv_norm_b, w_out,
                         w_pool, pool_scale, w_gate, w_up, w_down)
    y_p, conv_p, k_p, v_p, pool_p = _trunk_prompt(x_prompt, p, depth)
    y_s, conv_s, k_s, v_s, pool_s = _trunk_sample(x_sample, cache_conv, cache_k, cache_v, state_pool, PAST_LEN, p, depth)
    return (y_p, y_s, conv_p, k_p, v_p, pool_p, conv_s, k_s, v_s, pool_s)
```

```python
import functools

import jax
import jax.numpy as jnp
from jax import lax
from jax.experimental import pallas as pl
from jax.experimental.pallas import tpu as pltpu

F32 = jnp.float32
BF16 = jnp.bfloat16

HEAD_DIM = 64
N_HEADS = 8
N_KV_HEADS = 2
GQA_GROUP = N_HEADS // N_KV_HEADS
D_ATTN = N_HEADS * HEAD_DIM
D_KV = N_KV_HEADS * HEAD_DIM
WINDOW = 128
PAST_LEN = 8192
D_CONV = 512
CONV_WIDTH = 31
CONV_HALO = 32
POOL_WINDOWS = (2, 4, 8, 16)
POOL_BUF = max(POOL_WINDOWS) - 1
POOL_HALO = 16
RMS_EPS = 1e-6
LN_EPS = 1e-5
NEG_INF = -1e30
LANES = 128
ALIBI_SLOPES = tuple(2.0 ** (-8.0 * (h + 1) / N_HEADS) for h in range(N_HEADS))

TOKEN_TILE = 512
CONV_PHASES = 4
ATTN_TIME_TILE = 1024
POOL_TIME_TILE = 512
POOL_PHASES = 4
FFN_CHUNK = 1536
SAMPLE_SEQ_TILE = 32
SAMPLE_FFN_SEQ_TILE = 64
VMEM_LIMIT = 56 * 1024 * 1024


def _params(n_axes):
    return pltpu.CompilerParams(dimension_semantics=("arbitrary",) * n_axes, vmem_limit_bytes=VMEM_LIMIT)


def _rmsnorm(x, g):
    return x * lax.rsqrt(jnp.mean(x * x, axis=-1, keepdims=True) + RMS_EPS) * g


def _head_rmsnorm(x, g):
    lane = lax.broadcasted_iota(jnp.int32, (x.shape[0], LANES), 1)
    low = lane < HEAD_DIM
    outs = []
    for s in range(x.shape[1] // LANES):
        xs = x[:, s * LANES:(s + 1) * LANES]
        sq = xs * xs
        s_low = jnp.sum(jnp.where(low, sq, 0.0), axis=-1, keepdims=True)
        s_high = jnp.sum(jnp.where(low, 0.0, sq), axis=-1, keepdims=True)
        ms = jnp.where(low, s_low, s_high) * (1.0 / HEAD_DIM)
        outs.append(xs * lax.rsqrt(ms + RMS_EPS) * g[:, s * LANES:(s + 1) * LANES])
    return outs[0] if len(outs) == 1 else jnp.concatenate(outs, axis=-1)


def _layernorm_silu(y, g, b):
    mu = jnp.mean(y, axis=-1, keepdims=True)
    yc = y - mu
    var = jnp.mean(yc * yc, axis=-1, keepdims=True)
    z = yc * lax.rsqrt(var + LN_EPS) * g + b
    return z * jax.nn.sigmoid(z)


def _ffn_apply(x, g_ref, wg_ref, wu_ref, wd_ref):
    h = _rmsnorm(x, g_ref[...]).astype(BF16)
    acc = x
    d_ff = wg_ref.shape[1]
    for c0 in range(0, d_ff, FFN_CHUNK):
        c1 = min(c0 + FFN_CHUNK, d_ff)
        gate = jnp.dot(h, wg_ref[:, c0:c1], preferred_element_type=F32)
        up = jnp.dot(h, wu_ref[:, c0:c1], preferred_element_type=F32)
        a = (gate * jax.nn.sigmoid(gate) * up).astype(BF16)
        acc = acc + jnp.dot(a, wd_ref[c0:c1, :], preferred_element_type=F32)
    return acc


def _ffn_specs(w_gate, layer, n_axes):
    _, d, f = w_gate.shape
    at_layer = {1: lambda i: (layer, 0, 0), 2: lambda b, i: (layer, 0, 0)}[n_axes]
    resident = functools.partial(pl.BlockSpec, pipeline_mode=pl.Buffered(1))
    return [pl.BlockSpec((None, 1, d), at_layer), resident((None, d, f), at_layer), resident((None, d, f), at_layer),
            resident((None, f, d), at_layer)]


def _inproj_body(x_ref, g_ref, w_ref, qn_ref, kn_ref, glu_ref, q_ref, k_ref, v_ref):
    h = _rmsnorm(x_ref[...], g_ref[...]).astype(BF16)
    o_q = 2 * D_CONV
    ua = jnp.dot(h, w_ref[:, o_q:], preferred_element_type=F32)
    uc = jnp.dot(h, w_ref[:, :o_q], preferred_element_type=F32)
    q_ref[...] = (_head_rmsnorm(ua[:, :D_ATTN], qn_ref[...]) * (HEAD_DIM ** -0.5)).astype(BF16)
    k_ref[...] = _head_rmsnorm(ua[:, D_ATTN:D_ATTN + D_KV], kn_ref[...])
    v_ref[...] = ua[:, D_ATTN + D_KV:]
    glu_ref[...] = uc[:, :D_CONV] * jax.nn.sigmoid(uc[:, D_CONV:])


def _inproj(x, norm_g, w_in, q_gain, k_gain, layer, i_even):
    t, d = x.shape
    n = w_in.shape[2]
    row = lambda width: pl.BlockSpec((TOKEN_TILE, width), lambda i: (i, 0))
    return pl.pallas_call(
        _inproj_body,
        grid=(t // TOKEN_TILE,),
        in_specs=[row(d),
                  pl.BlockSpec((None, 1, d), lambda i: (layer, 0, 0)),
                  pl.BlockSpec((None, d, n), lambda i: (i_even, 0, 0), pipeline_mode=pl.Buffered(1)),
                  pl.BlockSpec((None, 1, D_ATTN), lambda i: (i_even, 0, 0)),
                  pl.BlockSpec((None, 1, D_KV), lambda i: (i_even, 0, 0))],
        out_specs=[row(D_CONV), row(D_ATTN), row(D_KV), row(D_KV)],
        out_shape=[jax.ShapeDtypeStruct((t, D_CONV), F32), jax.ShapeDtypeStruct((t, D_ATTN), BF16),
                   jax.ShapeDtypeStruct((t, D_KV), F32), jax.ShapeDtypeStruct((t, D_KV), F32)],
        compiler_params=_params(1),
        name="inproj",
    )(x, norm_g, w_in, q_gain, k_gain)


def _outproj_ffn_body(x_ref, c_ref, a_ref, w_ref, gf_ref, wg_ref, wu_ref, wd_ref, o_ref):
    y = jnp.dot(c_ref[...], w_ref[:D_CONV, :], preferred_element_type=F32)
    y = y + jnp.dot(a_ref[...], w_ref[D_CONV:, :], preferred_element_type=F32)
    o_ref[...] = _ffn_apply(x_ref[...] + y, gf_ref, wg_ref, wu_ref, wd_ref)


def _outproj_ffn(x, conv_out, attn, w_out, i_even, norm_ffn, w_gate, w_up, w_down, layer):
    t, d = x.shape
    row = lambda width: pl.BlockSpec((TOKEN_TILE, width), lambda i: (i, 0))
    return pl.pallas_call(
        _outproj_ffn_body,
        grid=(t // TOKEN_TILE,),
        in_specs=[row(d), row(D_CONV), row(D_ATTN),
                  pl.BlockSpec((None, D_CONV + D_ATTN, d), lambda i: (i_even, 0, 0), pipeline_mode=pl.Buffered(1))]
        + _ffn_specs(w_gate, layer, 1),
        out_specs=row(d),
        out_shape=jax.ShapeDtypeStruct((t, d), F32),
        compiler_params=_params(1),
        name="outproj_ffn",
    )(x, conv_out, attn, w_out, norm_ffn, w_gate, w_up, w_down)


def _depthwise_conv_halves(ext_ref, y_ref, w_ref, b_ref, slab, half_rows):
    lead = CONV_HALO - (CONV_WIDTH - 1)
    rows = 8 * CONV_PHASES
    lanes = slice(slab * LANES, (slab + 1) * LANES)

    def chunk(c, carry):
        r0 = c * rows
        acc = [jnp.zeros((16, LANES), F32) for _ in range(CONV_PHASES)]
        for e in range(CONV_WIDTH + CONV_PHASES - 1):
            v = jnp.concatenate([ext_ref[slab, pl.ds(r0 + lead + e, 8, stride=CONV_PHASES), :],
                                 ext_ref[slab, pl.ds(half_rows + r0 + lead + e, 8, stride=CONV_PHASES), :]],
                                axis=0).astype(BF16)
            for ph in range(CONV_PHASES):
                k = e - ph
                if 0 <= k < CONV_WIDTH:
                    acc[ph] = acc[ph] + v.astype(F32) * w_ref[k, :, lanes].astype(F32)
        bias = jnp.broadcast_to(b_ref[0:1, lanes], (16, LANES))
        for ph in range(CONV_PHASES):
            out = acc[ph] + bias
            y_ref[slab, pl.ds(r0 + ph, 8, stride=CONV_PHASES), :] = out[:8]
            y_ref[slab, pl.ds(half_rows + r0 + ph, 8, stride=CONV_PHASES), :] = out[8:]
        return carry

    lax.fori_loop(0, half_rows // rows, chunk, 0)


def _layernorm_silu_slabs(y_ref, g_ref, beta_ref, o_ref):
    n_slabs = y_ref.shape[0]
    ys = [y_ref[s] for s in range(n_slabs)]
    inv_c = 1.0 / (n_slabs * LANES)
    mu = jnp.sum(sum(ys), axis=-1, keepdims=True) * inv_c
    ycs = [y - mu for y in ys]
    var = jnp.sum(sum(yc * yc for yc in ycs), axis=-1, keepdims=True) * inv_c
    rstd = lax.rsqrt(var + LN_EPS)
    for s, yc in enumerate(ycs):
        lanes = slice(s * LANES, (s + 1) * LANES)
        z = yc * rstd * g_ref[:, lanes] + beta_ref[:, lanes]
        o_ref[:, lanes] = (z * jax.nn.sigmoid(z)).astype(o_ref.dtype)


def _banded_attention(sink_ref, q_ref, kcat, vcat, o_ref, bias_ref, sinkcol_ref, s_ref):
    b = pl.program_id(0)
    i = pl.program_id(1)
    w = WINDOW
    pair = 2 * HEAD_DIM
    n_blocks = q_ref.shape[0] // w
    lane = lax.broadcasted_iota(jnp.int32, (1, pair), 1)
    low = lane < HEAD_DIM

    @pl.when((b == 0) & (i == 0))
    def _():
        row = lax.broadcasted_iota(jnp.int32, (w, 2 * w), 0)
        col = lax.broadcasted_iota(jnp.int32, (w, 2 * w), 1)
        dist = w + row - col
        valid = (dist >= 0) & (dist < w)
        distf = dist.astype(F32)
        for h in range(N_HEADS):
            slab, half = h // 2, h % 2
            bias_ref[slab * w:(slab + 1) * w, half * 2 * w:(half + 1) * 2 * w] = (
                jnp.where(valid, -ALIBI_SLOPES[h] * distf, NEG_INF))
            sinkcol_ref[half, slab * w:(slab + 1) * w, :] = jnp.full((w, 1), sink_ref[h], F32)

    krot = pltpu.roll(kcat, HEAD_DIM, 1)
    vrot = pltpu.roll(vcat, HEAD_DIM, 1)
    zero = jnp.zeros((), F32)
    k_low = [jnp.where(low, kcat, zero).astype(BF16), jnp.where(low, krot, zero).astype(BF16)]
    k_high = [jnp.where(low, zero, krot).astype(BF16), jnp.where(low, zero, kcat).astype(BF16)]
    v_low = [jnp.where(low, vcat, zero).astype(BF16), jnp.where(low, vrot, zero).astype(BF16)]
    v_high = [jnp.where(low, zero, vrot).astype(BF16), jnp.where(low, zero, vcat).astype(BF16)]
    ones_low = jnp.broadcast_to(jnp.where(low, 1.0, 0.0).astype(BF16), (2 * w, pair))
    ones_high = jnp.broadcast_to(jnp.where(low, 0.0, 1.0).astype(BF16), (2 * w, pair))
    ones_ext = jnp.concatenate([ones_low, ones_high], axis=0)
    col = lax.broadcasted_iota(jnp.int32, (1, 4 * w), 1)
    no_prev = (col % (2 * w) < w) & (i == 0)

    for jb in range(n_blocks):
        rows = slice(jb * w, (jb + 1) * w)
        keys = slice(jb * w, (jb + 2) * w)
        for g in range(N_KV_HEADS):
            q2 = jnp.concatenate([q_ref[rows, (2 * g) * pair:(2 * g + 1) * pair],
                                  q_ref[rows, (2 * g + 1) * pair:(2 * g + 2) * pair]], axis=0)
            kk = jnp.concatenate([k_low[g][keys], k_high[g][keys]], axis=0)
            s_ref[jb, 2 * g * w:(2 * g + 2) * w, :] = lax.dot_general(
                q2, kk, (((1,), (1,)), ((), ())), preferred_element_type=F32)

    for jb in range(n_blocks):
        rows = slice(jb * w, (jb + 1) * w)
        keys = slice(jb * w, (jb + 2) * w)
        bias = bias_ref[...]
        if jb == 0:
            bias = jnp.where(no_prev, NEG_INF, bias)
        s = s_ref[jb] + bias
        ps, sink_terms = [], []
        for half in range(2):
            sh = s[:, half * 2 * w:(half + 1) * 2 * w]
            sink = sinkcol_ref[half]
            m = jnp.maximum(jnp.max(sh, axis=-1, keepdims=True), sink)
            ps.append(jnp.exp(sh - m).astype(BF16))
            sink_terms.append(jnp.exp(sink - m))
        p = jnp.concatenate(ps, axis=1)
        sink_term = jnp.where(low, sink_terms[0], sink_terms[1])
        for g in range(N_KV_HEADS):
            vv = jnp.concatenate([jnp.concatenate([v_low[g][keys], v_high[g][keys]], axis=0), ones_ext], axis=1)
            o = jnp.dot(p[2 * g * w:(2 * g + 2) * w], vv, preferred_element_type=F32)
            o = o[:, :pair] / (o[:, pair:] + sink_term[2 * g * w:(2 * g + 2) * w])
            o_ref[rows, (2 * g) * pair:(2 * g + 1) * pair] = o[:w].astype(o_ref.dtype)
            o_ref[rows, (2 * g + 1) * pair:(2 * g + 2) * pair] = o[w:].astype(o_ref.dtype)


def _attn_prompt_body(sink_ref, q_ref, kc_ref, kp_ref, vc_ref, vp_ref, o_ref, bias_ref, sinkcol_ref, s_ref):
    kcat = jnp.concatenate([kp_ref[...], kc_ref[...]], axis=0)
    vcat = jnp.concatenate([vp_ref[...], vc_ref[...]], axis=0)
    _banded_attention(sink_ref, q_ref, kcat, vcat, o_ref, bias_ref, sinkcol_ref, s_ref)


def _attn_prompt(q, k, v, sinks):
    n, t, _ = q.shape
    tt = ATTN_TIME_TILE
    cur = lambda width: pl.BlockSpec((None, tt, width), lambda b, i: (b, i, 0))
    prev = lambda width: pl.BlockSpec((None, WINDOW, width),
                                      lambda b, i: (b, jnp.maximum(i * (tt // WINDOW) - 1, 0), 0))
    return pl.pallas_call(
        _attn_prompt_body,
        grid=(n, t // tt),
        in_specs=[pl.BlockSpec(memory_space=pltpu.SMEM), cur(D_ATTN), cur(D_KV), prev(D_KV), cur(D_KV), prev(D_KV)],
        out_specs=cur(D_ATTN),
        out_shape=jax.ShapeDtypeStruct((n, t, D_ATTN), BF16),
        scratch_shapes=[pltpu.VMEM((N_HEADS // 2 * WINDOW, 4 * WINDOW), F32),
                        pltpu.VMEM((2, N_HEADS // 2 * WINDOW, 1), F32),
                        pltpu.VMEM((tt // WINDOW, N_HEADS // 2 * WINDOW, 4 * WINDOW), F32)],
        compiler_params=_params(2),
        name="attn_prompt",
    )(sinks, q, k, k, v, v)


def _inproj_conv_prompt_body(x_ref, g_ref, w_ref, qn_ref, kn_ref, wt_ref, b_ref, lg_ref, lb_ref,
                             conv_ref, q_ref, k_ref, v_ref, glu_tail_ref, ext_ref, y_ref):
    tt = x_ref.shape[0]
    n_slabs = D_CONV // LANES
    i = pl.program_id(1)

    @pl.when(i == 0)
    def _():
        ext_ref[:, 0:CONV_HALO, :] = jnp.zeros((n_slabs, CONV_HALO, LANES), F32)

    @pl.when(i > 0)
    def _():
        ext_ref[:, 0:CONV_HALO, :] = ext_ref[:, tt:tt + CONV_HALO, :]

    h = _rmsnorm(x_ref[...], g_ref[...]).astype(BF16)
    o_q = 2 * D_CONV
    ua = jnp.dot(h, w_ref[:, o_q:], preferred_element_type=F32)
    uc = jnp.dot(h, w_ref[:, :o_q], preferred_element_type=F32)
    q_ref[...] = (_head_rmsnorm(ua[:, :D_ATTN], qn_ref[...]) * (HEAD_DIM ** -0.5)).astype(BF16)
    k_ref[...] = _head_rmsnorm(ua[:, D_ATTN:D_ATTN + D_KV], kn_ref[...])
    v_ref[...] = ua[:, D_ATTN + D_KV:]
    glu = uc[:, :D_CONV] * jax.nn.sigmoid(uc[:, D_CONV:])
    glu_tail_ref[...] = glu[tt - CONV_HALO:, :]
    for s in range(n_slabs):
        ext_ref[s, CONV_HALO:, :] = glu[:, s * LANES:(s + 1) * LANES]
    for s in range(n_slabs):
        _depthwise_conv_halves(ext_ref, y_ref, wt_ref, b_ref, s, tt // 2)
    _layernorm_silu_slabs(y_ref, lg_ref, lb_ref, conv_ref)


def _inproj_conv_prompt(x, norm_g, w_in, q_gain, k_gain, w_taps, b_dw, ln_g, ln_b, layer, i_even):
    n, t, d = x.shape
    n_in = w_in.shape[2]
    tt = ATTN_TIME_TILE
    tile = lambda width: pl.BlockSpec((None, tt, width), lambda b, i: (b, i, 0))
    even = lambda width: pl.BlockSpec((None, 1, width), lambda b, i: (i_even, 0, 0))
    return pl.pallas_call(
        _inproj_conv_prompt_body,
        grid=(n, t // tt),
        in_specs=[tile(d),
                  pl.BlockSpec((None, 1, d), lambda b, i: (layer, 0, 0)),
                  pl.BlockSpec((None, d, n_in), lambda b, i: (i_even, 0, 0), pipeline_mode=pl.Buffered(1)),
                  even(D_ATTN), even(D_KV),
                  pl.BlockSpec((None, CONV_WIDTH, 16, D_CONV), lambda b, i: (i_even, 0, 0, 0)),
                  even(D_CONV), even(D_CONV), even(D_CONV)],
        out_specs=[tile(D_CONV), tile(D_ATTN), tile(D_KV), tile(D_KV),
                   pl.BlockSpec((None, CONV_HALO, D_CONV), lambda b, i: (b, 0, 0))],
        out_shape=[jax.ShapeDtypeStruct((n, t, D_CONV), BF16), jax.ShapeDtypeStruct((n, t, D_ATTN), BF16),
                   jax.ShapeDtypeStruct((n, t, D_KV), F32), jax.ShapeDtypeStruct((n, t, D_KV), F32),
                   jax.ShapeDtypeStruct((n, CONV_HALO, D_CONV), F32)],
        scratch_shapes=[pltpu.VMEM((D_CONV // LANES, CONV_HALO + tt, LANES), F32),
                        pltpu.VMEM((D_CONV // LANES, tt, LANES), F32)],
        compiler_params=_params(2),
        name="inproj_conv_prompt",
    )(x, norm_g, w_in, q_gain, k_gain, w_taps, b_dw, ln_g, ln_b)


def _window_sums(vals, w, n_out):
    shared = list(range(n_out - 1, w))
    base = None
    for e in shared:
        base = vals[e] if base is None else base + vals[e]
    sums = []
    for i in range(n_out):
        acc = base
        for e in range(i, i + w):
            if e not in shared:
                acc = vals[e] if acc is None else acc + vals[e]
        sums.append(acc)
    return sums


def _pool_deltas_strided(ext_ref, d_ref, pos0, n_rows):
    n_slabs = ext_ref.shape[0]
    slabs_per_group = n_slabs // len(POOL_WINDOWS)
    rows = 8 * POOL_PHASES

    def chunk(c, carry):
        r0 = c * rows
        sub = lax.broadcasted_iota(jnp.int32, (8, LANES), 0)
        pos = [pos0 + r0 + ph + POOL_PHASES * sub for ph in range(POOL_PHASES)]
        inv = {w: [1.0 / jnp.minimum(w, p + 1).astype(F32) for p in pos] for w in POOL_WINDOWS}
        for slab in range(n_slabs):
            w = POOL_WINDOWS[slab // slabs_per_group]
            first = POOL_HALO - (w - 1)
            vals = [ext_ref[slab, pl.ds(r0 + first + e, 8, stride=POOL_PHASES), :] for e in range(w + POOL_PHASES - 1)]
            sums = _window_sums(vals, w, POOL_PHASES)
            for ph in range(POOL_PHASES):
                d_ref[slab, pl.ds(r0 + ph, 8, stride=POOL_PHASES), :] = sums[ph] * inv[w][ph] - vals[ph + w - 1]
        return carry

    lax.fori_loop(0, n_rows // rows, chunk, 0)


def _pool_ffn_prompt_body(x_ref, g_ref, w_ref, sc_ref, gf_ref, wg_ref, wu_ref, wd_ref, o_ref, tail_ref, ext_ref, d_ref):
    tt, d = x_ref.shape
    n_slabs = d // LANES
    slabs_per_group = n_slabs // len(POOL_WINDOWS)
    i = pl.program_id(1)

    @pl.when(i == 0)
    def _():
        ext_ref[:, 0:POOL_HALO, :] = jnp.zeros((n_slabs, POOL_HALO, LANES), F32)

    @pl.when(i > 0)
    def _():
        ext_ref[:, 0:POOL_HALO, :] = ext_ref[:, tt:tt + POOL_HALO, :]

    x = x_ref[...]
    xn = _rmsnorm(x, g_ref[...])
    for s in range(n_slabs):
        ext_ref[s, POOL_HALO:, :] = xn[:, s * LANES:(s + 1) * LANES]
    tail_ref[...] = xn[tt - POOL_HALO:, :]
    _pool_deltas_strided(ext_ref, d_ref, i * tt, tt)
    ys = []
    for g in range(len(POOL_WINDOWS)):
        dg = jnp.concatenate([d_ref[s] for s in range(g * slabs_per_group, (g + 1) * slabs_per_group)], axis=-1)
        ys.append(jnp.dot(dg.astype(BF16), w_ref[g], preferred_element_type=F32))
    x1 = x + jnp.concatenate(ys, axis=-1) * sc_ref[...]
    o_ref[...] = _ffn_apply(x1, gf_ref, wg_ref, wu_ref, wd_ref)


def _pool_ffn_prompt(x, norm_g, w_pool, pool_scale, layer, i_odd, norm_ffn, w_gate, w_up, w_down):
    n, t, d = x.shape
    tt = POOL_TIME_TILE
    ng, gd, _ = w_pool.shape[1:]
    tile = pl.BlockSpec((None, tt, d), lambda b, i: (b, i, 0))
    return pl.pallas_call(
        _pool_ffn_prompt_body,
        grid=(n, t // tt),
        in_specs=[tile,
                  pl.BlockSpec((None, 1, d), lambda b, i: (layer, 0, 0)),
                  pl.BlockSpec((None, ng, gd, gd), lambda b, i: (i_odd, 0, 0, 0)),
                  pl.BlockSpec((None, 1, d), lambda b, i: (i_odd, 0, 0))] + _ffn_specs(w_gate, layer, 2),
        out_specs=[tile, pl.BlockSpec((None, POOL_HALO, d), lambda b, i: (b, 0, 0))],
        out_shape=[jax.ShapeDtypeStruct((n, t, d), F32), jax.ShapeDtypeStruct((n, POOL_HALO, d), F32)],
        scratch_shapes=[pltpu.VMEM((d // LANES, POOL_HALO + tt, LANES), F32), pltpu.VMEM((d // LANES, tt, LANES), F32)],
        compiler_params=_params(2),
        name="pool_ffn_prompt",
    )(x, norm_g, w_pool, pool_scale, norm_ffn, w_gate, w_up, w_down)


def _state_alias(states, first_input, first_output):
    if states is None:
        return [], [], {}
    aliases = {first_input + j: first_output + j for j in range(len(states))}
    return list(states), [pl.BlockSpec(memory_space=pl.ANY)] * len(states), aliases


def _conv_sample_body(cache_ref, glu_ref, w_ref, b_ref, g_ref, beta_ref, *refs):
    o_ref, new_ref = refs[-2:]
    hist = cache_ref.shape[0]
    t = glu_ref.shape[0]
    ext = lambda j: cache_ref[j] if j < hist else glu_ref[j - hist]
    rounded = lambda a: a.astype(BF16).astype(F32)
    for j in range(hist):
        new_ref[j] = ext(t + j)
    for step in range(t):
        acc = rounded(w_ref[0:1, :]) * rounded(ext(step)) + b_ref[...]
        for k in range(1, CONV_WIDTH):
            acc = acc + rounded(w_ref[k:k + 1, :]) * rounded(ext(step + k))
        o_ref[step] = _layernorm_silu(acc, g_ref[...], beta_ref[...]).astype(o_ref.dtype)


def _conv_sample(cache_tm, glu_tm, w_dw, b_dw, ln_g, ln_b, i_even, states):
    n_even, hist, n, c = cache_tm.shape
    t = glu_tm.shape[0]
    nb = SAMPLE_SEQ_TILE
    vec = pl.BlockSpec((None, 1, c), lambda b: (i_even, 0, 0))
    slab = pl.BlockSpec((None, hist, nb, c), lambda b: (i_even, 0, b, 0))
    alias_in, alias_specs, aliases = _state_alias(states, 6, 1)
    return pl.pallas_call(
        _conv_sample_body,
        grid=(n // nb,),
        in_specs=[slab,
                  pl.BlockSpec((t, nb, c), lambda b: (0, b, 0)),
                  pl.BlockSpec((None, CONV_WIDTH, c), lambda b: (i_even, 0, 0)),
                  vec, vec, vec] + alias_specs,
        out_specs=[pl.BlockSpec((t, nb, c), lambda b: (0, b, 0)), slab],
        out_shape=[jax.ShapeDtypeStruct((t, n, c), BF16), jax.ShapeDtypeStruct((n_even, hist, n, c), F32)],
        input_output_aliases=aliases,
        compiler_params=_params(1),
        name="conv_sample",
    )(cache_tm, glu_tm, w_dw, b_dw, ln_g, ln_b, *alias_in)


def _attn_sample_body(sink_ref, q_ref, kc_ref, kn_ref, vc_ref, vn_ref, *refs):
    o_ref, knew_ref, vnew_ref = refs[-3:]
    nb, nkv, rows, hd = q_ref.shape
    buf = kc_ref.shape[-1]
    t = kn_ref.shape[-1]
    batch = nb * nkv
    q = q_ref[...].reshape(batch, rows, hd)
    kc = kc_ref[...].reshape(batch, hd, buf)
    kn = kn_ref[...].reshape(batch, hd, t)
    vc = vc_ref[...].reshape(batch, hd, buf)
    vn = vn_ref[...].reshape(batch, hd, t)

    lane = lax.broadcasted_iota(jnp.int32, (1, 1, buf), 2)
    pad = jnp.zeros((batch, hd, buf - t), F32)
    k_new = jnp.where(lane < buf - t, pltpu.roll(kc, buf - t, 2), jnp.concatenate([pad, kn], axis=-1))
    v_new = jnp.where(lane < buf - t, pltpu.roll(vc, buf - t, 2), jnp.concatenate([pad, vn], axis=-1))
    knew_ref[...] = k_new.reshape(nb, nkv, hd, buf)
    vnew_ref[...] = v_new.reshape(nb, nkv, hd, buf)

    s_c = jnp.einsum("bqd,bdk->bqk", q, kc.astype(BF16), preferred_element_type=F32)
    s_n = jnp.einsum("bqd,bdk->bqk", q, kn.astype(BF16), preferred_element_type=F32)
    step = lax.broadcasted_iota(jnp.int32, (rows, 1), 0) % t
    head_in_group = lax.broadcasted_iota(jnp.int32, (rows, 1), 0) // t
    dist_c = buf + step - lax.broadcasted_iota(jnp.int32, (rows, buf), 1)
    dist_n = step - lax.broadcasted_iota(jnp.int32, (rows, t), 1)
    s_c = s_c.reshape(nb, nkv, rows, buf)
    s_n = s_n.reshape(nb, nkv, rows, t)
    ps_c, ps_n, invs = [], [], []
    for kv in range(nkv):
        slope = jnp.zeros((rows, 1), F32)
        sink = jnp.zeros((rows, 1), F32)
        for j in range(GQA_GROUP):
            slope = jnp.where(head_in_group == j, ALIBI_SLOPES[kv * GQA_GROUP + j], slope)
            sink = jnp.where(head_in_group == j, sink_ref[kv * GQA_GROUP + j], sink)
        sc = jnp.where((dist_c >= 0) & (dist_c < WINDOW), s_c[:, kv] - slope * dist_c.astype(F32), NEG_INF)
        sn = jnp.where((dist_n >= 0) & (dist_n < WINDOW), s_n[:, kv] - slope * dist_n.astype(F32), NEG_INF)
        m = jnp.maximum(jnp.maximum(jnp.max(sc, axis=-1, keepdims=True), jnp.max(sn, axis=-1, keepdims=True)), sink)
        pc = jnp.exp(sc - m)
        pn = jnp.exp(sn - m)
        denom = jnp.sum(pc, axis=-1, keepdims=True) + jnp.sum(pn, axis=-1, keepdims=True) + jnp.exp(sink - m)
        ps_c.append(pc.astype(BF16))
        ps_n.append(pn.astype(BF16))
        invs.append(1.0 / denom)
    p_c = jnp.stack(ps_c, axis=1).reshape(batch, rows, buf)
    p_n = jnp.stack(ps_n, axis=1).reshape(batch, rows, t)
    inv = jnp.stack(invs, axis=1).reshape(batch, rows, 1)
    o = jnp.einsum("bqk,bdk->bqd", p_c, vc.astype(BF16), preferred_element_type=F32)
    o = o + jnp.einsum("bqk,bdk->bqd", p_n, vn.astype(BF16), preferred_element_type=F32)
    o_ref[...] = (o * inv).reshape(nb, nkv, rows, hd).astype(o_ref.dtype)


def _attn_sample(q, kc, kn, vc, vn, sinks, i_even, states):
    n, nkv, rows, hd = q.shape
    n_even = kc.shape[0]
    buf = kc.shape[-1]
    t = kn.shape[-1]
    nb = SAMPLE_SEQ_TILE
    cache = pl.BlockSpec((None, nb, nkv, hd, buf), lambda b: (i_even, b, 0, 0, 0))
    new = pl.BlockSpec((nb, nkv, hd, t), lambda b: (b, 0, 0, 0))
    qspec = pl.BlockSpec((nb, nkv, rows, hd), lambda b: (b, 0, 0, 0))
    alias_in, alias_specs, aliases = _state_alias(states, 6, 1)
    state_shape = jax.ShapeDtypeStruct((n_even, n, nkv, hd, buf), F32)
    return pl.pallas_call(
        _attn_sample_body,
        grid=(n // nb,),
        in_specs=[pl.BlockSpec(memory_space=pltpu.SMEM), qspec, cache, new, cache, new] + alias_specs,
        out_specs=[qspec, cache, cache],
        out_shape=[jax.ShapeDtypeStruct((n, nkv, rows, hd), BF16), state_shape, state_shape],
        input_output_aliases=aliases,
        compiler_params=_params(1),
        name="attn_sample",
    )(sinks, q, kc, kn, vc, vn, *alias_in)


def _pool_ffn_sample_body(x_ref, st_ref, g_ref, w_ref, sc_ref, gf_ref, wg_ref, wu_ref, wd_ref, *refs, pos0):
    o_ref, new_ref = refs[-2:]
    t, nb, d = x_ref.shape
    hist = st_ref.shape[0]
    gdim = d // len(POOL_WINDOWS)
    x = x_ref[...].reshape(t * nb, d)
    xn = _rmsnorm(x, g_ref[...])
    ext = lambda j: st_ref[j] if j < hist else xn[(j - hist) * nb:(j - hist + 1) * nb]
    for j in range(hist):
        new_ref[j] = ext(t + j)
    ys = []
    for g, w in enumerate(POOL_WINDOWS):
        lanes = slice(g * gdim, (g + 1) * gdim)
        steps = []
        for step in range(t):
            acc = ext(hist + step)[:, lanes]
            for j in range(1, w):
                acc = acc + ext(hist + step - j)[:, lanes]
            cnt = float(min(w, pos0 + step + 1))
            steps.append(acc / cnt - xn[step * nb:(step + 1) * nb, lanes])
        dg = jnp.concatenate(steps, axis=0)
        ys.append(jnp.dot(dg.astype(BF16), w_ref[g], preferred_element_type=F32))
    x1 = x + jnp.concatenate(ys, axis=-1) * sc_ref[...]
    o_ref[...] = _ffn_apply(x1, gf_ref, wg_ref, wu_ref, wd_ref).reshape(t, nb, d)


def _pool_ffn_sample(x_tm, state_tm, norm_g, w_pool, pool_scale, layer, i_odd, pos0, norm_ffn, w_gate, w_up, w_down,
                     states):
    t, n, d = x_tm.shape
    n_odd, hist = state_tm.shape[:2]
    nb = SAMPLE_FFN_SEQ_TILE
    ng, gd, _ = w_pool.shape[1:]
    slab = pl.BlockSpec((None, hist, nb, d), lambda b: (i_odd, 0, b, 0))
    alias_in, alias_specs, aliases = _state_alias(states, 9, 1)
    return pl.pallas_call(
        functools.partial(_pool_ffn_sample_body, pos0=pos0),
        grid=(n // nb,),
        in_specs=[pl.BlockSpec((t, nb, d), lambda b: (0, b, 0)),
                  slab,
                  pl.BlockSpec((None, 1, d), lambda b: (layer, 0, 0)),
                  pl.BlockSpec((None, ng, gd, gd), lambda b: (i_odd, 0, 0, 0)),
                  pl.BlockSpec((None, 1, d), lambda b: (i_odd, 0, 0))] + _ffn_specs(w_gate, layer, 1) + alias_specs,
        out_specs=[pl.BlockSpec((t, nb, d), lambda b: (0, b, 0)), slab],
        out_shape=[jax.ShapeDtypeStruct((t, n, d), F32), jax.ShapeDtypeStruct((n_odd, hist, n, d), F32)],
        input_output_aliases=aliases,
        compiler_params=_params(1),
        name="pool_ffn_sample",
    )(x_tm, state_tm, norm_g, w_pool, pool_scale, norm_ffn, w_gate, w_up, w_down, *alias_in)


def _prepare_weights(norm_mix, norm_ffn, w_in, q_norm, k_norm, sinks, w_dw, b_dw, conv_norm_g, conv_norm_b, w_out,
                     w_pool, pool_scale, w_gate, w_up, w_down):
    vec = lambda a: a[:, None, :]
    return dict(
        norm_mix=vec(norm_mix), norm_ffn=vec(norm_ffn), w_in=w_in.astype(BF16),
        q_gain=vec(jnp.tile(q_norm, (1, N_HEADS))), k_gain=vec(jnp.tile(k_norm, (1, N_KV_HEADS))), sinks=sinks,
        w_dw=w_dw, b_dw=vec(b_dw),
        w_taps=jnp.broadcast_to(w_dw.astype(BF16)[:, :, None, :], w_dw.shape[:2] + (16, w_dw.shape[2])),
        ln_g=vec(conv_norm_g), ln_b=vec(conv_norm_b), w_out=w_out.astype(BF16),
        w_pool=w_pool.astype(BF16), pool_scale=vec(pool_scale),
        w_gate=w_gate.astype(BF16), w_up=w_up.astype(BF16), w_down=w_down.astype(BF16))


def _trunk_prompt(x, p, depth):
    n, t, d = x.shape
    xt = x.reshape(n * t, d)
    ffn = lambda layer: (p["norm_ffn"], p["w_gate"], p["w_up"], p["w_down"], layer)
    new_conv, new_k, new_v, new_pool = [], [], [], []
    for layer in range(depth):
        i = layer // 2
        if layer % 2 == 0:
            conv_out, q, k, v, glu_tail = _inproj_conv_prompt(
                xt.reshape(n, t, d), p["norm_mix"], p["w_in"], p["q_gain"], p["k_gain"], p["w_taps"], p["b_dw"],
                p["ln_g"], p["ln_b"], layer, i)
            attn = _attn_prompt(q, k, v, p["sinks"][i])
            xt = _outproj_ffn(xt, conv_out.reshape(n * t, -1), attn.reshape(n * t, -1), p["w_out"], i, *ffn(layer))
            keep = min(WINDOW, t)
            new_conv.append(glu_tail[:, CONV_HALO - (CONV_WIDTH - 1):])
            new_k.append(k[:, t - keep:].reshape(n, keep, N_KV_HEADS, HEAD_DIM))
            new_v.append(v[:, t - keep:].reshape(n, keep, N_KV_HEADS, HEAD_DIM))
        else:
            y, tail = _pool_ffn_prompt(xt.reshape(n, t, d), p["norm_mix"], p["w_pool"], p["pool_scale"], layer, i,
                                       *ffn(layer)[:-1])
            xt = y.reshape(n * t, d)
            new_pool.append(tail[:, POOL_HALO - POOL_BUF:])
    return xt.reshape(n, t, d), jnp.stack(new_conv), jnp.stack(new_k), jnp.stack(new_v), jnp.stack(new_pool)


def _trunk_sample(x, conv_bufs, k_bufs, v_bufs, pool_bufs, pos0, p, depth):
    n, t, d = x.shape
    x_tm = x.transpose(1, 0, 2)
    conv_tm = conv_bufs.transpose(0, 2, 1, 3)
    pool_tm = pool_bufs.transpose(0, 2, 1, 3)
    k_t = k_bufs.transpose(0, 1, 3, 4, 2)
    v_t = v_bufs.transpose(0, 1, 3, 4, 2)
    ffn = lambda layer: (p["norm_ffn"], p["w_gate"], p["w_up"], p["w_down"], layer)
    conv_new = kv_new = pool_new = None
    for layer in range(depth):
        i = layer // 2
        if layer % 2 == 0:
            xt = x_tm.reshape(t * n, d)
            glu, q, k, v = _inproj(xt, p["norm_mix"], p["w_in"], p["q_gain"], p["k_gain"], layer, i)
            conv_out, conv_new = _conv_sample(conv_tm, glu.reshape(t, n, D_CONV), p["w_dw"], p["b_dw"], p["ln_g"],
                                              p["ln_b"], i, None if conv_new is None else [conv_new])
            qh = q.reshape(t, n, N_KV_HEADS, GQA_GROUP, HEAD_DIM).transpose(1, 2, 3, 0, 4)
            qh = qh.reshape(n, N_KV_HEADS, GQA_GROUP * t, HEAD_DIM)
            kn = k.reshape(t, n, N_KV_HEADS, HEAD_DIM).transpose(1, 2, 3, 0)
            vn = v.reshape(t, n, N_KV_HEADS, HEAD_DIM).transpose(1, 2, 3, 0)
            oh, *kv_new = _attn_sample(qh, k_t, kn, v_t, vn, p["sinks"][i], i, kv_new)
            attn = oh.reshape(n, N_KV_HEADS, GQA_GROUP, t, HEAD_DIM).transpose(3, 0, 1, 2, 4).reshape(t * n, D_ATTN)
            xt = _outproj_ffn(xt, conv_out.reshape(t * n, D_CONV), attn, p["w_out"], i, *ffn(layer))
            x_tm = xt.reshape(t, n, d)
        else:
            x_tm, pool_new = _pool_ffn_sample(x_tm, pool_tm, p["norm_mix"], p["w_pool"], p["pool_scale"], layer, i, pos0,
                                              *ffn(layer)[:-1], None if pool_new is None else [pool_new])
    return (x_tm.transpose(1, 0, 2), conv_new.transpose(0, 2, 1, 3), kv_new[0].transpose(0, 1, 4, 2, 3),
            kv_new[1].transpose(0, 1, 4, 2, 3), pool_new.transpose(0, 2, 1, 3))


def kernel(x_prompt, x_sample, cache_conv, cache_k, cache_v, state_pool, norm_mix, norm_ffn, w_in, q_norm, k_norm,
           sinks, w_dw, b_dw, conv_norm_g, conv_norm_b, w_out, w_pool, pool_scale, w_gate, w_up, w_down):
    depth = norm_mix.shape[0]
    p = _prepare_weights(norm_mix, norm_ffn, w_in, q_norm, k_norm, sinks, w_dw, b_dw, conv_norm_g, conv_norm_b, w_out,
                         w_pool, pool_scale, w_gate, w_up, w_down)
    y_p, conv_p, k_p, v_p, pool_p = _trunk_prompt(x_prompt, p, depth)
    y_s, conv_s, k_s, v_s, pool_s = _trunk_sample(x_sample, cache_conv, cache_k, cache_v, state_pool, PAST_LEN, p, depth)
    return (y_p, y_s, conv_p, k_p, v_p, pool_p, conv_s, k_s, v_s, pool_s)
```

```python
import functools

import jax
import jax.numpy as jnp
from jax import lax
from jax.experimental import pallas as pl
from jax.experimental.pallas import tpu as pltpu

F32 = jnp.float32
BF16 = jnp.bfloat16

HEAD_DIM = 64
N_HEADS = 8
N_KV_HEADS = 2
GQA_GROUP = N_HEADS // N_KV_HEADS
D_ATTN = N_HEADS * HEAD_DIM
D_KV = N_KV_HEADS * HEAD_DIM
WINDOW = 128
PAST_LEN = 8192
D_CONV = 512
CONV_WIDTH = 31
CONV_HALO = 32
POOL_WINDOWS = (2, 4, 8, 16)
POOL_BUF = max(POOL_WINDOWS) - 1
POOL_HALO = 16
RMS_EPS = 1e-6
LN_EPS = 1e-5
NEG_INF = -1e30
LANES = 128
ALIBI_SLOPES = tuple(2.0 ** (-8.0 * (h + 1) / N_HEADS) for h in range(N_HEADS))

TOKEN_TILE = 512
CONV_PHASES = 4
ATTN_TIME_TILE = 1024
POOL_TIME_TILE = 512
POOL_PHASES = 4
FFN_CHUNK = 1536
SAMPLE_SEQ_TILE = 32
SAMPLE_FFN_SEQ_TILE = 64
VMEM_LIMIT = 56 * 1024 * 1024


def _params(n_axes):
    return pltpu.CompilerParams(dimension_semantics=("arbitrary",) * n_axes, vmem_limit_bytes=VMEM_LIMIT)


def _rmsnorm(x, g):
    return x * lax.rsqrt(jnp.mean(x * x, axis=-1, keepdims=True) + RMS_EPS) * g


def _head_rmsnorm(x, g):
    lane = lax.broadcasted_iota(jnp.int32, (x.shape[0], LANES), 1)
    low = lane < HEAD_DIM
    outs = []
    for s in range(x.shape[1] // LANES):
        xs = x[:, s * LANES:(s + 1) * LANES]
        sq = xs * xs
        s_low = jnp.sum(jnp.where(low, sq, 0.0), axis=-1, keepdims=True)
        s_high = jnp.sum(jnp.where(low, 0.0, sq), axis=-1, keepdims=True)
        ms = jnp.where(low, s_low, s_high) * (1.0 / HEAD_DIM)
        outs.append(xs * lax.rsqrt(ms + RMS_EPS) * g[:, s * LANES:(s + 1) * LANES])
    return outs[0] if len(outs) == 1 else jnp.concatenate(outs, axis=-1)


def _layernorm_silu(y, g, b):
    mu = jnp.mean(y, axis=-1, keepdims=True)
    yc = y - mu
    var = jnp.mean(yc * yc, axis=-1, keepdims=True)
    z = yc * lax.rsqrt(var + LN_EPS) * g + b
    return z * jax.nn.sigmoid(z)


def _ffn_apply(x, g_ref, wg_ref, wu_ref, wd_ref):
    h = _rmsnorm(x, g_ref[...]).astype(BF16)
    acc = x
    d_ff = wg_ref.shape[1]
    for c0 in range(0, d_ff, FFN_CHUNK):
        c1 = min(c0 + FFN_CHUNK, d_ff)
        gate = jnp.dot(h, wg_ref[:, c0:c1], preferred_element_type=F32)
        up = jnp.dot(h, wu_ref[:, c0:c1], preferred_element_type=F32)
        a = (gate * jax.nn.sigmoid(gate) * up).astype(BF16)
        acc = acc + jnp.dot(a, wd_ref[c0:c1, :], preferred_element_type=F32)
    return acc


def _ffn_specs(w_gate, layer, n_axes):
    _, d, f = w_gate.shape
    at_layer = {1: lambda i: (layer, 0, 0), 2: lambda b, i: (layer, 0, 0)}[n_axes]
    resident = functools.partial(pl.BlockSpec, pipeline_mode=pl.Buffered(1))
    return [pl.BlockSpec((None, 1, d), at_layer), resident((None, d, f), at_layer), resident((None, d, f), at_layer),
            resident((None, f, d), at_layer)]


def _inproj_body(x_ref, g_ref, w_ref, qn_ref, kn_ref, glu_ref, q_ref, k_ref, v_ref):
    h = _rmsnorm(x_ref[...], g_ref[...]).astype(BF16)
    o_q = 2 * D_CONV
    ua = jnp.dot(h, w_ref[:, o_q:], preferred_element_type=F32)
    uc = jnp.dot(h, w_ref[:, :o_q], preferred_element_type=F32)
    q_ref[...] = (_head_rmsnorm(ua[:, :D_ATTN], qn_ref[...]) * (HEAD_DIM ** -0.5)).astype(BF16)
    k_ref[...] = _head_rmsnorm(ua[:, D_ATTN:D_ATTN + D_KV], kn_ref[...])
    v_ref[...] = ua[:, D_ATTN + D_KV:]
    glu_ref[...] = uc[:, :D_CONV] * jax.nn.sigmoid(uc[:, D_CONV:])


def _inproj(x, norm_g, w_in, q_gain, k_gain, layer, i_even):
    t, d = x.shape
    n = w_in.shape[2]
    row = lambda width: pl.BlockSpec((TOKEN_TILE, width), lambda i: (i, 0))
    return pl.pallas_call(
        _inproj_body,
        grid=(t // TOKEN_TILE,),
        in_specs=[row(d),
                  pl.BlockSpec((None, 1, d), lambda i: (layer, 0, 0)),
                  pl.BlockSpec((None, d, n), lambda i: (i_even, 0, 0), pipeline_mode=pl.Buffered(1)),
                  pl.BlockSpec((None, 1, D_ATTN), lambda i: (i_even, 0, 0)),
                  pl.BlockSpec((None, 1, D_KV), lambda i: (i_even, 0, 0))],
        out_specs=[row(D_CONV), row(D_ATTN), row(D_KV), row(D_KV)],
        out_shape=[jax.ShapeDtypeStruct((t, D_CONV), F32), jax.ShapeDtypeStruct((t, D_ATTN), BF16),
                   jax.ShapeDtypeStruct((t, D_KV), F32), jax.ShapeDtypeStruct((t, D_KV), F32)],
        compiler_params=_params(1),
        name="inproj",
    )(x, norm_g, w_in, q_gain, k_gain)


def _outproj_ffn_body(x_ref, c_ref, a_ref, w_ref, gf_ref, wg_ref, wu_ref, wd_ref, o_ref):
    y = jnp.dot(c_ref[...], w_ref[:D_CONV, :], preferred_element_type=F32)
    y = y + jnp.dot(a_ref[...], w_ref[D_CONV:, :], preferred_element_type=F32)
    o_ref[...] = _ffn_apply(x_ref[...] + y, gf_ref, wg_ref, wu_ref, wd_ref)


def _outproj_ffn(x, conv_out, attn, w_out, i_even, norm_ffn, w_gate, w_up, w_down, layer):
    t, d = x.shape
    row = lambda width: pl.BlockSpec((TOKEN_TILE, width), lambda i: (i, 0))
    return pl.pallas_call(
        _outproj_ffn_body,
        grid=(t // TOKEN_TILE,),
        in_specs=[row(d), row(D_CONV), row(D_ATTN),
                  pl.BlockSpec((None, D_CONV + D_ATTN, d), lambda i: (i_even, 0, 0), pipeline_mode=pl.Buffered(1))]
        + _ffn_specs(w_gate, layer, 1),
        out_specs=row(d),
        out_shape=jax.ShapeDtypeStruct((t, d), F32),
        compiler_params=_params(1),
        name="outproj_ffn",
    )(x, conv_out, attn, w_out, norm_ffn, w_gate, w_up, w_down)


def _depthwise_conv_halves(ext_ref, y_ref, w_ref, b_ref, slab, half_rows):
    lead = CONV_HALO - (CONV_WIDTH - 1)
    rows = 8 * CONV_PHASES
    lanes = slice(slab * LANES, (slab + 1) * LANES)

    def chunk(c, carry):
        r0 = c * rows
        acc = [jnp.zeros((16, LANES), F32) for _ in range(CONV_PHASES)]
        for e in range(CONV_WIDTH + CONV_PHASES - 1):
            v = jnp.concatenate([ext_ref[slab, pl.ds(r0 + lead + e, 8, stride=CONV_PHASES), :],
                                 ext_ref[slab, pl.ds(half_rows + r0 + lead + e, 8, stride=CONV_PHASES), :]],
                                axis=0).astype(BF16)
            for ph in range(CONV_PHASES):
                k = e - ph
                if 0 <= k < CONV_WIDTH:
                    acc[ph] = acc[ph] + v.astype(F32) * w_ref[k, :, lanes].astype(F32)
        bias = jnp.broadcast_to(b_ref[0:1, lanes], (16, LANES))
        for ph in range(CONV_PHASES):
            out = acc[ph] + bias
            y_ref[slab, pl.ds(r0 + ph, 8, stride=CONV_PHASES), :] = out[:8]
            y_ref[slab, pl.ds(half_rows + r0 + ph, 8, stride=CONV_PHASES), :] = out[8:]
        return carry

    lax.fori_loop(0, half_rows // rows, chunk, 0)


def _layernorm_silu_slabs(y_ref, g_ref, beta_ref, o_ref):
    n_slabs = y_ref.shape[0]
    ys = [y_ref[s] for s in range(n_slabs)]
    inv_c = 1.0 / (n_slabs * LANES)
    mu = jnp.sum(sum(ys), axis=-1, keepdims=True) * inv_c
    ycs = [y - mu for y in ys]
    var = jnp.sum(sum(yc * yc for yc in ycs), axis=-1, keepdims=True) * inv_c
    rstd = lax.rsqrt(var + LN_EPS)
    for s, yc in enumerate(ycs):
        lanes = slice(s * LANES, (s + 1) * LANES)
        z = yc * rstd * g_ref[:, lanes] + beta_ref[:, lanes]
        o_ref[:, lanes] = (z * jax.nn.sigmoid(z)).astype(o_ref.dtype)


def _banded_attention(sink_ref, q_ref, kcat, vcat, o_ref, bias_ref, sinkcol_ref, s_ref):
    b = pl.program_id(0)
    i = pl.program_id(1)
    w = WINDOW
    pair = 2 * HEAD_DIM
    n_blocks = q_ref.shape[0] // w
    lane = lax.broadcasted_iota(jnp.int32, (1, pair), 1)
    low = lane < HEAD_DIM

    @pl.when((b == 0) & (i == 0))
    def _():
        row = lax.broadcasted_iota(jnp.int32, (w, 2 * w), 0)
        col = lax.broadcasted_iota(jnp.int32, (w, 2 * w), 1)
        dist = w + row - col
        valid = (dist >= 0) & (dist < w)
        distf = dist.astype(F32)
        for h in range(N_HEADS):
            slab, half = h // 2, h % 2
            bias_ref[slab * w:(slab + 1) * w, half * 2 * w:(half + 1) * 2 * w] = (
                jnp.where(valid, -ALIBI_SLOPES[h] * distf, NEG_INF))
            sinkcol_ref[half, slab * w:(slab + 1) * w, :] = jnp.full((w, 1), sink_ref[h], F32)

    krot = pltpu.roll(kcat, HEAD_DIM, 1)
    vrot = pltpu.roll(vcat, HEAD_DIM, 1)
    zero = jnp.zeros((), F32)
    k_low = [jnp.where(low, kcat, zero).astype(BF16), jnp.where(low, krot, zero).astype(BF16)]
    k_high = [jnp.where(low, zero, krot).astype(BF16), jnp.where(low, zero, kcat).astype(BF16)]
    v_low = [jnp.where(low, vcat, zero).astype(BF16), jnp.where(low, vrot, zero).astype(BF16)]
    v_high = [jnp.where(low, zero, vrot).astype(BF16), jnp.where(low, zero, vcat).astype(BF16)]
    ones_low = jnp.broadcast_to(jnp.where(low, 1.0, 0.0).astype(BF16), (2 * w, pair))
    ones_high = jnp.broadcast_to(jnp.where(low, 0.0, 1.0).astype(BF16), (2 * w, pair))
    ones_ext = jnp.concatenate([ones_low, ones_high], axis=0)
    col = lax.broadcasted_iota(jnp.int32, (1, 4 * w), 1)
    no_prev = (col % (2 * w) < w) & (i == 0)

    for jb in range(n_blocks):
        rows = slice(jb * w, (jb + 1) * w)
        keys = slice(jb * w, (jb + 2) * w)
        for g in range(N_KV_HEADS):
            q2 = jnp.concatenate([q_ref[rows, (2 * g) * pair:(2 * g + 1) * pair],
                                  q_ref[rows, (2 * g + 1) * pair:(2 * g + 2) * pair]], axis=0)
            kk = jnp.concatenate([k_low[g][keys], k_high[g][keys]], axis=0)
            s_ref[jb, 2 * g * w:(2 * g + 2) * w, :] = lax.dot_general(
                q2, kk, (((1,), (1,)), ((), ())), preferred_element_type=F32)

    for jb in range(n_blocks):
        rows = slice(jb * w, (jb + 1) * w)
        keys = slice(jb * w, (jb + 2) * w)
        bias = bias_ref[...]
        if jb == 0:
            bias = jnp.where(no_prev, NEG_INF, bias)
        s = s_ref[jb] + bias
        ps, sink_terms = [], []
        for half in range(2):
            sh = s[:, half * 2 * w:(half + 1) * 2 * w]
            sink = sinkcol_ref[half]
            m = jnp.maximum(jnp.max(sh, axis=-1, keepdims=True), sink)
            ps.append(jnp.exp(sh - m).astype(BF16))
            sink_terms.append(jnp.exp(sink - m))
        p = jnp.concatenate(ps, axis=1)
        sink_term = jnp.where(low, sink_terms[0], sink_terms[1])
        for g in range(N_KV_HEADS):
            vv = jnp.concatenate([jnp.concatenate([v_low[g][keys], v_high[g][keys]], axis=0), ones_ext], axis=1)
            o = jnp.dot(p[2 * g * w:(2 * g + 2) * w], vv, preferred_element_type=F32)
            o = o[:, :pair] / (o[:, pair:] + sink_term[2 * g * w:(2 * g + 2) * w])
            o_ref[rows, (2 * g) * pair:(2 * g + 1) * pair] = o[:w].astype(o_ref.dtype)
            o_ref[rows, (2 * g + 1) * pair:(2 * g + 2) * pair] = o[w:].astype(o_ref.dtype)


def _attn_prompt_body(sink_ref, q_ref, kc_ref, kp_ref, vc_ref, vp_ref, o_ref, bias_ref, sinkcol_ref, s_ref):
    kcat = jnp.concatenate([kp_ref[...], kc_ref[...]], axis=0)
    vcat = jnp.concatenate([vp_ref[...], vc_ref[...]], axis=0)
    _banded_attention(sink_ref, q_ref, kcat, vcat, o_ref, bias_ref, sinkcol_ref, s_ref)


def _attn_prompt(q, k, v, sinks):
    n, t, _ = q.shape
    tt = ATTN_TIME_TILE
    cur = lambda width: pl.BlockSpec((None, tt, width), lambda b, i: (b, i, 0))
    prev = lambda width: pl.BlockSpec((None, WINDOW, width),
                                      lambda b, i: (b, jnp.maximum(i * (tt // WINDOW) - 1, 0), 0))
    return pl.pallas_call(
        _attn_prompt_body,
        grid=(n, t // tt),
        in_specs=[pl.BlockSpec(memory_space=pltpu.SMEM), cur(D_ATTN), cur(D_KV), prev(D_KV), cur(D_KV), prev(D_KV)],
        out_specs=cur(D_ATTN),
        out_shape=jax.ShapeDtypeStruct((n, t, D_ATTN), BF16),
        scratch_shapes=[pltpu.VMEM((N_HEADS // 2 * WINDOW, 4 * WINDOW), F32),
                        pltpu.VMEM((2, N_HEADS // 2 * WINDOW, 1), F32),
                        pltpu.VMEM((tt // WINDOW, N_HEADS // 2 * WINDOW, 4 * WINDOW), F32)],
        compiler_params=_params(2),
        name="attn_prompt",
    )(sinks, q, k, k, v, v)


def _inproj_conv_prompt_body(x_ref, g_ref, w_ref, qn_ref, kn_ref, wt_ref, b_ref, lg_ref, lb_ref,
                             conv_ref, q_ref, k_ref, v_ref, glu_tail_ref, ext_ref, y_ref):
    tt = x_ref.shape[0]
    n_slabs = D_CONV // LANES
    i = pl.program_id(1)

    @pl.when(i == 0)
    def _():
        ext_ref[:, 0:CONV_HALO, :] = jnp.zeros((n_slabs, CONV_HALO, LANES), F32)

    @pl.when(i > 0)
    def _():
        ext_ref[:, 0:CONV_HALO, :] = ext_ref[:, tt:tt + CONV_HALO, :]

    h = _rmsnorm(x_ref[...], g_ref[...]).astype(BF16)
    o_q = 2 * D_CONV
    ua = jnp.dot(h, w_ref[:, o_q:], preferred_element_type=F32)
    uc = jnp.dot(h, w_ref[:, :o_q], preferred_element_type=F32)
    q_ref[...] = (_head_rmsnorm(ua[:, :D_ATTN], qn_ref[...]) * (HEAD_DIM ** -0.5)).astype(BF16)
    k_ref[...] = _head_rmsnorm(ua[:, D_ATTN:D_ATTN + D_KV], kn_ref[...])
    v_ref[...] = ua[:, D_ATTN + D_KV:]
    glu = uc[:, :D_CONV] * jax.nn.sigmoid(uc[:, D_CONV:])
    glu_tail_ref[...] = glu[tt - CONV_HALO:, :]
    for s in range(n_slabs):
        ext_ref[s, CONV_HALO:, :] = glu[:, s * LANES:(s + 1) * LANES]
    for s in range(n_slabs):
        _depthwise_conv_halves(ext_ref, y_ref, wt_ref, b_ref, s, tt // 2)
    _layernorm_silu_slabs(y_ref, lg_ref, lb_ref, conv_ref)


def _inproj_conv_prompt(x, norm_g, w_in, q_gain, k_gain, w_taps, b_dw, ln_g, ln_b, layer, i_even):
    n, t, d = x.shape
    n_in = w_in.shape[2]
    tt = ATTN_TIME_TILE
    tile = lambda width: pl.BlockSpec((None, tt, width), lambda b, i: (b, i, 0))
    even = lambda width: pl.BlockSpec((None, 1, width), lambda b, i: (i_even, 0, 0))
    return pl.pallas_call(
        _inproj_conv_prompt_body,
        grid=(n, t // tt),
        in_specs=[tile(d),
                  pl.BlockSpec((None, 1, d), lambda b, i: (layer, 0, 0)),
                  pl.BlockSpec((None, d, n_in), lambda b, i: (i_even, 0, 0), pipeline_mode=pl.Buffered(1)),
                  even(D_ATTN), even(D_KV),
                  pl.BlockSpec((None, CONV_WIDTH, 16, D_CONV), lambda b, i: (i_even, 0, 0, 0)),
                  even(D_CONV), even(D_CONV), even(D_CONV)],
        out_specs=[tile(D_CONV), tile(D_ATTN), tile(D_KV), tile(D_KV),
                   pl.BlockSpec((None, CONV_HALO, D_CONV), lambda b, i: (b, 0, 0))],
        out_shape=[jax.ShapeDtypeStruct((n, t, D_CONV), BF16), jax.ShapeDtypeStruct((n, t, D_ATTN), BF16),
                   jax.ShapeDtypeStruct((n, t, D_KV), F32), jax.ShapeDtypeStruct((n, t, D_KV), F32),
                   jax.ShapeDtypeStruct((n, CONV_HALO, D_CONV), F32)],
        scratch_shapes=[pltpu.VMEM((D_CONV // LANES, CONV_HALO + tt, LANES), F32),
                        pltpu.VMEM((D_CONV // LANES, tt, LANES), F32)],
        compiler_params=_params(2),
        name="inproj_conv_prompt",
    )(x, norm_g, w_in, q_gain, k_gain, w_taps, b_dw, ln_g, ln_b)


def _window_sums(vals, w, n_out):
    shared = list(range(n_out - 1, w))
    base = None
    for e in shared:
        base = vals[e] if base is None else base + vals[e]
    sums = []
    for i in range(n_out):
        acc = base
        for e in range(i, i + w):
            if e not in shared:
                acc = vals[e] if acc is None else acc + vals[e]
        sums.append(acc)
    return sums


def _pool_deltas_strided(ext_ref, d_ref, pos0, n_rows):
    n_slabs = ext_ref.shape[0]
    slabs_per_group = n_slabs // len(POOL_WINDOWS)
    rows = 8 * POOL_PHASES

    def chunk(c, carry):
        r0 = c * rows
        sub = lax.broadcasted_iota(jnp.int32, (8, LANES), 0)
        pos = [pos0 + r0 + ph + POOL_PHASES * sub for ph in range(POOL_PHASES)]
        inv = {w: [1.0 / jnp.minimum(w, p + 1).astype(F32) for p in pos] for w in POOL_WINDOWS}
        for slab in range(n_slabs):
            w = POOL_WINDOWS[slab // slabs_per_group]
            first = POOL_HALO - (w - 1)
            vals = [ext_ref[slab, pl.ds(r0 + first + e, 8, stride=POOL_PHASES), :] for e in range(w + POOL_PHASES - 1)]
            sums = _window_sums(vals, w, POOL_PHASES)
            for ph in range(POOL_PHASES):
                d_ref[slab, pl.ds(r0 + ph, 8, stride=POOL_PHASES), :] = sums[ph] * inv[w][ph] - vals[ph + w - 1]
        return carry

    lax.fori_loop(0, n_rows // rows, chunk, 0)


def _pool_ffn_prompt_body(x_ref, g_ref, w_ref, sc_ref, gf_ref, wg_ref, wu_ref, wd_ref, o_ref, tail_ref, ext_ref, d_ref):
    tt, d = x_ref.shape
    n_slabs = d // LANES
    slabs_per_group = n_slabs // len(POOL_WINDOWS)
    i = pl.program_id(1)

    @pl.when(i == 0)
    def _():
        ext_ref[:, 0:POOL_HALO, :] = jnp.zeros((n_slabs, POOL_HALO, LANES), F32)

    @pl.when(i > 0)
    def _():
        ext_ref[:, 0:POOL_HALO, :] = ext_ref[:, tt:tt + POOL_HALO, :]

    x = x_ref[...]
    xn = _rmsnorm(x, g_ref[...])
    for s in range(n_slabs):
        ext_ref[s, POOL_HALO:, :] = xn[:, s * LANES:(s + 1) * LANES]
    tail_ref[...] = xn[tt - POOL_HALO:, :]
    _pool_deltas_strided(ext_ref, d_ref, i * tt, tt)
    ys = []
    for g in range(len(POOL_WINDOWS)):
        dg = jnp.concatenate([d_ref[s] for s in range(g * slabs_per_group, (g + 1) * slabs_per_group)], axis=-1)
        ys.append(jnp.dot(dg.astype(BF16), w_ref[g], preferred_element_type=F32))
    x1 = x + jnp.concatenate(ys, axis=-1) * sc_ref[...]
    o_ref[...] = _ffn_apply(x1, gf_ref, wg_ref, wu_ref, wd_ref)


def _pool_ffn_prompt(x, norm_g, w_pool, pool_scale, layer, i_odd, norm_ffn, w_gate, w_up, w_down):
    n, t, d = x.shape
    tt = POOL_TIME_TILE
    ng, gd, _ = w_pool.shape[1:]
    tile = pl.BlockSpec((None, tt, d), lambda b, i: (b, i, 0))
    return pl.pallas_call(
        _pool_ffn_prompt_body,
        grid=(n, t // tt),
        in_specs=[tile,
                  pl.BlockSpec((None, 1, d), lambda b, i: (layer, 0, 0)),
                  pl.BlockSpec((None, ng, gd, gd), lambda b, i: (i_odd, 0, 0, 0)),
                  pl.BlockSpec((None, 1, d), lambda b, i: (i_odd, 0, 0))] + _ffn_specs(w_gate, layer, 2),
        out_specs=[tile, pl.BlockSpec((None, POOL_HALO, d), lambda b, i: (b, 0, 0))],
        out_shape=[jax.ShapeDtypeStruct((n, t, d), F32), jax.ShapeDtypeStruct((n, POOL_HALO, d), F32)],
        scratch_shapes=[pltpu.VMEM((d // LANES, POOL_HALO + tt, LANES), F32), pltpu.VMEM((d // LANES, tt, LANES), F32)],
        compiler_params=_params(2),
        name="pool_ffn_prompt",
    )(x, norm_g, w_pool, pool_scale, norm_ffn, w_gate, w_up, w_down)


def _state_alias(states, first_input, first_output):
    aliases = {first_input + j: first_output + j for j in range(len(states))}
    return list(states), [pl.BlockSpec(memory_space=pl.ANY)] * len(states), aliases


def _conv_sample_body(cache_ref, glu_ref, w_ref, b_ref, g_ref, beta_ref, *refs):
    o_ref, new_ref = refs[-2:]
    hist = cache_ref.shape[0]
    t = glu_ref.shape[0]
    ext = lambda j: cache_ref[j] if j < hist else glu_ref[j - hist]
    rounded = lambda a: a.astype(BF16).astype(F32)
    for j in range(hist):
        new_ref[j] = ext(t + j)
    for step in range(t):
        acc = rounded(w_ref[0:1, :]) * rounded(ext(step)) + b_ref[...]
        for k in range(1, CONV_WIDTH):
            acc = acc + rounded(w_ref[k:k + 1, :]) * rounded(ext(step + k))
        o_ref[step] = _layernorm_silu(acc, g_ref[...], beta_ref[...]).astype(o_ref.dtype)


def _conv_sample(cache_tm, glu_tm, w_dw, b_dw, ln_g, ln_b, i_even, states):
    n_even, hist, n, c = cache_tm.shape
    t = glu_tm.shape[0]
    nb = SAMPLE_SEQ_TILE
    vec = pl.BlockSpec((None, 1, c), lambda b: (i_even, 0, 0))
    slab = pl.BlockSpec((None, hist, nb, c), lambda b: (i_even, 0, b, 0))
    alias_in, alias_specs, aliases = _state_alias(states, 6, 1)
    return pl.pallas_call(
        _conv_sample_body,
        grid=(n // nb,),
        in_specs=[slab,
                  pl.BlockSpec((t, nb, c), lambda b: (0, b, 0)),
                  pl.BlockSpec((None, CONV_WIDTH, c), lambda b: (i_even, 0, 0)),
                  vec, vec, vec] + alias_specs,
        out_specs=[pl.BlockSpec((t, nb, c), lambda b: (0, b, 0)), slab],
        out_shape=[jax.ShapeDtypeStruct((t, n, c), BF16), jax.ShapeDtypeStruct((n_even, hist, n, c), F32)],
        input_output_aliases=aliases,
        compiler_params=_params(1),
        name="conv_sample",
    )(cache_tm, glu_tm, w_dw, b_dw, ln_g, ln_b, *alias_in)


def _attn_sample_body(sink_ref, q_ref, kc_ref, kn_ref, vc_ref, vn_ref, *refs):
    o_ref, knew_ref, vnew_ref = refs[-3:]
    nb, nkv, rows, hd = q_ref.shape
    buf = kc_ref.shape[-1]
    t = kn_ref.shape[-1]
    batch = nb * nkv
    q = q_ref[...].reshape(batch, rows, hd)
    kc = kc_ref[...].reshape(batch, hd, buf)
    kn = kn_ref[...].reshape(batch, hd, t)
    vc = vc_ref[...].reshape(batch, hd, buf)
    vn = vn_ref[...].reshape(batch, hd, t)

    lane = lax.broadcasted_iota(jnp.int32, (1, 1, buf), 2)
    pad = jnp.zeros((batch, hd, buf - t), F32)
    k_new = jnp.where(lane < buf - t, pltpu.roll(kc, buf - t, 2), jnp.concatenate([pad, kn], axis=-1))
    v_new = jnp.where(lane < buf - t, pltpu.roll(vc, buf - t, 2), jnp.concatenate([pad, vn], axis=-1))
    knew_ref[...] = k_new.reshape(nb, nkv, hd, buf)
    vnew_ref[...] = v_new.reshape(nb, nkv, hd, buf)

    s_c = jnp.einsum("bqd,bdk->bqk", q, kc.astype(BF16), preferred_element_type=F32)
    s_n = jnp.einsum("bqd,bdk->bqk", q, kn.astype(BF16), preferred_element_type=F32)
    step = lax.broadcasted_iota(jnp.int32, (rows, 1), 0) % t
    head_in_group = lax.broadcasted_iota(jnp.int32, (rows, 1), 0) // t
    dist_c = buf + step - lax.broadcasted_iota(jnp.int32, (rows, buf), 1)
    dist_n = step - lax.broadcasted_iota(jnp.int32, (rows, t), 1)
    s_c = s_c.reshape(nb, nkv, rows, buf)
    s_n = s_n.reshape(nb, nkv, rows, t)
    ps_c, ps_n, invs = [], [], []
    for kv in range(nkv):
        slope = jnp.zeros((rows, 1), F32)
        sink = jnp.zeros((rows, 1), F32)
        for j in range(GQA_GROUP):
            slope = jnp.where(head_in_group == j, ALIBI_SLOPES[kv * GQA_GROUP + j], slope)
            sink = jnp.where(head_in_group == j, sink_ref[kv * GQA_GROUP + j], sink)
        sc = jnp.where((dist_c >= 0) & (dist_c < WINDOW), s_c[:, kv] - slope * dist_c.astype(F32), NEG_INF)
        sn = jnp.where((dist_n >= 0) & (dist_n < WINDOW), s_n[:, kv] - slope * dist_n.astype(F32), NEG_INF)
        m = jnp.maximum(jnp.maximum(jnp.max(sc, axis=-1, keepdims=True), jnp.max(sn, axis=-1, keepdims=True)), sink)
        pc = jnp.exp(sc - m)
        pn = jnp.exp(sn - m)
        denom = jnp.sum(pc, axis=-1, keepdims=True) + jnp.sum(pn, axis=-1, keepdims=True) + jnp.exp(sink - m)
        ps_c.append(pc.astype(BF16))
        ps_n.append(pn.astype(BF16))
        invs.append(1.0 / denom)
    p_c = jnp.stack(ps_c, axis=1).reshape(batch, rows, buf)
    p_n = jnp.stack(ps_n, axis=1).reshape(batch, rows, t)
    inv = jnp.stack(invs, axis=1).reshape(batch, rows, 1)
    o = jnp.einsum("bqk,bdk->bqd", p_c, vc.astype(BF16), preferred_element_type=F32)
    o = o + jnp.einsum("bqk,bdk->bqd", p_n, vn.astype(BF16), preferred_element_type=F32)
    o_ref[...] = (o * inv).reshape(nb, nkv, rows, hd).astype(o_ref.dtype)


def _attn_sample(q, kc, kn, vc, vn, sinks, i_even, states):
    n, nkv, rows, hd = q.shape
    n_even = kc.shape[0]
    buf = kc.shape[-1]
    t = kn.shape[-1]
    nb = SAMPLE_SEQ_TILE
    cache = pl.BlockSpec((None, nb, nkv, hd, buf), lambda b: (i_even, b, 0, 0, 0))
    new = pl.BlockSpec((nb, nkv, hd, t), lambda b: (b, 0, 0, 0))
    qspec = pl.BlockSpec((nb, nkv, rows, hd), lambda b: (b, 0, 0, 0))
    alias_in, alias_specs, aliases = _state_alias(states, 6, 1)
    state_shape = jax.ShapeDtypeStruct((n_even, n, nkv, hd, buf), F32)
    return pl.pallas_call(
        _attn_sample_body,
        grid=(n // nb,),
        in_specs=[pl.BlockSpec(memory_space=pltpu.SMEM), qspec, cache, new, cache, new] + alias_specs,
        out_specs=[qspec, cache, cache],
        out_shape=[jax.ShapeDtypeStruct((n, nkv, rows, hd), BF16), state_shape, state_shape],
        input_output_aliases=aliases,
        compiler_params=_params(1),
        name="attn_sample",
    )(sinks, q, kc, kn, vc, vn, *alias_in)


def _pool_ffn_sample_body(x_ref, st_ref, g_ref, w_ref, sc_ref, gf_ref, wg_ref, wu_ref, wd_ref, *refs, pos0):
    o_ref, new_ref = refs[-2:]
    t, nb, d = x_ref.shape
    hist = st_ref.shape[0]
    gdim = d // len(POOL_WINDOWS)
    x = x_ref[...].reshape(t * nb, d)
    xn = _rmsnorm(x, g_ref[...])
    ext = lambda j: st_ref[j] if j < hist else xn[(j - hist) * nb:(j - hist + 1) * nb]
    for j in range(hist):
        new_ref[j] = ext(t + j)
    ys = []
    for g, w in enumerate(POOL_WINDOWS):
        lanes = slice(g * gdim, (g + 1) * gdim)
        steps = []
        for step in range(t):
            acc = ext(hist + step)[:, lanes]
            for j in range(1, w):
                acc = acc + ext(hist + step - j)[:, lanes]
            cnt = float(min(w, pos0 + step + 1))
            steps.append(acc / cnt - xn[step * nb:(step + 1) * nb, lanes])
        dg = jnp.concatenate(steps, axis=0)
        ys.append(jnp.dot(dg.astype(BF16), w_ref[g], preferred_element_type=F32))
    x1 = x + jnp.concatenate(ys, axis=-1) * sc_ref[...]
    o_ref[...] = _ffn_apply(x1, gf_ref, wg_ref, wu_ref, wd_ref).reshape(t, nb, d)


def _pool_ffn_sample(x_tm, state_tm, norm_g, w_pool, pool_scale, layer, i_odd, pos0, norm_ffn, w_gate, w_up, w_down,
                     states):
    t, n, d = x_tm.shape
    n_odd, hist = state_tm.shape[:2]
    nb = SAMPLE_FFN_SEQ_TILE
    ng, gd, _ = w_pool.shape[1:]
    slab = pl.BlockSpec((None, hist, nb, d), lambda b: (i_odd, 0, b, 0))
    alias_in, alias_specs, aliases = _state_alias(states, 9, 1)
    return pl.pallas_call(
        functools.partial(_pool_ffn_sample_body, pos0=pos0),
        grid=(n // nb,),
        in_specs=[pl.BlockSpec((t, nb, d), lambda b: (0, b, 0)),
                  slab,
                  pl.BlockSpec((None, 1, d), lambda b: (layer, 0, 0)),
                  pl.BlockSpec((None, ng, gd, gd), lambda b: (i_odd, 0, 0, 0)),
                  pl.BlockSpec((None, 1, d), lambda b: (i_odd, 0, 0))] + _ffn_specs(w_gate, layer, 1) + alias_specs,
        out_specs=[pl.BlockSpec((t, nb, d), lambda b: (0, b, 0)), slab],
        out_shape=[jax.ShapeDtypeStruct((t, n, d), F32), jax.ShapeDtypeStruct((n_odd, hist, n, d), F32)],
        input_output_aliases=aliases,
        compiler_params=_params(1),
        name="pool_ffn_sample",
    )(x_tm, state_tm, norm_g, w_pool, pool_scale, norm_ffn, w_gate, w_up, w_down, *alias_in)


def _prepare_weights(norm_mix, norm_ffn, w_in, q_norm, k_norm, sinks, w_dw, b_dw, conv_norm_g, conv_norm_b, w_out,
                     w_pool, pool_scale, w_gate, w_up, w_down):
    vec = lambda a: a[:, None, :]
    return dict(
        norm_mix=vec(norm_mix), norm_ffn=vec(norm_ffn), w_in=w_in.astype(BF16),
        q_gain=vec(jnp.tile(q_norm, (1, N_HEADS))), k_gain=vec(jnp.tile(k_norm, (1, N_KV_HEADS))), sinks=sinks,
        w_dw=w_dw, b_dw=vec(b_dw),
        w_taps=jnp.broadcast_to(w_dw.astype(BF16)[:, :, None, :], w_dw.shape[:2] + (16, w_dw.shape[2])),
        ln_g=vec(conv_norm_g), ln_b=vec(conv_norm_b), w_out=w_out.astype(BF16),
        w_pool=w_pool.astype(BF16), pool_scale=vec(pool_scale),
        w_gate=w_gate.astype(BF16), w_up=w_up.astype(BF16), w_down=w_down.astype(BF16))


def _trunk_prompt(x, p, depth):
    n, t, d = x.shape
    xt = x.reshape(n * t, d)
    ffn = lambda layer: (p["norm_ffn"], p["w_gate"], p["w_up"], p["w_down"], layer)
    new_conv, new_k, new_v, new_pool = [], [], [], []
    for layer in range(depth):
        i = layer // 2
        if layer % 2 == 0:
            conv_out, q, k, v, glu_tail = _inproj_conv_prompt(
                xt.reshape(n, t, d), p["norm_mix"], p["w_in"], p["q_gain"], p["k_gain"], p["w_taps"], p["b_dw"],
                p["ln_g"], p["ln_b"], layer, i)
            attn = _attn_prompt(q, k, v, p["sinks"][i])
            xt = _outproj_ffn(xt, conv_out.reshape(n * t, -1), attn.reshape(n * t, -1), p["w_out"], i, *ffn(layer))
            keep = min(WINDOW, t)
            new_conv.append(glu_tail[:, CONV_HALO - (CONV_WIDTH - 1):])
            new_k.append(k[:, t - keep:].reshape(n, keep, N_KV_HEADS, HEAD_DIM))
            new_v.append(v[:, t - keep:].reshape(n, keep, N_KV_HEADS, HEAD_DIM))
        else:
            y, tail = _pool_ffn_prompt(xt.reshape(n, t, d), p["norm_mix"], p["w_pool"], p["pool_scale"], layer, i,
                                       *ffn(layer)[:-1])
            xt = y.reshape(n * t, d)
            new_pool.append(tail[:, POOL_HALO - POOL_BUF:])
    return xt.reshape(n, t, d), jnp.stack(new_conv), jnp.stack(new_k), jnp.stack(new_v), jnp.stack(new_pool)


def _trunk_sample(x, conv_bufs, k_bufs, v_bufs, pool_bufs, pos0, p, depth):
    n, t, d = x.shape
    x_tm = x.transpose(1, 0, 2)
    conv_tm = conv_bufs.transpose(0, 2, 1, 3)
    pool_tm = pool_bufs.transpose(0, 2, 1, 3)
    k_t = k_bufs.transpose(0, 1, 3, 4, 2)
    v_t = v_bufs.transpose(0, 1, 3, 4, 2)
    ffn = lambda layer: (p["norm_ffn"], p["w_gate"], p["w_up"], p["w_down"], layer)
    conv_new = jnp.zeros(conv_tm.shape, F32)
    kv_new = [jnp.zeros(k_t.shape, F32), jnp.zeros(v_t.shape, F32)]
    pool_new = jnp.zeros(pool_tm.shape, F32)
    for layer in range(depth):
        i = layer // 2
        if layer % 2 == 0:
            xt = x_tm.reshape(t * n, d)
            glu, q, k, v = _inproj(xt, p["norm_mix"], p["w_in"], p["q_gain"], p["k_gain"], layer, i)
            conv_out, conv_new = _conv_sample(conv_tm, glu.reshape(t, n, D_CONV), p["w_dw"], p["b_dw"], p["ln_g"],
                                              p["ln_b"], i, [conv_new])
            qh = q.reshape(t, n, N_KV_HEADS, GQA_GROUP, HEAD_DIM).transpose(1, 2, 3, 0, 4)
            qh = qh.reshape(n, N_KV_HEADS, GQA_GROUP * t, HEAD_DIM)
            kn = k.reshape(t, n, N_KV_HEADS, HEAD_DIM).transpose(1, 2, 3, 0)
            vn = v.reshape(t, n, N_KV_HEADS, HEAD_DIM).transpose(1, 2, 3, 0)
            oh, *kv_new = _attn_sample(qh, k_t, kn, v_t, vn, p["sinks"][i], i, kv_new)
            attn = oh.reshape(n, N_KV_HEADS, GQA_GROUP, t, HEAD_DIM).transpose(3, 0, 1, 2, 4).reshape(t * n, D_ATTN)
            xt = _outproj_ffn(xt, conv_out.reshape(t * n, D_CONV), attn, p["w_out"], i, *ffn(layer))
            x_tm = xt.reshape(t, n, d)
        else:
            x_tm, pool_new = _pool_ffn_sample(x_tm, pool_tm, p["norm_mix"], p["w_pool"], p["pool_scale"], layer, i, pos0,
                                              *ffn(layer)[:-1], [pool_new])
    return (x_tm.transpose(1, 0, 2), conv_new.transpose(0, 2, 1, 3), kv_new[0].transpose(0, 1, 4, 2, 3),
            kv_new[1].transpose(0, 1, 4, 2, 3), pool_new.transpose(0, 2, 1, 3))


def kernel(x_prompt, x_sample, cache_conv, cache_k, cache_v, state_pool, norm_mix, norm_ffn, w_in, q_norm, k_norm,
           sinks, w_dw, b_dw, conv_norm_g, conv_norm_b, w_out, w_pool, pool_scale, w_gate, w_up, w_down):
    depth = norm_mix.shape[0]
    p = _prepare_weights(norm_mix, norm_ffn, w_in, q_norm, k_norm, sinks, w_dw, b_dw, conv_norm_g, conv_norm_b, w_out,
                         w_pool, pool_scale, w_gate, w_up, w_down)
    y_p, conv_p, k_p, v_p, pool_p = _trunk_prompt(x_prompt, p, depth)
    y_s, conv_s, k_s, v_s, pool_s = _trunk_sample(x_sample, cache_conv, cache_k, cache_v, state_pool, PAST_LEN, p, depth)
    return (y_p, y_s, conv_p, k_p, v_p, pool_p, conv_s, k_s, v_s, pool_s)
```

```python
import functools

import jax
import jax.numpy as jnp
from jax import lax
from jax.experimental import pallas as pl
from jax.experimental.pallas import tpu as pltpu

F32 = jnp.float32
BF16 = jnp.bfloat16

HEAD_DIM = 64
N_HEADS = 8
N_KV_HEADS = 2
GQA_GROUP = N_HEADS // N_KV_HEADS
D_ATTN = N_HEADS * HEAD_DIM
D_KV = N_KV_HEADS * HEAD_DIM
WINDOW = 128
PAST_LEN = 8192
D_CONV = 512
CONV_WIDTH = 31
CONV_HALO = 32
POOL_WINDOWS = (2, 4, 8, 16)
POOL_BUF = max(POOL_WINDOWS) - 1
POOL_HALO = 16
RMS_EPS = 1e-6
LN_EPS = 1e-5
NEG_INF = -1e30
LANES = 128
ALIBI_SLOPES = tuple(2.0 ** (-8.0 * (h + 1) / N_HEADS) for h in range(N_HEADS))

TOKEN_TILE = 512
CONV_PHASES = 4
ATTN_TIME_TILE = 1024
POOL_TIME_TILE = 512
POOL_PHASES = 4
FFN_CHUNK = 1536
SAMPLE_SEQ_TILE = 32
SAMPLE_FFN_SEQ_TILE = 64
ZERO_FILL_STEPS = 8
VMEM_LIMIT = 56 * 1024 * 1024


def _params(n_axes):
    return pltpu.CompilerParams(dimension_semantics=("arbitrary",) * n_axes, vmem_limit_bytes=VMEM_LIMIT)


def _rmsnorm(x, g):
    return x * lax.rsqrt(jnp.mean(x * x, axis=-1, keepdims=True) + RMS_EPS) * g


def _head_rmsnorm(x, g):
    lane = lax.broadcasted_iota(jnp.int32, (x.shape[0], LANES), 1)
    low = lane < HEAD_DIM
    outs = []
    for s in range(x.shape[1] // LANES):
        xs = x[:, s * LANES:(s + 1) * LANES]
        sq = xs * xs
        s_low = jnp.sum(jnp.where(low, sq, 0.0), axis=-1, keepdims=True)
        s_high = jnp.sum(jnp.where(low, 0.0, sq), axis=-1, keepdims=True)
        ms = jnp.where(low, s_low, s_high) * (1.0 / HEAD_DIM)
        outs.append(xs * lax.rsqrt(ms + RMS_EPS) * g[:, s * LANES:(s + 1) * LANES])
    return outs[0] if len(outs) == 1 else jnp.concatenate(outs, axis=-1)


def _layernorm_silu(y, g, b):
    mu = jnp.mean(y, axis=-1, keepdims=True)
    yc = y - mu
    var = jnp.mean(yc * yc, axis=-1, keepdims=True)
    z = yc * lax.rsqrt(var + LN_EPS) * g + b
    return z * jax.nn.sigmoid(z)


def _ffn_apply(x, g_ref, wg_ref, wu_ref, wd_ref):
    h = _rmsnorm(x, g_ref[...]).astype(BF16)
    acc = x
    d_ff = wg_ref.shape[1]
    for c0 in range(0, d_ff, FFN_CHUNK):
        c1 = min(c0 + FFN_CHUNK, d_ff)
        gate = jnp.dot(h, wg_ref[:, c0:c1], preferred_element_type=F32)
        up = jnp.dot(h, wu_ref[:, c0:c1], preferred_element_type=F32)
        a = (gate * jax.nn.sigmoid(gate) * up).astype(BF16)
        acc = acc + jnp.dot(a, wd_ref[c0:c1, :], preferred_element_type=F32)
    return acc


def _ffn_specs(w_gate, layer, n_axes):
    _, d, f = w_gate.shape
    at_layer = {1: lambda i: (layer, 0, 0), 2: lambda b, i: (layer, 0, 0)}[n_axes]
    resident = functools.partial(pl.BlockSpec, pipeline_mode=pl.Buffered(1))
    return [pl.BlockSpec((None, 1, d), at_layer), resident((None, d, f), at_layer), resident((None, d, f), at_layer),
            resident((None, f, d), at_layer)]


def _inproj_body(x_ref, g_ref, w_ref, qn_ref, kn_ref, glu_ref, q_ref, k_ref, v_ref):
    h = _rmsnorm(x_ref[...], g_ref[...]).astype(BF16)
    o_q = 2 * D_CONV
    ua = jnp.dot(h, w_ref[:, o_q:], preferred_element_type=F32)
    uc = jnp.dot(h, w_ref[:, :o_q], preferred_element_type=F32)
    q_ref[...] = (_head_rmsnorm(ua[:, :D_ATTN], qn_ref[...]) * (HEAD_DIM ** -0.5)).astype(BF16)
    k_ref[...] = _head_rmsnorm(ua[:, D_ATTN:D_ATTN + D_KV], kn_ref[...])
    v_ref[...] = ua[:, D_ATTN + D_KV:]
    glu_ref[...] = uc[:, :D_CONV] * jax.nn.sigmoid(uc[:, D_CONV:])


def _inproj(x, norm_g, w_in, q_gain, k_gain, layer, i_even):
    t, d = x.shape
    n = w_in.shape[2]
    row = lambda width: pl.BlockSpec((TOKEN_TILE, width), lambda i: (i, 0))
    return pl.pallas_call(
        _inproj_body,
        grid=(t // TOKEN_TILE,),
        in_specs=[row(d),
                  pl.BlockSpec((None, 1, d), lambda i: (layer, 0, 0)),
                  pl.BlockSpec((None, d, n), lambda i: (i_even, 0, 0), pipeline_mode=pl.Buffered(1)),
                  pl.BlockSpec((None, 1, D_ATTN), lambda i: (i_even, 0, 0)),
                  pl.BlockSpec((None, 1, D_KV), lambda i: (i_even, 0, 0))],
        out_specs=[row(D_CONV), row(D_ATTN), row(D_KV), row(D_KV)],
        out_shape=[jax.ShapeDtypeStruct((t, D_CONV), F32), jax.ShapeDtypeStruct((t, D_ATTN), BF16),
                   jax.ShapeDtypeStruct((t, D_KV), F32), jax.ShapeDtypeStruct((t, D_KV), F32)],
        compiler_params=_params(1),
        name="inproj",
    )(x, norm_g, w_in, q_gain, k_gain)


def _outproj_ffn_body(x_ref, c_ref, a_ref, w_ref, gf_ref, wg_ref, wu_ref, wd_ref, o_ref):
    y = jnp.dot(c_ref[...], w_ref[:D_CONV, :], preferred_element_type=F32)
    y = y + jnp.dot(a_ref[...], w_ref[D_CONV:, :], preferred_element_type=F32)
    o_ref[...] = _ffn_apply(x_ref[...] + y, gf_ref, wg_ref, wu_ref, wd_ref)


def _outproj_ffn(x, conv_out, attn, w_out, i_even, norm_ffn, w_gate, w_up, w_down, layer):
    t, d = x.shape
    row = lambda width: pl.BlockSpec((TOKEN_TILE, width), lambda i: (i, 0))
    return pl.pallas_call(
        _outproj_ffn_body,
        grid=(t // TOKEN_TILE,),
        in_specs=[row(d), row(D_CONV), row(D_ATTN),
                  pl.BlockSpec((None, D_CONV + D_ATTN, d), lambda i: (i_even, 0, 0), pipeline_mode=pl.Buffered(1))]
        + _ffn_specs(w_gate, layer, 1),
        out_specs=row(d),
        out_shape=jax.ShapeDtypeStruct((t, d), F32),
        compiler_params=_params(1),
        name="outproj_ffn",
    )(x, conv_out, attn, w_out, norm_ffn, w_gate, w_up, w_down)


def _depthwise_conv_halves(ext_ref, y_ref, w_ref, b_ref, slab, half_rows):
    lead = CONV_HALO - (CONV_WIDTH - 1)
    rows = 8 * CONV_PHASES
    lanes = slice(slab * LANES, (slab + 1) * LANES)

    def chunk(c, carry):
        r0 = c * rows
        acc = [jnp.zeros((16, LANES), F32) for _ in range(CONV_PHASES)]
        for e in range(CONV_WIDTH + CONV_PHASES - 1):
            v = jnp.concatenate([ext_ref[slab, pl.ds(r0 + lead + e, 8, stride=CONV_PHASES), :],
                                 ext_ref[slab, pl.ds(half_rows + r0 + lead + e, 8, stride=CONV_PHASES), :]],
                                axis=0).astype(BF16)
            for ph in range(CONV_PHASES):
                k = e - ph
                if 0 <= k < CONV_WIDTH:
                    acc[ph] = acc[ph] + v.astype(F32) * w_ref[k, :, lanes].astype(F32)
        bias = jnp.broadcast_to(b_ref[0:1, lanes], (16, LANES))
        for ph in range(CONV_PHASES):
            out = acc[ph] + bias
            y_ref[slab, pl.ds(r0 + ph, 8, stride=CONV_PHASES), :] = out[:8]
            y_ref[slab, pl.ds(half_rows + r0 + ph, 8, stride=CONV_PHASES), :] = out[8:]
        return carry

    lax.fori_loop(0, half_rows // rows, chunk, 0)


def _layernorm_silu_slabs(y_ref, g_ref, beta_ref, o_ref):
    n_slabs = y_ref.shape[0]
    ys = [y_ref[s] for s in range(n_slabs)]
    inv_c = 1.0 / (n_slabs * LANES)
    mu = jnp.sum(sum(ys), axis=-1, keepdims=True) * inv_c
    ycs = [y - mu for y in ys]
    var = jnp.sum(sum(yc * yc for yc in ycs), axis=-1, keepdims=True) * inv_c
    rstd = lax.rsqrt(var + LN_EPS)
    for s, yc in enumerate(ycs):
        lanes = slice(s * LANES, (s + 1) * LANES)
        z = yc * rstd * g_ref[:, lanes] + beta_ref[:, lanes]
        o_ref[:, lanes] = (z * jax.nn.sigmoid(z)).astype(o_ref.dtype)


def _banded_attention(sink_ref, q_ref, kcat, vcat, o_ref, bias_ref, sinkcol_ref, s_ref):
    b = pl.program_id(0)
    i = pl.program_id(1)
    w = WINDOW
    pair = 2 * HEAD_DIM
    n_blocks = q_ref.shape[0] // w
    lane = lax.broadcasted_iota(jnp.int32, (1, pair), 1)
    low = lane < HEAD_DIM

    @pl.when((b == 0) & (i == 0))
    def _():
        row = lax.broadcasted_iota(jnp.int32, (w, 2 * w), 0)
        col = lax.broadcasted_iota(jnp.int32, (w, 2 * w), 1)
        dist = w + row - col
        valid = (dist >= 0) & (dist < w)
        distf = dist.astype(F32)
        for h in range(N_HEADS):
            slab, half = h // 2, h % 2
            bias_ref[slab * w:(slab + 1) * w, half * 2 * w:(half + 1) * 2 * w] = (
                jnp.where(valid, -ALIBI_SLOPES[h] * distf, NEG_INF))
            sinkcol_ref[half, slab * w:(slab + 1) * w, :] = jnp.full((w, 1), sink_ref[h], F32)

    krot = pltpu.roll(kcat, HEAD_DIM, 1)
    vrot = pltpu.roll(vcat, HEAD_DIM, 1)
    zero = jnp.zeros((), F32)
    k_low = [jnp.where(low, kcat, zero).astype(BF16), jnp.where(low, krot, zero).astype(BF16)]
    k_high = [jnp.where(low, zero, krot).astype(BF16), jnp.where(low, zero, kcat).astype(BF16)]
    v_low = [jnp.where(low, vcat, zero).astype(BF16), jnp.where(low, vrot, zero).astype(BF16)]
    v_high = [jnp.where(low, zero, vrot).astype(BF16), jnp.where(low, zero, vcat).astype(BF16)]
    ones_low = jnp.broadcast_to(jnp.where(low, 1.0, 0.0).astype(BF16), (2 * w, pair))
    ones_high = jnp.broadcast_to(jnp.where(low, 0.0, 1.0).astype(BF16), (2 * w, pair))
    ones_ext = jnp.concatenate([ones_low, ones_high], axis=0)
    col = lax.broadcasted_iota(jnp.int32, (1, 4 * w), 1)
    no_prev = (col % (2 * w) < w) & (i == 0)

    for jb in range(n_blocks):
        rows = slice(jb * w, (jb + 1) * w)
        keys = slice(jb * w, (jb + 2) * w)
        for g in range(N_KV_HEADS):
            q2 = jnp.concatenate([q_ref[rows, (2 * g) * pair:(2 * g + 1) * pair],
                                  q_ref[rows, (2 * g + 1) * pair:(2 * g + 2) * pair]], axis=0)
            kk = jnp.concatenate([k_low[g][keys], k_high[g][keys]], axis=0)
            s_ref[jb, 2 * g * w:(2 * g + 2) * w, :] = lax.dot_general(
                q2, kk, (((1,), (1,)), ((), ())), preferred_element_type=F32)

    for jb in range(n_blocks):
        rows = slice(jb * w, (jb + 1) * w)
        keys = slice(jb * w, (jb + 2) * w)
        bias = bias_ref[...]
        if jb == 0:
            bias = jnp.where(no_prev, NEG_INF, bias)
        s = s_ref[jb] + bias
        ps, sink_terms = [], []
        for half in range(2):
            sh = s[:, half * 2 * w:(half + 1) * 2 * w]
            sink = sinkcol_ref[half]
            m = jnp.maximum(jnp.max(sh, axis=-1, keepdims=True), sink)
            ps.append(jnp.exp(sh - m).astype(BF16))
            sink_terms.append(jnp.exp(sink - m))
        p = jnp.concatenate(ps, axis=1)
        sink_term = jnp.where(low, sink_terms[0], sink_terms[1])
        for g in range(N_KV_HEADS):
            vv = jnp.concatenate([jnp.concatenate([v_low[g][keys], v_high[g][keys]], axis=0), ones_ext], axis=1)
            o = jnp.dot(p[2 * g * w:(2 * g + 2) * w], vv, preferred_element_type=F32)
            o = o[:, :pair] / (o[:, pair:] + sink_term[2 * g * w:(2 * g + 2) * w])
            o_ref[rows, (2 * g) * pair:(2 * g + 1) * pair] = o[:w].astype(o_ref.dtype)
            o_ref[rows, (2 * g + 1) * pair:(2 * g + 2) * pair] = o[w:].astype(o_ref.dtype)


def _attn_prompt_body(sink_ref, q_ref, kc_ref, kp_ref, vc_ref, vp_ref, o_ref, bias_ref, sinkcol_ref, s_ref):
    kcat = jnp.concatenate([kp_ref[...], kc_ref[...]], axis=0)
    vcat = jnp.concatenate([vp_ref[...], vc_ref[...]], axis=0)
    _banded_attention(sink_ref, q_ref, kcat, vcat, o_ref, bias_ref, sinkcol_ref, s_ref)


def _attn_prompt(q, k, v, sinks):
    n, t, _ = q.shape
    tt = ATTN_TIME_TILE
    cur = lambda width: pl.BlockSpec((None, tt, width), lambda b, i: (b, i, 0))
    prev = lambda width: pl.BlockSpec((None, WINDOW, width),
                                      lambda b, i: (b, jnp.maximum(i * (tt // WINDOW) - 1, 0), 0))
    return pl.pallas_call(
        _attn_prompt_body,
        grid=(n, t // tt),
        in_specs=[pl.BlockSpec(memory_space=pltpu.SMEM), cur(D_ATTN), cur(D_KV), prev(D_KV), cur(D_KV), prev(D_KV)],
        out_specs=cur(D_ATTN),
        out_shape=jax.ShapeDtypeStruct((n, t, D_ATTN), BF16),
        scratch_shapes=[pltpu.VMEM((N_HEADS // 2 * WINDOW, 4 * WINDOW), F32),
                        pltpu.VMEM((2, N_HEADS // 2 * WINDOW, 1), F32),
                        pltpu.VMEM((tt // WINDOW, N_HEADS // 2 * WINDOW, 4 * WINDOW), F32)],
        compiler_params=_params(2),
        name="attn_prompt",
    )(sinks, q, k, k, v, v)


def _inproj_conv_prompt_body(x_ref, g_ref, w_ref, qn_ref, kn_ref, wt_ref, b_ref, lg_ref, lb_ref,
                             conv_ref, q_ref, k_ref, v_ref, glu_tail_ref, ext_ref, y_ref):
    tt = x_ref.shape[0]
    n_slabs = D_CONV // LANES
    i = pl.program_id(1)

    @pl.when(i == 0)
    def _():
        ext_ref[:, 0:CONV_HALO, :] = jnp.zeros((n_slabs, CONV_HALO, LANES), F32)

    @pl.when(i > 0)
    def _():
        ext_ref[:, 0:CONV_HALO, :] = ext_ref[:, tt:tt + CONV_HALO, :]

    h = _rmsnorm(x_ref[...], g_ref[...]).astype(BF16)
    o_q = 2 * D_CONV
    ua = jnp.dot(h, w_ref[:, o_q:], preferred_element_type=F32)
    uc = jnp.dot(h, w_ref[:, :o_q], preferred_element_type=F32)
    q_ref[...] = (_head_rmsnorm(ua[:, :D_ATTN], qn_ref[...]) * (HEAD_DIM ** -0.5)).astype(BF16)
    k_ref[...] = _head_rmsnorm(ua[:, D_ATTN:D_ATTN + D_KV], kn_ref[...])
    v_ref[...] = ua[:, D_ATTN + D_KV:]
    glu = uc[:, :D_CONV] * jax.nn.sigmoid(uc[:, D_CONV:])
    glu_tail_ref[...] = glu[tt - CONV_HALO:, :]
    for s in range(n_slabs):
        ext_ref[s, CONV_HALO:, :] = glu[:, s * LANES:(s + 1) * LANES]
    for s in range(n_slabs):
        _depthwise_conv_halves(ext_ref, y_ref, wt_ref, b_ref, s, tt // 2)
    _layernorm_silu_slabs(y_ref, lg_ref, lb_ref, conv_ref)


def _inproj_conv_prompt(x, norm_g, w_in, q_gain, k_gain, w_taps, b_dw, ln_g, ln_b, layer, i_even):
    n, t, d = x.shape
    n_in = w_in.shape[2]
    tt = ATTN_TIME_TILE
    tile = lambda width: pl.BlockSpec((None, tt, width), lambda b, i: (b, i, 0))
    even = lambda width: pl.BlockSpec((None, 1, width), lambda b, i: (i_even, 0, 0))
    return pl.pallas_call(
        _inproj_conv_prompt_body,
        grid=(n, t // tt),
        in_specs=[tile(d),
                  pl.BlockSpec((None, 1, d), lambda b, i: (layer, 0, 0)),
                  pl.BlockSpec((None, d, n_in), lambda b, i: (i_even, 0, 0), pipeline_mode=pl.Buffered(1)),
                  even(D_ATTN), even(D_KV),
                  pl.BlockSpec((None, CONV_WIDTH, 16, D_CONV), lambda b, i: (i_even, 0, 0, 0)),
                  even(D_CONV), even(D_CONV), even(D_CONV)],
        out_specs=[tile(D_CONV), tile(D_ATTN), tile(D_KV), tile(D_KV),
                   pl.BlockSpec((None, CONV_HALO, D_CONV), lambda b, i: (b, 0, 0))],
        out_shape=[jax.ShapeDtypeStruct((n, t, D_CONV), BF16), jax.ShapeDtypeStruct((n, t, D_ATTN), BF16),
                   jax.ShapeDtypeStruct((n, t, D_KV), F32), jax.ShapeDtypeStruct((n, t, D_KV), F32),
                   jax.ShapeDtypeStruct((n, CONV_HALO, D_CONV), F32)],
        scratch_shapes=[pltpu.VMEM((D_CONV // LANES, CONV_HALO + tt, LANES), F32),
                        pltpu.VMEM((D_CONV // LANES, tt, LANES), F32)],
        compiler_params=_params(2),
        name="inproj_conv_prompt",
    )(x, norm_g, w_in, q_gain, k_gain, w_taps, b_dw, ln_g, ln_b)


def _window_sums(vals, w, n_out):
    shared = list(range(n_out - 1, w))
    base = None
    for e in shared:
        base = vals[e] if base is None else base + vals[e]
    sums = []
    for i in range(n_out):
        acc = base
        for e in range(i, i + w):
            if e not in shared:
                acc = vals[e] if acc is None else acc + vals[e]
        sums.append(acc)
    return sums


def _pool_deltas_strided(ext_ref, d_ref, pos0, n_rows):
    n_slabs = ext_ref.shape[0]
    slabs_per_group = n_slabs // len(POOL_WINDOWS)
    rows = 8 * POOL_PHASES

    def chunk(c, carry):
        r0 = c * rows
        sub = lax.broadcasted_iota(jnp.int32, (8, LANES), 0)
        pos = [pos0 + r0 + ph + POOL_PHASES * sub for ph in range(POOL_PHASES)]
        inv = {w: [1.0 / jnp.minimum(w, p + 1).astype(F32) for p in pos] for w in POOL_WINDOWS}
        for slab in range(n_slabs):
            w = POOL_WINDOWS[slab // slabs_per_group]
            first = POOL_HALO - (w - 1)
            vals = [ext_ref[slab, pl.ds(r0 + first + e, 8, stride=POOL_PHASES), :] for e in range(w + POOL_PHASES - 1)]
            sums = _window_sums(vals, w, POOL_PHASES)
            for ph in range(POOL_PHASES):
                d_ref[slab, pl.ds(r0 + ph, 8, stride=POOL_PHASES), :] = sums[ph] * inv[w][ph] - vals[ph + w - 1]
        return carry

    lax.fori_loop(0, n_rows // rows, chunk, 0)


def _pool_ffn_prompt_body(x_ref, g_ref, w_ref, sc_ref, gf_ref, wg_ref, wu_ref, wd_ref, o_ref, tail_ref, ext_ref, d_ref):
    tt, d = x_ref.shape
    n_slabs = d // LANES
    slabs_per_group = n_slabs // len(POOL_WINDOWS)
    i = pl.program_id(1)

    @pl.when(i == 0)
    def _():
        ext_ref[:, 0:POOL_HALO, :] = jnp.zeros((n_slabs, POOL_HALO, LANES), F32)

    @pl.when(i > 0)
    def _():
        ext_ref[:, 0:POOL_HALO, :] = ext_ref[:, tt:tt + POOL_HALO, :]

    x = x_ref[...]
    xn = _rmsnorm(x, g_ref[...])
    for s in range(n_slabs):
        ext_ref[s, POOL_HALO:, :] = xn[:, s * LANES:(s + 1) * LANES]
    tail_ref[...] = xn[tt - POOL_HALO:, :]
    _pool_deltas_strided(ext_ref, d_ref, i * tt, tt)
    ys = []
    for g in range(len(POOL_WINDOWS)):
        dg = jnp.concatenate([d_ref[s] for s in range(g * slabs_per_group, (g + 1) * slabs_per_group)], axis=-1)
        ys.append(jnp.dot(dg.astype(BF16), w_ref[g], preferred_element_type=F32))
    x1 = x + jnp.concatenate(ys, axis=-1) * sc_ref[...]
    o_ref[...] = _ffn_apply(x1, gf_ref, wg_ref, wu_ref, wd_ref)


def _pool_ffn_prompt(x, norm_g, w_pool, pool_scale, layer, i_odd, norm_ffn, w_gate, w_up, w_down):
    n, t, d = x.shape
    tt = POOL_TIME_TILE
    ng, gd, _ = w_pool.shape[1:]
    tile = pl.BlockSpec((None, tt, d), lambda b, i: (b, i, 0))
    return pl.pallas_call(
        _pool_ffn_prompt_body,
        grid=(n, t // tt),
        in_specs=[tile,
                  pl.BlockSpec((None, 1, d), lambda b, i: (layer, 0, 0)),
                  pl.BlockSpec((None, ng, gd, gd), lambda b, i: (i_odd, 0, 0, 0)),
                  pl.BlockSpec((None, 1, d), lambda b, i: (i_odd, 0, 0))] + _ffn_specs(w_gate, layer, 2),
        out_specs=[tile, pl.BlockSpec((None, POOL_HALO, d), lambda b, i: (b, 0, 0))],
        out_shape=[jax.ShapeDtypeStruct((n, t, d), F32), jax.ShapeDtypeStruct((n, POOL_HALO, d), F32)],
        scratch_shapes=[pltpu.VMEM((d // LANES, POOL_HALO + tt, LANES), F32), pltpu.VMEM((d // LANES, tt, LANES), F32)],
        compiler_params=_params(2),
        name="pool_ffn_prompt",
    )(x, norm_g, w_pool, pool_scale, norm_ffn, w_gate, w_up, w_down)


def _state_alias(states, first_input, first_output):
    aliases = {first_input + j: first_output + j for j in range(len(states))}
    return list(states), [pl.BlockSpec(memory_space=pl.ANY)] * len(states), aliases


def _zeros_body(*o_refs):
    for o_ref in o_refs:
        o_ref[...] = jnp.zeros(o_ref.shape, o_ref.dtype)


def _zeros_like_all(arrays):
    steps = ZERO_FILL_STEPS
    views = [(a.size // a.shape[-1], a.shape[-1]) for a in arrays]
    outs = pl.pallas_call(
        _zeros_body,
        grid=(steps,),
        in_specs=[],
        out_specs=[pl.BlockSpec((rows // steps, cols), lambda i: (i, 0)) for rows, cols in views],
        out_shape=[jax.ShapeDtypeStruct(v, F32) for v in views],
        compiler_params=_params(1),
        name="zero_fill",
    )()
    return [o.reshape(a.shape) for o, a in zip(outs, arrays)]


def _conv_sample_body(cache_ref, glu_ref, w_ref, b_ref, g_ref, beta_ref, *refs):
    o_ref, new_ref = refs[-2:]
    hist = cache_ref.shape[0]
    t = glu_ref.shape[0]
    ext = lambda j: cache_ref[j] if j < hist else glu_ref[j - hist]
    rounded = lambda a: a.astype(BF16).astype(F32)
    for j in range(hist):
        new_ref[j] = ext(t + j)
    for step in range(t):
        acc = rounded(w_ref[0:1, :]) * rounded(ext(step)) + b_ref[...]
        for k in range(1, CONV_WIDTH):
            acc = acc + rounded(w_ref[k:k + 1, :]) * rounded(ext(step + k))
        o_ref[step] = _layernorm_silu(acc, g_ref[...], beta_ref[...]).astype(o_ref.dtype)


def _conv_sample(cache_tm, glu_tm, w_dw, b_dw, ln_g, ln_b, i_even, states):
    n_even, hist, n, c = cache_tm.shape
    t = glu_tm.shape[0]
    nb = SAMPLE_SEQ_TILE
    vec = pl.BlockSpec((None, 1, c), lambda b: (i_even, 0, 0))
    slab = pl.BlockSpec((None, hist, nb, c), lambda b: (i_even, 0, b, 0))
    alias_in, alias_specs, aliases = _state_alias(states, 6, 1)
    return pl.pallas_call(
        _conv_sample_body,
        grid=(n // nb,),
        in_specs=[slab,
                  pl.BlockSpec((t, nb, c), lambda b: (0, b, 0)),
                  pl.BlockSpec((None, CONV_WIDTH, c), lambda b: (i_even, 0, 0)),
                  vec, vec, vec] + alias_specs,
        out_specs=[pl.BlockSpec((t, nb, c), lambda b: (0, b, 0)), slab],
        out_shape=[jax.ShapeDtypeStruct((t, n, c), BF16), jax.ShapeDtypeStruct((n_even, hist, n, c), F32)],
        input_output_aliases=aliases,
        compiler_params=_params(1),
        name="conv_sample",
    )(cache_tm, glu_tm, w_dw, b_dw, ln_g, ln_b, *alias_in)


def _attn_sample_body(sink_ref, q_ref, kc_ref, kn_ref, vc_ref, vn_ref, *refs):
    o_ref, knew_ref, vnew_ref = refs[-3:]
    nb, nkv, rows, hd = q_ref.shape
    buf = kc_ref.shape[-1]
    t = kn_ref.shape[-1]
    batch = nb * nkv
    q = q_ref[...].reshape(batch, rows, hd)
    kc = kc_ref[...].reshape(batch, hd, buf)
    kn = kn_ref[...].reshape(batch, hd, t)
    vc = vc_ref[...].reshape(batch, hd, buf)
    vn = vn_ref[...].reshape(batch, hd, t)

    lane = lax.broadcasted_iota(jnp.int32, (1, 1, buf), 2)
    pad = jnp.zeros((batch, hd, buf - t), F32)
    k_new = jnp.where(lane < buf - t, pltpu.roll(kc, buf - t, 2), jnp.concatenate([pad, kn], axis=-1))
    v_new = jnp.where(lane < buf - t, pltpu.roll(vc, buf - t, 2), jnp.concatenate([pad, vn], axis=-1))
    knew_ref[...] = k_new.reshape(nb, nkv, hd, buf)
    vnew_ref[...] = v_new.reshape(nb, nkv, hd, buf)

    s_c = jnp.einsum("bqd,bdk->bqk", q, kc.astype(BF16), preferred_element_type=F32)
    s_n = jnp.einsum("bqd,bdk->bqk", q, kn.astype(BF16), preferred_element_type=F32)
    step = lax.broadcasted_iota(jnp.int32, (rows, 1), 0) % t
    head_in_group = lax.broadcasted_iota(jnp.int32, (rows, 1), 0) // t
    dist_c = buf + step - lax.broadcasted_iota(jnp.int32, (rows, buf), 1)
    dist_n = step - lax.broadcasted_iota(jnp.int32, (rows, t), 1)
    s_c = s_c.reshape(nb, nkv, rows, buf)
    s_n = s_n.reshape(nb, nkv, rows, t)
    ps_c, ps_n, invs = [], [], []
    for kv in range(nkv):
        slope = jnp.zeros((rows, 1), F32)
        sink = jnp.zeros((rows, 1), F32)
        for j in range(GQA_GROUP):
            slope = jnp.where(head_in_group == j, ALIBI_SLOPES[kv * GQA_GROUP + j], slope)
            sink = jnp.where(head_in_group == j, sink_ref[kv * GQA_GROUP + j], sink)
        sc = jnp.where((dist_c >= 0) & (dist_c < WINDOW), s_c[:, kv] - slope * dist_c.astype(F32), NEG_INF)
        sn = jnp.where((dist_n >= 0) & (dist_n < WINDOW), s_n[:, kv] - slope * dist_n.astype(F32), NEG_INF)
        m = jnp.maximum(jnp.maximum(jnp.max(sc, axis=-1, keepdims=True), jnp.max(sn, axis=-1, keepdims=True)), sink)
        pc = jnp.exp(sc - m)
        pn = jnp.exp(sn - m)
        denom = jnp.sum(pc, axis=-1, keepdims=True) + jnp.sum(pn, axis=-1, keepdims=True) + jnp.exp(sink - m)
        ps_c.append(pc.astype(BF16))
        ps_n.append(pn.astype(BF16))
        invs.append(1.0 / denom)
    p_c = jnp.stack(ps_c, axis=1).reshape(batch, rows, buf)
    p_n = jnp.stack(ps_n, axis=1).reshape(batch, rows, t)
    inv = jnp.stack(invs, axis=1).reshape(batch, rows, 1)
    o = jnp.einsum("bqk,bdk->bqd", p_c, vc.astype(BF16), preferred_element_type=F32)
    o = o + jnp.einsum("bqk,bdk->bqd", p_n, vn.astype(BF16), preferred_element_type=F32)
    o_ref[...] = (o * inv).reshape(nb, nkv, rows, hd).astype(o_ref.dtype)


def _attn_sample(q, kc, kn, vc, vn, sinks, i_even, states):
    n, nkv, rows, hd = q.shape
    n_even = kc.shape[0]
    buf = kc.shape[-1]
    t = kn.shape[-1]
    nb = SAMPLE_SEQ_TILE
    cache = pl.BlockSpec((None, nb, nkv, hd, buf), lambda b: (i_even, b, 0, 0, 0))
    new = pl.BlockSpec((nb, nkv, hd, t), lambda b: (b, 0, 0, 0))
    qspec = pl.BlockSpec((nb, nkv, rows, hd), lambda b: (b, 0, 0, 0))
    alias_in, alias_specs, aliases = _state_alias(states, 6, 1)
    state_shape = jax.ShapeDtypeStruct((n_even, n, nkv, hd, buf), F32)
    return pl.pallas_call(
        _attn_sample_body,
        grid=(n // nb,),
        in_specs=[pl.BlockSpec(memory_space=pltpu.SMEM), qspec, cache, new, cache, new] + alias_specs,
        out_specs=[qspec, cache, cache],
        out_shape=[jax.ShapeDtypeStruct((n, nkv, rows, hd), BF16), state_shape, state_shape],
        input_output_aliases=aliases,
        compiler_params=_params(1),
        name="attn_sample",
    )(sinks, q, kc, kn, vc, vn, *alias_in)


def _pool_ffn_sample_body(x_ref, st_ref, g_ref, w_ref, sc_ref, gf_ref, wg_ref, wu_ref, wd_ref, *refs, pos0):
    o_ref, new_ref = refs[-2:]
    t, nb, d = x_ref.shape
    hist = st_ref.shape[0]
    gdim = d // len(POOL_WINDOWS)
    x = x_ref[...].reshape(t * nb, d)
    xn = _rmsnorm(x, g_ref[...])
    ext = lambda j: st_ref[j] if j < hist else xn[(j - hist) * nb:(j - hist + 1) * nb]
    for j in range(hist):
        new_ref[j] = ext(t + j)
    ys = []
    for g, w in enumerate(POOL_WINDOWS):
        lanes = slice(g * gdim, (g + 1) * gdim)
        steps = []
        for step in range(t):
            acc = ext(hist + step)[:, lanes]
            for j in range(1, w):
                acc = acc + ext(hist + step - j)[:, lanes]
            cnt = float(min(w, pos0 + step + 1))
            steps.append(acc / cnt - xn[step * nb:(step + 1) * nb, lanes])
        dg = jnp.concatenate(steps, axis=0)
        ys.append(jnp.dot(dg.astype(BF16), w_ref[g], preferred_element_type=F32))
    x1 = x + jnp.concatenate(ys, axis=-1) * sc_ref[...]
    o_ref[...] = _ffn_apply(x1, gf_ref, wg_ref, wu_ref, wd_ref).reshape(t, nb, d)


def _pool_ffn_sample(x_tm, state_tm, norm_g, w_pool, pool_scale, layer, i_odd, pos0, norm_ffn, w_gate, w_up, w_down,
                     states):
    t, n, d = x_tm.shape
    n_odd, hist = state_tm.shape[:2]
    nb = SAMPLE_FFN_SEQ_TILE
    ng, gd, _ = w_pool.shape[1:]
    slab = pl.BlockSpec((None, hist, nb, d), lambda b: (i_odd, 0, b, 0))
    alias_in, alias_specs, aliases = _state_alias(states, 9, 1)
    return pl.pallas_call(
        functools.partial(_pool_ffn_sample_body, pos0=pos0),
        grid=(n // nb,),
        in_specs=[pl.BlockSpec((t, nb, d), lambda b: (0, b, 0)),
                  slab,
                  pl.BlockSpec((None, 1, d), lambda b: (layer, 0, 0)),
                  pl.BlockSpec((None, ng, gd, gd), lambda b: (i_odd, 0, 0, 0)),
                  pl.BlockSpec((None, 1, d), lambda b: (i_odd, 0, 0))] + _ffn_specs(w_gate, layer, 1) + alias_specs,
        out_specs=[pl.BlockSpec((t, nb, d), lambda b: (0, b, 0)), slab],
        out_shape=[jax.ShapeDtypeStruct((t, n, d), F32), jax.ShapeDtypeStruct((n_odd, hist, n, d), F32)],
        input_output_aliases=aliases,
        compiler_params=_params(1),
        name="pool_ffn_sample",
    )(x_tm, state_tm, norm_g, w_pool, pool_scale, norm_ffn, w_gate, w_up, w_down, *alias_in)


def _prepare_weights(norm_mix, norm_ffn, w_in, q_norm, k_norm, sinks, w_dw, b_dw, conv_norm_g, conv_norm_b, w_out,
                     w_pool, pool_scale, w_gate, w_up, w_down):
    vec = lambda a: a[:, None, :]
    return dict(
        norm_mix=vec(norm_mix), norm_ffn=vec(norm_ffn), w_in=w_in.astype(BF16),
        q_gain=vec(jnp.tile(q_norm, (1, N_HEADS))), k_gain=vec(jnp.tile(k_norm, (1, N_KV_HEADS))), sinks=sinks,
        w_dw=w_dw, b_dw=vec(b_dw),
        w_taps=jnp.broadcast_to(w_dw.astype(BF16)[:, :, None, :], w_dw.shape[:2] + (16, w_dw.shape[2])),
        ln_g=vec(conv_norm_g), ln_b=vec(conv_norm_b), w_out=w_out.astype(BF16),
        w_pool=w_pool.astype(BF16), pool_scale=vec(pool_scale),
        w_gate=w_gate.astype(BF16), w_up=w_up.astype(BF16), w_down=w_down.astype(BF16))


def _trunk_prompt(x, p, depth):
    n, t, d = x.shape
    xt = x.reshape(n * t, d)
    ffn = lambda layer: (p["norm_ffn"], p["w_gate"], p["w_up"], p["w_down"], layer)
    new_conv, new_k, new_v, new_pool = [], [], [], []
    for layer in range(depth):
        i = layer // 2
        if layer % 2 == 0:
            conv_out, q, k, v, glu_tail = _inproj_conv_prompt(
                xt.reshape(n, t, d), p["norm_mix"], p["w_in"], p["q_gain"], p["k_gain"], p["w_taps"], p["b_dw"],
                p["ln_g"], p["ln_b"], layer, i)
            attn = _attn_prompt(q, k, v, p["sinks"][i])
            xt = _outproj_ffn(xt, conv_out.reshape(n * t, -1), attn.reshape(n * t, -1), p["w_out"], i, *ffn(layer))
            keep = min(WINDOW, t)
            new_conv.append(glu_tail[:, CONV_HALO - (CONV_WIDTH - 1):])
            new_k.append(k[:, t - keep:].reshape(n, keep, N_KV_HEADS, HEAD_DIM))
            new_v.append(v[:, t - keep:].reshape(n, keep, N_KV_HEADS, HEAD_DIM))
        else:
            y, tail = _pool_ffn_prompt(xt.reshape(n, t, d), p["norm_mix"], p["w_pool"], p["pool_scale"], layer, i,
                                       *ffn(layer)[:-1])
            xt = y.reshape(n * t, d)
            new_pool.append(tail[:, POOL_HALO - POOL_BUF:])
    return xt.reshape(n, t, d), jnp.stack(new_conv), jnp.stack(new_k), jnp.stack(new_v), jnp.stack(new_pool)


def _trunk_sample(x, conv_bufs, k_bufs, v_bufs, pool_bufs, pos0, p, depth):
    n, t, d = x.shape
    x_tm = x.transpose(1, 0, 2)
    conv_tm = conv_bufs.transpose(0, 2, 1, 3)
    pool_tm = pool_bufs.transpose(0, 2, 1, 3)
    k_t = k_bufs.transpose(0, 1, 3, 4, 2)
    v_t = v_bufs.transpose(0, 1, 3, 4, 2)
    ffn = lambda layer: (p["norm_ffn"], p["w_gate"], p["w_up"], p["w_down"], layer)
    conv_new, k_new, v_new, pool_new = _zeros_like_all([conv_tm, k_t, v_t, pool_tm])
    kv_new = [k_new, v_new]
    for layer in range(depth):
        i = layer // 2
        if layer % 2 == 0:
            xt = x_tm.reshape(t * n, d)
            glu, q, k, v = _inproj(xt, p["norm_mix"], p["w_in"], p["q_gain"], p["k_gain"], layer, i)
            conv_out, conv_new = _conv_sample(conv_tm, glu.reshape(t, n, D_CONV), p["w_dw"], p["b_dw"], p["ln_g"],
                                              p["ln_b"], i, [conv_new])
            qh = q.reshape(t, n, N_KV_HEADS, GQA_GROUP, HEAD_DIM).transpose(1, 2, 3, 0, 4)
            qh = qh.reshape(n, N_KV_HEADS, GQA_GROUP * t, HEAD_DIM)
            kn = k.reshape(t, n, N_KV_HEADS, HEAD_DIM).transpose(1, 2, 3, 0)
            vn = v.reshape(t, n, N_KV_HEADS, HEAD_DIM).transpose(1, 2, 3, 0)
            oh, *kv_new = _attn_sample(qh, k_t, kn, v_t, vn, p["sinks"][i], i, kv_new)
            attn = oh.reshape(n, N_KV_HEADS, GQA_GROUP, t, HEAD_DIM).transpose(3, 0, 1, 2, 4).reshape(t * n, D_ATTN)
            xt = _outproj_ffn(xt, conv_out.reshape(t * n, D_CONV), attn, p["w_out"], i, *ffn(layer))
            x_tm = xt.reshape(t, n, d)
        else:
            x_tm, pool_new = _pool_ffn_sample(x_tm, pool_tm, p["norm_mix"], p["w_pool"], p["pool_scale"], layer, i, pos0,
                                              *ffn(layer)[:-1], [pool_new])
    return (x_tm.transpose(1, 0, 2), conv_new.transpose(0, 2, 1, 3), kv_new[0].transpose(0, 1, 4, 2, 3),
            kv_new[1].transpose(0, 1, 4, 2, 3), pool_new.transpose(0, 2, 1, 3))


def kernel(x_prompt, x_sample, cache_conv, cache_k, cache_v, state_pool, norm_mix, norm_ffn, w_in, q_norm, k_norm,
           sinks, w_dw, b_dw, conv_norm_g, conv_norm_b, w_out, w_pool, pool_scale, w_gate, w_up, w_down):
    depth = norm_mix.shape[0]
    p = _prepare_weights(norm_mix, norm_ffn, w_in, q_norm, k_norm, sinks, w_dw, b_dw, conv_norm_g, conv_norm_b, w_out,
                         w_pool, pool_scale, w_gate, w_up, w_down)
    y_p, conv_p, k_p, v_p, pool_p = _trunk_prompt(x_prompt, p, depth)
    y_s, conv_s, k_s, v_s, pool_s = _trunk_sample(x_sample, cache_conv, cache_k, cache_v, state_pool, PAST_LEN, p, depth)
    return (y_p, y_s, conv_p, k_p, v_p, pool_p, conv_s, k_s, v_s, pool_s)
```

```python
import functools

import jax
import jax.numpy as jnp
from jax import lax
from jax.experimental import pallas as pl
from jax.experimental.pallas import tpu as pltpu

F32 = jnp.float32
BF16 = jnp.bfloat16

HEAD_DIM = 64
N_HEADS = 8
N_KV_HEADS = 2
GQA_GROUP = N_HEADS // N_KV_HEADS
D_ATTN = N_HEADS * HEAD_DIM
D_KV = N_KV_HEADS * HEAD_DIM
WINDOW = 128
PAST_LEN = 8192
D_CONV = 512
CONV_WIDTH = 31
CONV_HALO = 32
POOL_WINDOWS = (2, 4, 8, 16)
POOL_BUF = max(POOL_WINDOWS) - 1
POOL_HALO = 16
RMS_EPS = 1e-6
LN_EPS = 1e-5
NEG_INF = -1e30
LANES = 128
ALIBI_SLOPES = tuple(2.0 ** (-8.0 * (h + 1) / N_HEADS) for h in range(N_HEADS))

TOKEN_TILE = 512
CONV_PHASES = 4
ATTN_TIME_TILE = 1024
POOL_TIME_TILE = 512
POOL_PHASES = 4
FFN_CHUNK = 1536
SAMPLE_SEQ_TILE = 32
SAMPLE_FFN_SEQ_TILE = 64
ZERO_FILL_STEPS = 8
VMEM_LIMIT = 56 * 1024 * 1024


def _params(n_axes):
    return pltpu.CompilerParams(dimension_semantics=("arbitrary",) * n_axes, vmem_limit_bytes=VMEM_LIMIT)


def _rmsnorm(x, g):
    return x * lax.rsqrt(jnp.mean(x * x, axis=-1, keepdims=True) + RMS_EPS) * g


def _head_rmsnorm(x, g):
    lane = lax.broadcasted_iota(jnp.int32, (x.shape[0], LANES), 1)
    low = lane < HEAD_DIM
    outs = []
    for s in range(x.shape[1] // LANES):
        xs = x[:, s * LANES:(s + 1) * LANES]
        sq = xs * xs
        s_low = jnp.sum(jnp.where(low, sq, 0.0), axis=-1, keepdims=True)
        s_high = jnp.sum(jnp.where(low, 0.0, sq), axis=-1, keepdims=True)
        ms = jnp.where(low, s_low, s_high) * (1.0 / HEAD_DIM)
        outs.append(xs * lax.rsqrt(ms + RMS_EPS) * g[:, s * LANES:(s + 1) * LANES])
    return outs[0] if len(outs) == 1 else jnp.concatenate(outs, axis=-1)


def _layernorm_silu(y, g, b):
    mu = jnp.mean(y, axis=-1, keepdims=True)
    yc = y - mu
    var = jnp.mean(yc * yc, axis=-1, keepdims=True)
    z = yc * lax.rsqrt(var + LN_EPS) * g + b
    return z * jax.nn.sigmoid(z)


def _ffn_apply(x, g_ref, wg_ref, wu_ref, wd_ref):
    h = _rmsnorm(x, g_ref[...]).astype(BF16)
    acc = x
    d_ff = wg_ref.shape[1]
    for c0 in range(0, d_ff, FFN_CHUNK):
        c1 = min(c0 + FFN_CHUNK, d_ff)
        gate = jnp.dot(h, wg_ref[:, c0:c1], preferred_element_type=F32)
        up = jnp.dot(h, wu_ref[:, c0:c1], preferred_element_type=F32)
        a = (gate * jax.nn.sigmoid(gate) * up).astype(BF16)
        acc = acc + jnp.dot(a, wd_ref[c0:c1, :], preferred_element_type=F32)
    return acc


def _ffn_specs(w_gate, layer, n_axes):
    _, d, f = w_gate.shape
    at_layer = {1: lambda i: (layer, 0, 0), 2: lambda b, i: (layer, 0, 0)}[n_axes]
    resident = functools.partial(pl.BlockSpec, pipeline_mode=pl.Buffered(1))
    return [pl.BlockSpec((None, 1, d), at_layer), resident((None, d, f), at_layer), resident((None, d, f), at_layer),
            resident((None, f, d), at_layer)]


def _inproj_body(x_ref, g_ref, w_ref, qn_ref, kn_ref, glu_ref, q_ref, k_ref, v_ref):
    h = _rmsnorm(x_ref[...], g_ref[...]).astype(BF16)
    o_q = 2 * D_CONV
    ua = jnp.dot(h, w_ref[:, o_q:], preferred_element_type=F32)
    uc = jnp.dot(h, w_ref[:, :o_q], preferred_element_type=F32)
    q_ref[...] = (_head_rmsnorm(ua[:, :D_ATTN], qn_ref[...]) * (HEAD_DIM ** -0.5)).astype(BF16)
    k_ref[...] = _head_rmsnorm(ua[:, D_ATTN:D_ATTN + D_KV], kn_ref[...])
    v_ref[...] = ua[:, D_ATTN + D_KV:]
    glu_ref[...] = uc[:, :D_CONV] * jax.nn.sigmoid(uc[:, D_CONV:])


def _inproj(x, norm_g, w_in, q_gain, k_gain, layer, i_even):
    t, d = x.shape
    n = w_in.shape[2]
    row = lambda width: pl.BlockSpec((TOKEN_TILE, width), lambda i: (i, 0))
    return pl.pallas_call(
        _inproj_body,
        grid=(t // TOKEN_TILE,),
        in_specs=[row(d),
                  pl.BlockSpec((None, 1, d), lambda i: (layer, 0, 0)),
                  pl.BlockSpec((None, d, n), lambda i: (i_even, 0, 0), pipeline_mode=pl.Buffered(1)),
                  pl.BlockSpec((None, 1, D_ATTN), lambda i: (i_even, 0, 0)),
                  pl.BlockSpec((None, 1, D_KV), lambda i: (i_even, 0, 0))],
        out_specs=[row(D_CONV), row(D_ATTN), row(D_KV), row(D_KV)],
        out_shape=[jax.ShapeDtypeStruct((t, D_CONV), F32), jax.ShapeDtypeStruct((t, D_ATTN), BF16),
                   jax.ShapeDtypeStruct((t, D_KV), F32), jax.ShapeDtypeStruct((t, D_KV), F32)],
        compiler_params=_params(1),
        name="inproj",
    )(x, norm_g, w_in, q_gain, k_gain)


def _outproj_ffn_body(xa_ref, ca_ref, aa_ref, xb_ref, cb_ref, ab_ref, w_ref, gf_ref, wg_ref, wu_ref, wd_ref,
                      oa_ref, ob_ref, *, steps_a):
    def tile(x_ref, c_ref, a_ref, o_ref):
        y = jnp.dot(c_ref[...], w_ref[:D_CONV, :], preferred_element_type=F32)
        y = y + jnp.dot(a_ref[...], w_ref[D_CONV:, :], preferred_element_type=F32)
        o_ref[...] = _ffn_apply(x_ref[...] + y, gf_ref, wg_ref, wu_ref, wd_ref)

    i = pl.program_id(0)

    @pl.when(i < steps_a)
    def _():
        tile(xa_ref, ca_ref, aa_ref, oa_ref)

    @pl.when(i >= steps_a)
    def _():
        tile(xb_ref, cb_ref, ab_ref, ob_ref)


def _outproj_ffn(group_a, group_b, w_out, i_even, norm_ffn, w_gate, w_up, w_down, layer):
    (xa, ca, aa), (xb, cb, ab) = group_a, group_b
    d = xa.shape[1]
    steps_a, steps_b = xa.shape[0] // TOKEN_TILE, xb.shape[0] // TOKEN_TILE
    row_a = lambda width: pl.BlockSpec((TOKEN_TILE, width), lambda i: (jnp.minimum(i, steps_a - 1), 0))
    row_b = lambda width: pl.BlockSpec((TOKEN_TILE, width), lambda i: (jnp.maximum(i - steps_a, 0), 0))
    return pl.pallas_call(
        functools.partial(_outproj_ffn_body, steps_a=steps_a),
        grid=(steps_a + steps_b,),
        in_specs=[row_a(d), row_a(D_CONV), row_a(D_ATTN), row_b(d), row_b(D_CONV), row_b(D_ATTN),
                  pl.BlockSpec((None, D_CONV + D_ATTN, d), lambda i: (i_even, 0, 0), pipeline_mode=pl.Buffered(1))]
        + _ffn_specs(w_gate, layer, 1),
        out_specs=[row_a(d), row_b(d)],
        out_shape=[jax.ShapeDtypeStruct(xa.shape, F32), jax.ShapeDtypeStruct(xb.shape, F32)],
        compiler_params=_params(1),
        name="outproj_ffn",
    )(xa, ca, aa, xb, cb, ab, w_out, norm_ffn, w_gate, w_up, w_down)


def _depthwise_conv_halves(ext_ref, y_ref, w_ref, b_ref, slab, half_rows):
    lead = CONV_HALO - (CONV_WIDTH - 1)
    rows = 8 * CONV_PHASES
    lanes = slice(slab * LANES, (slab + 1) * LANES)
    taps = [w_ref[k, :, lanes] for k in range(CONV_WIDTH)]

    def chunk(c, carry):
        r0 = c * rows
        acc = [jnp.zeros((16, LANES), F32) for _ in range(CONV_PHASES)]
        for e in range(CONV_WIDTH + CONV_PHASES - 1):
            v = jnp.concatenate([ext_ref[slab, pl.ds(r0 + lead + e, 8, stride=CONV_PHASES), :],
                                 ext_ref[slab, pl.ds(half_rows + r0 + lead + e, 8, stride=CONV_PHASES), :]],
                                axis=0).astype(BF16)
            for ph in range(CONV_PHASES):
                k = e - ph
                if 0 <= k < CONV_WIDTH:
                    acc[ph] = acc[ph] + v.astype(F32) * taps[k].astype(F32)
        bias = jnp.broadcast_to(b_ref[0:1, lanes], (16, LANES))
        for ph in range(CONV_PHASES):
            out = acc[ph] + bias
            y_ref[slab, pl.ds(r0 + ph, 8, stride=CONV_PHASES), :] = out[:8]
            y_ref[slab, pl.ds(half_rows + r0 + ph, 8, stride=CONV_PHASES), :] = out[8:]
        return carry

    lax.fori_loop(0, half_rows // rows, chunk, 0)


def _layernorm_silu_slabs(y_ref, g_ref, beta_ref, o_ref):
    n_slabs = y_ref.shape[0]
    ys = [y_ref[s] for s in range(n_slabs)]
    inv_c = 1.0 / (n_slabs * LANES)
    mu = jnp.sum(sum(ys), axis=-1, keepdims=True) * inv_c
    ycs = [y - mu for y in ys]
    var = jnp.sum(sum(yc * yc for yc in ycs), axis=-1, keepdims=True) * inv_c
    rstd = lax.rsqrt(var + LN_EPS)
    for s, yc in enumerate(ycs):
        lanes = slice(s * LANES, (s + 1) * LANES)
        z = yc * rstd * g_ref[:, lanes] + beta_ref[:, lanes]
        o_ref[:, lanes] = (z * jax.nn.sigmoid(z)).astype(o_ref.dtype)


def _banded_attention(sink_ref, q_ref, kcat, vcat, o_ref, bias_ref, sinkcol_ref, s_ref):
    b = pl.program_id(0)
    i = pl.program_id(1)
    w = WINDOW
    pair = 2 * HEAD_DIM
    n_blocks = q_ref.shape[0] // w
    lane = lax.broadcasted_iota(jnp.int32, (1, pair), 1)
    low = lane < HEAD_DIM

    @pl.when((b == 0) & (i == 0))
    def _():
        row = lax.broadcasted_iota(jnp.int32, (w, 2 * w), 0)
        col = lax.broadcasted_iota(jnp.int32, (w, 2 * w), 1)
        dist = w + row - col
        valid = (dist >= 0) & (dist < w)
        distf = dist.astype(F32)
        for h in range(N_HEADS):
            slab, half = h // 2, h % 2
            bias_ref[slab * w:(slab + 1) * w, half * 2 * w:(half + 1) * 2 * w] = (
                jnp.where(valid, -ALIBI_SLOPES[h] * distf, NEG_INF))
            sinkcol_ref[half, slab * w:(slab + 1) * w, :] = jnp.full((w, 1), sink_ref[h], F32)

    krot = pltpu.roll(kcat, HEAD_DIM, 1)
    vrot = pltpu.roll(vcat, HEAD_DIM, 1)
    zero = jnp.zeros((), F32)
    k_low = [jnp.where(low, kcat, zero).astype(BF16), jnp.where(low, krot, zero).astype(BF16)]
    k_high = [jnp.where(low, zero, krot).astype(BF16), jnp.where(low, zero, kcat).astype(BF16)]
    v_low = [jnp.where(low, vcat, zero).astype(BF16), jnp.where(low, vrot, zero).astype(BF16)]
    v_high = [jnp.where(low, zero, vrot).astype(BF16), jnp.where(low, zero, vcat).astype(BF16)]
    ones_low = jnp.broadcast_to(jnp.where(low, 1.0, 0.0).astype(BF16), (2 * w, pair))
    ones_high = jnp.broadcast_to(jnp.where(low, 0.0, 1.0).astype(BF16), (2 * w, pair))
    ones_ext = jnp.concatenate([ones_low, ones_high], axis=0)
    col = lax.broadcasted_iota(jnp.int32, (1, 4 * w), 1)
    no_prev = (col % (2 * w) < w) & (i == 0)

    for jb in range(n_blocks):
        rows = slice(jb * w, (jb + 1) * w)
        keys = slice(jb * w, (jb + 2) * w)
        for g in range(N_KV_HEADS):
            q2 = jnp.concatenate([q_ref[rows, (2 * g) * pair:(2 * g + 1) * pair],
                                  q_ref[rows, (2 * g + 1) * pair:(2 * g + 2) * pair]], axis=0)
            kk = jnp.concatenate([k_low[g][keys], k_high[g][keys]], axis=0)
            s_ref[jb, 2 * g * w:(2 * g + 2) * w, :] = lax.dot_general(
                q2, kk, (((1,), (1,)), ((), ())), preferred_element_type=F32)

    for jb in range(n_blocks):
        rows = slice(jb * w, (jb + 1) * w)
        keys = slice(jb * w, (jb + 2) * w)
        bias = bias_ref[...]
        if jb == 0:
            bias = jnp.where(no_prev, NEG_INF, bias)
        s = s_ref[jb] + bias
        ps, sink_terms = [], []
        for half in range(2):
            sh = s[:, half * 2 * w:(half + 1) * 2 * w]
            sink = sinkcol_ref[half]
            m = jnp.maximum(jnp.max(sh, axis=-1, keepdims=True), sink)
            ps.append(jnp.exp(sh - m).astype(BF16))
            sink_terms.append(jnp.exp(sink - m))
        p = jnp.concatenate(ps, axis=1)
        sink_term = jnp.where(low, sink_terms[0], sink_terms[1])
        for g in range(N_KV_HEADS):
            vv = jnp.concatenate([jnp.concatenate([v_low[g][keys], v_high[g][keys]], axis=0), ones_ext], axis=1)
            o = jnp.dot(p[2 * g * w:(2 * g + 2) * w], vv, preferred_element_type=F32)
            o = o[:, :pair] / (o[:, pair:] + sink_term[2 * g * w:(2 * g + 2) * w])
            o_ref[rows, (2 * g) * pair:(2 * g + 1) * pair] = o[:w].astype(o_ref.dtype)
            o_ref[rows, (2 * g + 1) * pair:(2 * g + 2) * pair] = o[w:].astype(o_ref.dtype)


def _attn_prompt_body(sink_ref, q_ref, kc_ref, kp_ref, vc_ref, vp_ref, o_ref, bias_ref, sinkcol_ref, s_ref):
    kcat = jnp.concatenate([kp_ref[...], kc_ref[...]], axis=0)
    vcat = jnp.concatenate([vp_ref[...], vc_ref[...]], axis=0)
    _banded_attention(sink_ref, q_ref, kcat, vcat, o_ref, bias_ref, sinkcol_ref, s_ref)


def _attn_prompt(q, k, v, sinks):
    n, t, _ = q.shape
    tt = ATTN_TIME_TILE
    cur = lambda width: pl.BlockSpec((None, tt, width), lambda b, i: (b, i, 0))
    prev = lambda width: pl.BlockSpec((None, WINDOW, width),
                                      lambda b, i: (b, jnp.maximum(i * (tt // WINDOW) - 1, 0), 0))
    return pl.pallas_call(
        _attn_prompt_body,
        grid=(n, t // tt),
        in_specs=[pl.BlockSpec(memory_space=pltpu.SMEM), cur(D_ATTN), cur(D_KV), prev(D_KV), cur(D_KV), prev(D_KV)],
        out_specs=cur(D_ATTN),
        out_shape=jax.ShapeDtypeStruct((n, t, D_ATTN), BF16),
        scratch_shapes=[pltpu.VMEM((N_HEADS // 2 * WINDOW, 4 * WINDOW), F32),
                        pltpu.VMEM((2, N_HEADS // 2 * WINDOW, 1), F32),
                        pltpu.VMEM((tt // WINDOW, N_HEADS // 2 * WINDOW, 4 * WINDOW), F32)],
        compiler_params=_params(2),
        name="attn_prompt",
    )(sinks, q, k, k, v, v)


def _inproj_conv_prompt_body(x_ref, g_ref, w_ref, qn_ref, kn_ref, wt_ref, b_ref, lg_ref, lb_ref,
                             conv_ref, q_ref, k_ref, v_ref, glu_tail_ref, ext_ref, y_ref):
    tt = x_ref.shape[0]
    n_slabs = D_CONV // LANES
    i = pl.program_id(1)

    @pl.when(i == 0)
    def _():
        ext_ref[:, 0:CONV_HALO, :] = jnp.zeros((n_slabs, CONV_HALO, LANES), F32)

    @pl.when(i > 0)
    def _():
        ext_ref[:, 0:CONV_HALO, :] = ext_ref[:, tt:tt + CONV_HALO, :]

    h = _rmsnorm(x_ref[...], g_ref[...]).astype(BF16)
    o_q = 2 * D_CONV
    ua = jnp.dot(h, w_ref[:, o_q:], preferred_element_type=F32)
    uc = jnp.dot(h, w_ref[:, :o_q], preferred_element_type=F32)
    q_ref[...] = (_head_rmsnorm(ua[:, :D_ATTN], qn_ref[...]) * (HEAD_DIM ** -0.5)).astype(BF16)
    k_ref[...] = _head_rmsnorm(ua[:, D_ATTN:D_ATTN + D_KV], kn_ref[...])
    v_ref[...] = ua[:, D_ATTN + D_KV:]
    glu = uc[:, :D_CONV] * jax.nn.sigmoid(uc[:, D_CONV:])
    glu_tail_ref[...] = glu[tt - CONV_HALO:, :]
    for s in range(n_slabs):
        ext_ref[s, CONV_HALO:, :] = glu[:, s * LANES:(s + 1) * LANES]
    for s in range(n_slabs):
        _depthwise_conv_halves(ext_ref, y_ref, wt_ref, b_ref, s, tt // 2)
    _layernorm_silu_slabs(y_ref, lg_ref, lb_ref, conv_ref)


def _inproj_conv_prompt(x, norm_g, w_in, q_gain, k_gain, w_taps, b_dw, ln_g, ln_b, layer, i_even):
    n, t, d = x.shape
    n_in = w_in.shape[2]
    tt = ATTN_TIME_TILE
    tile = lambda width: pl.BlockSpec((None, tt, width), lambda b, i: (b, i, 0))
    even = lambda width: pl.BlockSpec((None, 1, width), lambda b, i: (i_even, 0, 0))
    return pl.pallas_call(
        _inproj_conv_prompt_body,
        grid=(n, t // tt),
        in_specs=[tile(d),
                  pl.BlockSpec((None, 1, d), lambda b, i: (layer, 0, 0)),
                  pl.BlockSpec((None, d, n_in), lambda b, i: (i_even, 0, 0), pipeline_mode=pl.Buffered(1)),
                  even(D_ATTN), even(D_KV),
                  pl.BlockSpec((None, CONV_WIDTH, 16, D_CONV), lambda b, i: (i_even, 0, 0, 0)),
                  even(D_CONV), even(D_CONV), even(D_CONV)],
        out_specs=[tile(D_CONV), tile(D_ATTN), tile(D_KV), tile(D_KV),
                   pl.BlockSpec((None, CONV_HALO, D_CONV), lambda b, i: (b, 0, 0))],
        out_shape=[jax.ShapeDtypeStruct((n, t, D_CONV), BF16), jax.ShapeDtypeStruct((n, t, D_ATTN), BF16),
                   jax.ShapeDtypeStruct((n, t, D_KV), F32), jax.ShapeDtypeStruct((n, t, D_KV), F32),
                   jax.ShapeDtypeStruct((n, CONV_HALO, D_CONV), F32)],
        scratch_shapes=[pltpu.VMEM((D_CONV // LANES, CONV_HALO + tt, LANES), F32),
                        pltpu.VMEM((D_CONV // LANES, tt, LANES), F32)],
        compiler_params=_params(2),
        name="inproj_conv_prompt",
    )(x, norm_g, w_in, q_gain, k_gain, w_taps, b_dw, ln_g, ln_b)


def _window_sums(vals, w, n_out):
    shared = list(range(n_out - 1, w))
    base = None
    for e in shared:
        base = vals[e] if base is None else base + vals[e]
    sums = []
    for i in range(n_out):
        acc = base
        for e in range(i, i + w):
            if e not in shared:
                acc = vals[e] if acc is None else acc + vals[e]
        sums.append(acc)
    return sums


def _pool_deltas_strided(ext_ref, d_ref, pos0, n_rows):
    n_slabs = ext_ref.shape[0]
    slabs_per_group = n_slabs // len(POOL_WINDOWS)
    rows = 8 * POOL_PHASES

    def chunk(c, carry):
        r0 = c * rows
        sub = lax.broadcasted_iota(jnp.int32, (8, LANES), 0)
        pos = [pos0 + r0 + ph + POOL_PHASES * sub for ph in range(POOL_PHASES)]
        inv = {w: [1.0 / jnp.minimum(w, p + 1).astype(F32) for p in pos] for w in POOL_WINDOWS}
        for slab in range(n_slabs):
            w = POOL_WINDOWS[slab // slabs_per_group]
            first = POOL_HALO - (w - 1)
            vals = [ext_ref[slab, pl.ds(r0 + first + e, 8, stride=POOL_PHASES), :] for e in range(w + POOL_PHASES - 1)]
            sums = _window_sums(vals, w, POOL_PHASES)
            for ph in range(POOL_PHASES):
                d_ref[slab, pl.ds(r0 + ph, 8, stride=POOL_PHASES), :] = sums[ph] * inv[w][ph] - vals[ph + w - 1]
        return carry

    lax.fori_loop(0, n_rows // rows, chunk, 0)


def _pool_ffn_prompt_body(x_ref, g_ref, w_ref, sc_ref, gf_ref, wg_ref, wu_ref, wd_ref, o_ref, tail_ref, ext_ref, d_ref):
    tt, d = x_ref.shape
    n_slabs = d // LANES
    slabs_per_group = n_slabs // len(POOL_WINDOWS)
    i = pl.program_id(1)

    @pl.when(i == 0)
    def _():
        ext_ref[:, 0:POOL_HALO, :] = jnp.zeros((n_slabs, POOL_HALO, LANES), F32)

    @pl.when(i > 0)
    def _():
        ext_ref[:, 0:POOL_HALO, :] = ext_ref[:, tt:tt + POOL_HALO, :]

    x = x_ref[...]
    xn = _rmsnorm(x, g_ref[...])
    for s in range(n_slabs):
        ext_ref[s, POOL_HALO:, :] = xn[:, s * LANES:(s + 1) * LANES]
    tail_ref[...] = xn[tt - POOL_HALO:, :]
    _pool_deltas_strided(ext_ref, d_ref, i * tt, tt)
    ys = []
    for g in range(len(POOL_WINDOWS)):
        dg = jnp.concatenate([d_ref[s] for s in range(g * slabs_per_group, (g + 1) * slabs_per_group)], axis=-1)
        ys.append(jnp.dot(dg.astype(BF16), w_ref[g], preferred_element_type=F32))
    x1 = x + jnp.concatenate(ys, axis=-1) * sc_ref[...]
    o_ref[...] = _ffn_apply(x1, gf_ref, wg_ref, wu_ref, wd_ref)


def _pool_ffn_prompt(x, norm_g, w_pool, pool_scale, layer, i_odd, norm_ffn, w_gate, w_up, w_down):
    n, t, d = x.shape
    tt = POOL_TIME_TILE
    ng, gd, _ = w_pool.shape[1:]
    tile = pl.BlockSpec((None, tt, d), lambda b, i: (b, i, 0))
    return pl.pallas_call(
        _pool_ffn_prompt_body,
        grid=(n, t // tt),
        in_specs=[tile,
                  pl.BlockSpec((None, 1, d), lambda b, i: (layer, 0, 0)),
                  pl.BlockSpec((None, ng, gd, gd), lambda b, i: (i_odd, 0, 0, 0)),
                  pl.BlockSpec((None, 1, d), lambda b, i: (i_odd, 0, 0))] + _ffn_specs(w_gate, layer, 2),
        out_specs=[tile, pl.BlockSpec((None, POOL_HALO, d), lambda b, i: (b, 0, 0))],
        out_shape=[jax.ShapeDtypeStruct((n, t, d), F32), jax.ShapeDtypeStruct((n, POOL_HALO, d), F32)],
        scratch_shapes=[pltpu.VMEM((d // LANES, POOL_HALO + tt, LANES), F32), pltpu.VMEM((d // LANES, tt, LANES), F32)],
        compiler_params=_params(2),
        name="pool_ffn_prompt",
    )(x, norm_g, w_pool, pool_scale, norm_ffn, w_gate, w_up, w_down)


def _state_alias(states, first_input, first_output):
    aliases = {first_input + j: first_output + j for j in range(len(states))}
    return list(states), [pl.BlockSpec(memory_space=pl.ANY)] * len(states), aliases


def _zeros_body(*o_refs):
    for o_ref in o_refs:
        o_ref[...] = jnp.zeros(o_ref.shape, o_ref.dtype)


def _zeros_like_all(arrays):
    steps = ZERO_FILL_STEPS
    views = [(a.size // a.shape[-1], a.shape[-1]) for a in arrays]
    outs = pl.pallas_call(
        _zeros_body,
        grid=(steps,),
        in_specs=[],
        out_specs=[pl.BlockSpec((rows // steps, cols), lambda i: (i, 0)) for rows, cols in views],
        out_shape=[jax.ShapeDtypeStruct(v, F32) for v in views],
        compiler_params=_params(1),
        name="zero_fill",
    )()
    return [o.reshape(a.shape) for o, a in zip(outs, arrays)]


def _conv_sample_body(cache_ref, glu_ref, w_ref, b_ref, g_ref, beta_ref, *refs):
    o_ref, new_ref = refs[-2:]
    hist = cache_ref.shape[0]
    t = glu_ref.shape[0]
    ext = lambda j: cache_ref[j] if j < hist else glu_ref[j - hist]
    rounded = lambda a: a.astype(BF16).astype(F32)
    for j in range(hist):
        new_ref[j] = ext(t + j)
    for step in range(t):
        acc = rounded(w_ref[0:1, :]) * rounded(ext(step)) + b_ref[...]
        for k in range(1, CONV_WIDTH):
            acc = acc + rounded(w_ref[k:k + 1, :]) * rounded(ext(step + k))
        o_ref[step] = _layernorm_silu(acc, g_ref[...], beta_ref[...]).astype(o_ref.dtype)


def _conv_sample(cache_tm, glu_tm, w_dw, b_dw, ln_g, ln_b, i_even, states):
    n_even, hist, n, c = cache_tm.shape
    t = glu_tm.shape[0]
    nb = SAMPLE_SEQ_TILE
    vec = pl.BlockSpec((None, 1, c), lambda b: (i_even, 0, 0))
    slab = pl.BlockSpec((None, hist, nb, c), lambda b: (i_even, 0, b, 0))
    alias_in, alias_specs, aliases = _state_alias(states, 6, 1)
    return pl.pallas_call(
        _conv_sample_body,
        grid=(n // nb,),
        in_specs=[slab,
                  pl.BlockSpec((t, nb, c), lambda b: (0, b, 0)),
                  pl.BlockSpec((None, CONV_WIDTH, c), lambda b: (i_even, 0, 0)),
                  vec, vec, vec] + alias_specs,
        out_specs=[pl.BlockSpec((t, nb, c), lambda b: (0, b, 0)), slab],
        out_shape=[jax.ShapeDtypeStruct((t, n, c), BF16), jax.ShapeDtypeStruct((n_even, hist, n, c), F32)],
        input_output_aliases=aliases,
        compiler_params=_params(1),
        name="conv_sample",
    )(cache_tm, glu_tm, w_dw, b_dw, ln_g, ln_b, *alias_in)


def _attn_sample_body(sink_ref, q_ref, kc_ref, kn_ref, vc_ref, vn_ref, *refs):
    o_ref, knew_ref, vnew_ref = refs[-3:]
    nb, nkv, rows, hd = q_ref.shape
    buf = kc_ref.shape[-1]
    t = kn_ref.shape[-1]
    batch = nb * nkv
    q = q_ref[...].reshape(batch, rows, hd)
    kc = kc_ref[...].reshape(batch, hd, buf)
    kn = kn_ref[...].reshape(batch, hd, t)
    vc = vc_ref[...].reshape(batch, hd, buf)
    vn = vn_ref[...].reshape(batch, hd, t)

    lane = lax.broadcasted_iota(jnp.int32, (1, 1, buf), 2)
    pad = jnp.zeros((batch, hd, buf - t), F32)
    k_new = jnp.where(lane < buf - t, pltpu.roll(kc, buf - t, 2), jnp.concatenate([pad, kn], axis=-1))
    v_new = jnp.where(lane < buf - t, pltpu.roll(vc, buf - t, 2), jnp.concatenate([pad, vn], axis=-1))
    knew_ref[...] = k_new.reshape(nb, nkv, hd, buf)
    vnew_ref[...] = v_new.reshape(nb, nkv, hd, buf)

    s_c = jnp.einsum("bqd,bdk->bqk", q, kc.astype(BF16), preferred_element_type=F32)
    s_n = jnp.einsum("bqd,bdk->bqk", q, kn.astype(BF16), preferred_element_type=F32)
    step = lax.broadcasted_iota(jnp.int32, (rows, 1), 0) % t
    head_in_group = lax.broadcasted_iota(jnp.int32, (rows, 1), 0) // t
    dist_c = buf + step - lax.broadcasted_iota(jnp.int32, (rows, buf), 1)
    dist_n = step - lax.broadcasted_iota(jnp.int32, (rows, t), 1)
    s_c = s_c.reshape(nb, nkv, rows, buf)
    s_n = s_n.reshape(nb, nkv, rows, t)
    ps_c, ps_n, invs = [], [], []
    for kv in range(nkv):
        slope = jnp.zeros((rows, 1), F32)
        sink = jnp.zeros((rows, 1), F32)
        for j in range(GQA_GROUP):
            slope = jnp.where(head_in_group == j, ALIBI_SLOPES[kv * GQA_GROUP + j], slope)
            sink = jnp.where(head_in_group == j, sink_ref[kv * GQA_GROUP + j], sink)
        sc = jnp.where((dist_c >= 0) & (dist_c < WINDOW), s_c[:, kv] - slope * dist_c.astype(F32), NEG_INF)
        sn = jnp.where((dist_n >= 0) & (dist_n < WINDOW), s_n[:, kv] - slope * dist_n.astype(F32), NEG_INF)
        m = jnp.maximum(jnp.maximum(jnp.max(sc, axis=-1, keepdims=True), jnp.max(sn, axis=-1, keepdims=True)), sink)
        pc = jnp.exp(sc - m)
        pn = jnp.exp(sn - m)
        denom = jnp.sum(pc, axis=-1, keepdims=True) + jnp.sum(pn, axis=-1, keepdims=True) + jnp.exp(sink - m)
        ps_c.append(pc.astype(BF16))
        ps_n.append(pn.astype(BF16))
        invs.append(1.0 / denom)
    p_c = jnp.stack(ps_c, axis=1).reshape(batch, rows, buf)
    p_n = jnp.stack(ps_n, axis=1).reshape(batch, rows, t)
    inv = jnp.stack(invs, axis=1).reshape(batch, rows, 1)
    o = jnp.einsum("bqk,bdk->bqd", p_c, vc.astype(BF16), preferred_element_type=F32)
    o = o + jnp.einsum("bqk,bdk->bqd", p_n, vn.astype(BF16), preferred_element_type=F32)
    o_ref[...] = (o * inv).reshape(nb, nkv, rows, hd).astype(o_ref.dtype)


def _attn_sample(q, kc, kn, vc, vn, sinks, i_even, states):
    n, nkv, rows, hd = q.shape
    n_even = kc.shape[0]
    buf = kc.shape[-1]
    t = kn.shape[-1]
    nb = SAMPLE_SEQ_TILE
    cache = pl.BlockSpec((None, nb, nkv, hd, buf), lambda b: (i_even, b, 0, 0, 0))
    new = pl.BlockSpec((nb, nkv, hd, t), lambda b: (b, 0, 0, 0))
    qspec = pl.BlockSpec((nb, nkv, rows, hd), lambda b: (b, 0, 0, 0))
    alias_in, alias_specs, aliases = _state_alias(states, 6, 1)
    state_shape = jax.ShapeDtypeStruct((n_even, n, nkv, hd, buf), F32)
    return pl.pallas_call(
        _attn_sample_body,
        grid=(n // nb,),
        in_specs=[pl.BlockSpec(memory_space=pltpu.SMEM), qspec, cache, new, cache, new] + alias_specs,
        out_specs=[qspec, cache, cache],
        out_shape=[jax.ShapeDtypeStruct((n, nkv, rows, hd), BF16), state_shape, state_shape],
        input_output_aliases=aliases,
        compiler_params=_params(1),
        name="attn_sample",
    )(sinks, q, kc, kn, vc, vn, *alias_in)


def _pool_ffn_sample_body(x_ref, st_ref, g_ref, w_ref, sc_ref, gf_ref, wg_ref, wu_ref, wd_ref, *refs, pos0):
    o_ref, new_ref = refs[-2:]
    t, nb, d = x_ref.shape
    hist = st_ref.shape[0]
    gdim = d // len(POOL_WINDOWS)
    x = x_ref[...].reshape(t * nb, d)
    xn = _rmsnorm(x, g_ref[...])
    ext = lambda j: st_ref[j] if j < hist else xn[(j - hist) * nb:(j - hist + 1) * nb]
    for j in range(hist):
        new_ref[j] = ext(t + j)
    ys = []
    for g, w in enumerate(POOL_WINDOWS):
        lanes = slice(g * gdim, (g + 1) * gdim)
        steps = []
        for step in range(t):
            acc = ext(hist + step)[:, lanes]
            for j in range(1, w):
                acc = acc + ext(hist + step - j)[:, lanes]
            cnt = float(min(w, pos0 + step + 1))
            steps.append(acc / cnt - xn[step * nb:(step + 1) * nb, lanes])
        dg = jnp.concatenate(steps, axis=0)
        ys.append(jnp.dot(dg.astype(BF16), w_ref[g], preferred_element_type=F32))
    x1 = x + jnp.concatenate(ys, axis=-1) * sc_ref[...]
    o_ref[...] = _ffn_apply(x1, gf_ref, wg_ref, wu_ref, wd_ref).reshape(t, nb, d)


def _pool_ffn_sample(x_tm, state_tm, norm_g, w_pool, pool_scale, layer, i_odd, pos0, norm_ffn, w_gate, w_up, w_down,
                     states):
    t, n, d = x_tm.shape
    n_odd, hist = state_tm.shape[:2]
    nb = SAMPLE_FFN_SEQ_TILE
    ng, gd, _ = w_pool.shape[1:]
    slab = pl.BlockSpec((None, hist, nb, d), lambda b: (i_odd, 0, b, 0))
    alias_in, alias_specs, aliases = _state_alias(states, 9, 1)
    return pl.pallas_call(
        functools.partial(_pool_ffn_sample_body, pos0=pos0),
        grid=(n // nb,),
        in_specs=[pl.BlockSpec((t, nb, d), lambda b: (0, b, 0)),
                  slab,
                  pl.BlockSpec((None, 1, d), lambda b: (layer, 0, 0)),
                  pl.BlockSpec((None, ng, gd, gd), lambda b: (i_odd, 0, 0, 0)),
                  pl.BlockSpec((None, 1, d), lambda b: (i_odd, 0, 0))] + _ffn_specs(w_gate, layer, 1) + alias_specs,
        out_specs=[pl.BlockSpec((t, nb, d), lambda b: (0, b, 0)), slab],
        out_shape=[jax.ShapeDtypeStruct((t, n, d), F32), jax.ShapeDtypeStruct((n_odd, hist, n, d), F32)],
        input_output_aliases=aliases,
        compiler_params=_params(1),
        name="pool_ffn_sample",
    )(x_tm, state_tm, norm_g, w_pool, pool_scale, norm_ffn, w_gate, w_up, w_down, *alias_in)


def _prepare_weights(norm_mix, norm_ffn, w_in, q_norm, k_norm, sinks, w_dw, b_dw, conv_norm_g, conv_norm_b, w_out,
                     w_pool, pool_scale, w_gate, w_up, w_down):
    vec = lambda a: a[:, None, :]
    return dict(
        norm_mix=vec(norm_mix), norm_ffn=vec(norm_ffn), w_in=w_in.astype(BF16),
        q_gain=vec(jnp.tile(q_norm, (1, N_HEADS))), k_gain=vec(jnp.tile(k_norm, (1, N_KV_HEADS))), sinks=sinks,
        w_dw=w_dw, b_dw=vec(b_dw),
        w_taps=jnp.broadcast_to(w_dw.astype(BF16)[:, :, None, :], w_dw.shape[:2] + (16, w_dw.shape[2])),
        ln_g=vec(conv_norm_g), ln_b=vec(conv_norm_b), w_out=w_out.astype(BF16),
        w_pool=w_pool.astype(BF16), pool_scale=vec(pool_scale),
        w_gate=w_gate.astype(BF16), w_up=w_up.astype(BF16), w_down=w_down.astype(BF16))


def _trunks(x_prompt, x_sample, conv_bufs, k_bufs, v_bufs, pool_bufs, pos0, p, depth):
    n, t, d = x_prompt.shape
    ns, ts, _ = x_sample.shape
    xp = x_prompt.reshape(n * t, d)
    x_tm = x_sample.transpose(1, 0, 2)
    conv_tm = conv_bufs.transpose(0, 2, 1, 3)
    pool_tm = pool_bufs.transpose(0, 2, 1, 3)
    k_t = k_bufs.transpose(0, 1, 3, 4, 2)
    v_t = v_bufs.transpose(0, 1, 3, 4, 2)
    ffn = lambda layer: (p["norm_ffn"], p["w_gate"], p["w_up"], p["w_down"], layer)
    new_conv, new_k, new_v, new_pool = [], [], [], []
    conv_new, k_new, v_new, pool_new = _zeros_like_all([conv_tm, k_t, v_t, pool_tm])
    kv_new = [k_new, v_new]
    for layer in range(depth):
        i = layer // 2
        if layer % 2 == 0:
            conv_p, q, k, v, glu_tail = _inproj_conv_prompt(
                xp.reshape(n, t, d), p["norm_mix"], p["w_in"], p["q_gain"], p["k_gain"], p["w_taps"], p["b_dw"],
                p["ln_g"], p["ln_b"], layer, i)
            attn_p = _attn_prompt(q, k, v, p["sinks"][i])
            keep = min(WINDOW, t)
            new_conv.append(glu_tail[:, CONV_HALO - (CONV_WIDTH - 1):])
            new_k.append(k[:, t - keep:].reshape(n, keep, N_KV_HEADS, HEAD_DIM))
            new_v.append(v[:, t - keep:].reshape(n, keep, N_KV_HEADS, HEAD_DIM))

            xs = x_tm.reshape(ts * ns, d)
            glu, q, k, v = _inproj(xs, p["norm_mix"], p["w_in"], p["q_gain"], p["k_gain"], layer, i)
            conv_s, conv_new = _conv_sample(conv_tm, glu.reshape(ts, ns, D_CONV), p["w_dw"], p["b_dw"], p["ln_g"],
                                            p["ln_b"], i, [conv_new])
            qh = q.reshape(ts, ns, N_KV_HEADS, GQA_GROUP, HEAD_DIM).transpose(1, 2, 3, 0, 4)
            qh = qh.reshape(ns, N_KV_HEADS, GQA_GROUP * ts, HEAD_DIM)
            kn = k.reshape(ts, ns, N_KV_HEADS, HEAD_DIM).transpose(1, 2, 3, 0)
            vn = v.reshape(ts, ns, N_KV_HEADS, HEAD_DIM).transpose(1, 2, 3, 0)
            oh, *kv_new = _attn_sample(qh, k_t, kn, v_t, vn, p["sinks"][i], i, kv_new)
            attn_s = oh.reshape(ns, N_KV_HEADS, GQA_GROUP, ts, HEAD_DIM).transpose(3, 0, 1, 2, 4)

            xp, xs = _outproj_ffn((xp, conv_p.reshape(n * t, -1), attn_p.reshape(n * t, -1)),
                                  (xs, conv_s.reshape(ts * ns, D_CONV), attn_s.reshape(ts * ns, D_ATTN)),
                                  p["w_out"], i, *ffn(layer))
            x_tm = xs.reshape(ts, ns, d)
        else:
            y, tail = _pool_ffn_prompt(xp.reshape(n, t, d), p["norm_mix"], p["w_pool"], p["pool_scale"], layer, i,
                                       *ffn(layer)[:-1])
            xp = y.reshape(n * t, d)
            new_pool.append(tail[:, POOL_HALO - POOL_BUF:])
            x_tm, pool_new = _pool_ffn_sample(x_tm, pool_tm, p["norm_mix"], p["w_pool"], p["pool_scale"], layer, i, pos0,
                                              *ffn(layer)[:-1], [pool_new])
    return (xp.reshape(n, t, d), x_tm.transpose(1, 0, 2),
            jnp.stack(new_conv), jnp.stack(new_k), jnp.stack(new_v), jnp.stack(new_pool),
            conv_new.transpose(0, 2, 1, 3), kv_new[0].transpose(0, 1, 4, 2, 3), kv_new[1].transpose(0, 1, 4, 2, 3),
            pool_new.transpose(0, 2, 1, 3))


def kernel(x_prompt, x_sample, cache_conv, cache_k, cache_v, state_pool, norm_mix, norm_ffn, w_in, q_norm, k_norm,
           sinks, w_dw, b_dw, conv_norm_g, conv_norm_b, w_out, w_pool, pool_scale, w_gate, w_up, w_down):
    depth = norm_mix.shape[0]
    p = _prepare_weights(norm_mix, norm_ffn, w_in, q_norm, k_norm, sinks, w_dw, b_dw, conv_norm_g, conv_norm_b, w_out,
                         w_pool, pool_scale, w_gate, w_up, w_down)
    return _trunks(x_prompt, x_sample, cache_conv, cache_k, cache_v, state_pool, PAST_LEN, p, depth)
```

```python
import functools

import jax
import jax.numpy as jnp
from jax import lax
from jax.experimental import pallas as pl
from jax.experimental.pallas import tpu as pltpu

F32 = jnp.float32
BF16 = jnp.bfloat16

HEAD_DIM = 64
N_HEADS = 8
N_KV_HEADS = 2
GQA_GROUP = N_HEADS // N_KV_HEADS
D_ATTN = N_HEADS * HEAD_DIM
D_KV = N_KV_HEADS * HEAD_DIM
WINDOW = 128
PAST_LEN = 8192
D_CONV = 512
CONV_WIDTH = 31
CONV_HALO = 32
POOL_WINDOWS = (2, 4, 8, 16)
POOL_BUF = max(POOL_WINDOWS) - 1
POOL_HALO = 16
RMS_EPS = 1e-6
LN_EPS = 1e-5
NEG_INF = -1e30
LANES = 128
ALIBI_SLOPES = tuple(2.0 ** (-8.0 * (h + 1) / N_HEADS) for h in range(N_HEADS))

TOKEN_TILE = 512
CONV_PHASES = 4
ATTN_TIME_TILE = 1024
POOL_TIME_TILE = 512
POOL_PHASES = 4
FFN_CHUNK = 1536
SAMPLE_SEQ_TILE = 32
SAMPLE_FFN_SEQ_TILE = 32
VMEM_LIMIT = 56 * 1024 * 1024


def _params(n_axes):
    return pltpu.CompilerParams(dimension_semantics=("arbitrary",) * n_axes, vmem_limit_bytes=VMEM_LIMIT)


def _rmsnorm(x, g):
    return x * lax.rsqrt(jnp.mean(x * x, axis=-1, keepdims=True) + RMS_EPS) * g


def _head_rmsnorm(x, g):
    lane = lax.broadcasted_iota(jnp.int32, (x.shape[0], LANES), 1)
    low = lane < HEAD_DIM
    outs = []
    for s in range(x.shape[1] // LANES):
        xs = x[:, s * LANES:(s + 1) * LANES]
        sq = xs * xs
        s_low = jnp.sum(jnp.where(low, sq, 0.0), axis=-1, keepdims=True)
        s_high = jnp.sum(jnp.where(low, 0.0, sq), axis=-1, keepdims=True)
        ms = jnp.where(low, s_low, s_high) * (1.0 / HEAD_DIM)
        outs.append(xs * lax.rsqrt(ms + RMS_EPS) * g[:, s * LANES:(s + 1) * LANES])
    return outs[0] if len(outs) == 1 else jnp.concatenate(outs, axis=-1)


def _layernorm_silu(y, g, b):
    mu = jnp.mean(y, axis=-1, keepdims=True)
    yc = y - mu
    var = jnp.mean(yc * yc, axis=-1, keepdims=True)
    z = yc * lax.rsqrt(var + LN_EPS) * g + b
    return z * jax.nn.sigmoid(z)


def _ffn_apply(x, g_ref, wg_ref, wu_ref, wd_ref):
    h = _rmsnorm(x, g_ref[...]).astype(BF16)
    acc = x
    d_ff = wg_ref.shape[1]
    for c0 in range(0, d_ff, FFN_CHUNK):
        c1 = min(c0 + FFN_CHUNK, d_ff)
        gate = jnp.dot(h, wg_ref[:, c0:c1], preferred_element_type=F32)
        up = jnp.dot(h, wu_ref[:, c0:c1], preferred_element_type=F32)
        a = (gate * jax.nn.sigmoid(gate) * up).astype(BF16)
        acc = acc + jnp.dot(a, wd_ref[c0:c1, :], preferred_element_type=F32)
    return acc


def _ffn_specs(w_gate, layer, n_axes):
    _, d, f = w_gate.shape
    at_layer = {1: lambda i: (layer, 0, 0), 2: lambda b, i: (layer, 0, 0)}[n_axes]
    resident = functools.partial(pl.BlockSpec, pipeline_mode=pl.Buffered(1))
    return [pl.BlockSpec((None, 1, d), at_layer), resident((None, d, f), at_layer), resident((None, d, f), at_layer),
            resident((None, f, d), at_layer)]


def _inproj_body(x_ref, g_ref, w_ref, qn_ref, kn_ref, glu_ref, q_ref, k_ref, v_ref):
    h = _rmsnorm(x_ref[...], g_ref[...]).astype(BF16)
    o_q = 2 * D_CONV
    ua = jnp.dot(h, w_ref[:, o_q:], preferred_element_type=F32)
    uc = jnp.dot(h, w_ref[:, :o_q], preferred_element_type=F32)
    q_ref[...] = (_head_rmsnorm(ua[:, :D_ATTN], qn_ref[...]) * (HEAD_DIM ** -0.5)).astype(BF16)
    k_ref[...] = _head_rmsnorm(ua[:, D_ATTN:D_ATTN + D_KV], kn_ref[...])
    v_ref[...] = ua[:, D_ATTN + D_KV:]
    glu_ref[...] = uc[:, :D_CONV] * jax.nn.sigmoid(uc[:, D_CONV:])


def _inproj(x, norm_g, w_in, q_gain, k_gain, layer, i_even):
    t, d = x.shape
    n = w_in.shape[2]
    row = lambda width: pl.BlockSpec((TOKEN_TILE, width), lambda i: (i, 0))
    return pl.pallas_call(
        _inproj_body,
        grid=(t // TOKEN_TILE,),
        in_specs=[row(d),
                  pl.BlockSpec((None, 1, d), lambda i: (layer, 0, 0)),
                  pl.BlockSpec((None, d, n), lambda i: (i_even, 0, 0), pipeline_mode=pl.Buffered(1)),
                  pl.BlockSpec((None, 1, D_ATTN), lambda i: (i_even, 0, 0)),
                  pl.BlockSpec((None, 1, D_KV), lambda i: (i_even, 0, 0))],
        out_specs=[row(D_CONV), row(D_ATTN), row(D_KV), row(D_KV)],
        out_shape=[jax.ShapeDtypeStruct((t, D_CONV), F32), jax.ShapeDtypeStruct((t, D_ATTN), BF16),
                   jax.ShapeDtypeStruct((t, D_KV), F32), jax.ShapeDtypeStruct((t, D_KV), F32)],
        compiler_params=_params(1),
        name="inproj",
    )(x, norm_g, w_in, q_gain, k_gain)


def _outproj_ffn_body(xa_ref, ca_ref, aa_ref, xb_ref, cb_ref, ab_ref, w_ref, gf_ref, wg_ref, wu_ref, wd_ref,
                      oa_ref, ob_ref, *, steps_a):
    def tile(x_ref, c_ref, a_ref, o_ref):
        y = jnp.dot(c_ref[...], w_ref[:D_CONV, :], preferred_element_type=F32)
        y = y + jnp.dot(a_ref[...], w_ref[D_CONV:, :], preferred_element_type=F32)
        o_ref[...] = _ffn_apply(x_ref[...] + y, gf_ref, wg_ref, wu_ref, wd_ref)

    i = pl.program_id(0)

    @pl.when(i < steps_a)
    def _():
        tile(xa_ref, ca_ref, aa_ref, oa_ref)

    @pl.when(i >= steps_a)
    def _():
        tile(xb_ref, cb_ref, ab_ref, ob_ref)


def _outproj_ffn(group_a, group_b, w_out, i_even, norm_ffn, w_gate, w_up, w_down, layer):
    (xa, ca, aa), (xb, cb, ab) = group_a, group_b
    d = xa.shape[1]
    steps_a, steps_b = xa.shape[0] // TOKEN_TILE, xb.shape[0] // TOKEN_TILE
    row_a = lambda width: pl.BlockSpec((TOKEN_TILE, width), lambda i: (jnp.minimum(i, steps_a - 1), 0))
    row_b = lambda width: pl.BlockSpec((TOKEN_TILE, width), lambda i: (jnp.maximum(i - steps_a, 0), 0))
    return pl.pallas_call(
        functools.partial(_outproj_ffn_body, steps_a=steps_a),
        grid=(steps_a + steps_b,),
        in_specs=[row_a(d), row_a(D_CONV), row_a(D_ATTN), row_b(d), row_b(D_CONV), row_b(D_ATTN),
                  pl.BlockSpec((None, D_CONV + D_ATTN, d), lambda i: (i_even, 0, 0), pipeline_mode=pl.Buffered(1))]
        + _ffn_specs(w_gate, layer, 1),
        out_specs=[row_a(d), row_b(d)],
        out_shape=[jax.ShapeDtypeStruct(xa.shape, F32), jax.ShapeDtypeStruct(xb.shape, F32)],
        compiler_params=_params(1),
        name="outproj_ffn",
    )(xa, ca, aa, xb, cb, ab, w_out, norm_ffn, w_gate, w_up, w_down)


def _depthwise_conv_halves(ext_ref, y_ref, w_ref, b_ref, slab, half_rows):
    lead = CONV_HALO - (CONV_WIDTH - 1)
    rows = 8 * CONV_PHASES
    lanes = slice(slab * LANES, (slab + 1) * LANES)
    taps = [w_ref[k, :, lanes] for k in range(CONV_WIDTH)]

    def chunk(c, carry):
        r0 = c * rows
        acc = [jnp.zeros((16, LANES), F32) for _ in range(CONV_PHASES)]
        for e in range(CONV_WIDTH + CONV_PHASES - 1):
            v = jnp.concatenate([ext_ref[slab, pl.ds(r0 + lead + e, 8, stride=CONV_PHASES), :],
                                 ext_ref[slab, pl.ds(half_rows + r0 + lead + e, 8, stride=CONV_PHASES), :]],
                                axis=0).astype(BF16)
            for ph in range(CONV_PHASES):
                k = e - ph
                if 0 <= k < CONV_WIDTH:
                    acc[ph] = acc[ph] + v.astype(F32) * taps[k].astype(F32)
        bias = jnp.broadcast_to(b_ref[0:1, lanes], (16, LANES))
        for ph in range(CONV_PHASES):
            out = acc[ph] + bias
            y_ref[slab, pl.ds(r0 + ph, 8, stride=CONV_PHASES), :] = out[:8]
            y_ref[slab, pl.ds(half_rows + r0 + ph, 8, stride=CONV_PHASES), :] = out[8:]
        return carry

    lax.fori_loop(0, half_rows // rows, chunk, 0)


def _layernorm_silu_slabs(y_ref, g_ref, beta_ref, o_ref):
    n_slabs = y_ref.shape[0]
    ys = [y_ref[s] for s in range(n_slabs)]
    inv_c = 1.0 / (n_slabs * LANES)
    mu = jnp.sum(sum(ys), axis=-1, keepdims=True) * inv_c
    ycs = [y - mu for y in ys]
    var = jnp.sum(sum(yc * yc for yc in ycs), axis=-1, keepdims=True) * inv_c
    rstd = lax.rsqrt(var + LN_EPS)
    for s, yc in enumerate(ycs):
        lanes = slice(s * LANES, (s + 1) * LANES)
        z = yc * rstd * g_ref[:, lanes] + beta_ref[:, lanes]
        o_ref[:, lanes] = (z * jax.nn.sigmoid(z)).astype(o_ref.dtype)


def _banded_attention(sink_ref, q_ref, kcat, vcat, o_ref, bias_ref, sinkcol_ref, s_ref):
    b = pl.program_id(0)
    i = pl.program_id(1)
    w = WINDOW
    pair = 2 * HEAD_DIM
    n_blocks = q_ref.shape[0] // w
    lane = lax.broadcasted_iota(jnp.int32, (1, pair), 1)
    low = lane < HEAD_DIM

    @pl.when((b == 0) & (i == 0))
    def _():
        row = lax.broadcasted_iota(jnp.int32, (w, 2 * w), 0)
        col = lax.broadcasted_iota(jnp.int32, (w, 2 * w), 1)
        dist = w + row - col
        valid = (dist >= 0) & (dist < w)
        distf = dist.astype(F32)
        for h in range(N_HEADS):
            slab, half = h // 2, h % 2
            bias_ref[slab * w:(slab + 1) * w, half * 2 * w:(half + 1) * 2 * w] = (
                jnp.where(valid, -ALIBI_SLOPES[h] * distf, NEG_INF))
            sinkcol_ref[half, slab * w:(slab + 1) * w, :] = jnp.full((w, 1), sink_ref[h], F32)

    krot = pltpu.roll(kcat, HEAD_DIM, 1)
    vrot = pltpu.roll(vcat, HEAD_DIM, 1)
    zero = jnp.zeros((), F32)
    k_low = [jnp.where(low, kcat, zero).astype(BF16), jnp.where(low, krot, zero).astype(BF16)]
    k_high = [jnp.where(low, zero, krot).astype(BF16), jnp.where(low, zero, kcat).astype(BF16)]
    v_low = [jnp.where(low, vcat, zero).astype(BF16), jnp.where(low, vrot, zero).astype(BF16)]
    v_high = [jnp.where(low, zero, vrot).astype(BF16), jnp.where(low, zero, vcat).astype(BF16)]
    ones_low = jnp.broadcast_to(jnp.where(low, 1.0, 0.0).astype(BF16), (2 * w, pair))
    ones_high = jnp.broadcast_to(jnp.where(low, 0.0, 1.0).astype(BF16), (2 * w, pair))
    ones_ext = jnp.concatenate([ones_low, ones_high], axis=0)
    col = lax.broadcasted_iota(jnp.int32, (1, 4 * w), 1)
    no_prev = (col % (2 * w) < w) & (i == 0)

    for jb in range(n_blocks):
        rows = slice(jb * w, (jb + 1) * w)
        keys = slice(jb * w, (jb + 2) * w)
        for g in range(N_KV_HEADS):
            q2 = jnp.concatenate([q_ref[rows, (2 * g) * pair:(2 * g + 1) * pair],
                                  q_ref[rows, (2 * g + 1) * pair:(2 * g + 2) * pair]], axis=0)
            kk = jnp.concatenate([k_low[g][keys], k_high[g][keys]], axis=0)
            s_ref[jb, 2 * g * w:(2 * g + 2) * w, :] = lax.dot_general(
                q2, kk, (((1,), (1,)), ((), ())), preferred_element_type=F32)

    for jb in range(n_blocks):
        rows = slice(jb * w, (jb + 1) * w)
        keys = slice(jb * w, (jb + 2) * w)
        bias = bias_ref[...]
        if jb == 0:
            bias = jnp.where(no_prev, NEG_INF, bias)
        s = s_ref[jb] + bias
        ps, sink_terms = [], []
        for half in range(2):
            sh = s[:, half * 2 * w:(half + 1) * 2 * w]
            sink = sinkcol_ref[half]
            m = jnp.maximum(jnp.max(sh, axis=-1, keepdims=True), sink)
            ps.append(jnp.exp(sh - m).astype(BF16))
            sink_terms.append(jnp.exp(sink - m))
        p = jnp.concatenate(ps, axis=1)
        sink_term = jnp.where(low, sink_terms[0], sink_terms[1])
        for g in range(N_KV_HEADS):
            vv = jnp.concatenate([jnp.concatenate([v_low[g][keys], v_high[g][keys]], axis=0), ones_ext], axis=1)
            o = jnp.dot(p[2 * g * w:(2 * g + 2) * w], vv, preferred_element_type=F32)
            o = o[:, :pair] / (o[:, pair:] + sink_term[2 * g * w:(2 * g + 2) * w])
            o_ref[rows, (2 * g) * pair:(2 * g + 1) * pair] = o[:w].astype(o_ref.dtype)
            o_ref[rows, (2 * g + 1) * pair:(2 * g + 2) * pair] = o[w:].astype(o_ref.dtype)


def _attn_prompt_body(sink_ref, q_ref, kc_ref, kp_ref, vc_ref, vp_ref, o_ref, bias_ref, sinkcol_ref, s_ref):
    kcat = jnp.concatenate([kp_ref[...], kc_ref[...]], axis=0)
    vcat = jnp.concatenate([vp_ref[...], vc_ref[...]], axis=0)
    _banded_attention(sink_ref, q_ref, kcat, vcat, o_ref, bias_ref, sinkcol_ref, s_ref)


def _attn_prompt(q, k, v, sinks):
    n, t, _ = q.shape
    tt = ATTN_TIME_TILE
    cur = lambda width: pl.BlockSpec((None, tt, width), lambda b, i: (b, i, 0))
    prev = lambda width: pl.BlockSpec((None, WINDOW, width),
                                      lambda b, i: (b, jnp.maximum(i * (tt // WINDOW) - 1, 0), 0))
    return pl.pallas_call(
        _attn_prompt_body,
        grid=(n, t // tt),
        in_specs=[pl.BlockSpec(memory_space=pltpu.SMEM), cur(D_ATTN), cur(D_KV), prev(D_KV), cur(D_KV), prev(D_KV)],
        out_specs=cur(D_ATTN),
        out_shape=jax.ShapeDtypeStruct((n, t, D_ATTN), BF16),
        scratch_shapes=[pltpu.VMEM((N_HEADS // 2 * WINDOW, 4 * WINDOW), F32),
                        pltpu.VMEM((2, N_HEADS // 2 * WINDOW, 1), F32),
                        pltpu.VMEM((tt // WINDOW, N_HEADS // 2 * WINDOW, 4 * WINDOW), F32)],
        compiler_params=_params(2),
        name="attn_prompt",
    )(sinks, q, k, k, v, v)


def _inproj_conv_prompt_body(x_ref, g_ref, w_ref, qn_ref, kn_ref, wt_ref, b_ref, lg_ref, lb_ref,
                             conv_ref, q_ref, k_ref, v_ref, glu_tail_ref, ext_ref, y_ref):
    tt = x_ref.shape[0]
    n_slabs = D_CONV // LANES
    i = pl.program_id(1)

    @pl.when(i == 0)
    def _():
        ext_ref[:, 0:CONV_HALO, :] = jnp.zeros((n_slabs, CONV_HALO, LANES), F32)

    @pl.when(i > 0)
    def _():
        ext_ref[:, 0:CONV_HALO, :] = ext_ref[:, tt:tt + CONV_HALO, :]

    h = _rmsnorm(x_ref[...], g_ref[...]).astype(BF16)
    o_q = 2 * D_CONV
    ua = jnp.dot(h, w_ref[:, o_q:], preferred_element_type=F32)
    uc = jnp.dot(h, w_ref[:, :o_q], preferred_element_type=F32)
    q_ref[...] = (_head_rmsnorm(ua[:, :D_ATTN], qn_ref[...]) * (HEAD_DIM ** -0.5)).astype(BF16)
    k_ref[...] = _head_rmsnorm(ua[:, D_ATTN:D_ATTN + D_KV], kn_ref[...])
    v_ref[...] = ua[:, D_ATTN + D_KV:]
    glu = uc[:, :D_CONV] * jax.nn.sigmoid(uc[:, D_CONV:])
    glu_tail_ref[...] = glu[tt - CONV_HALO:, :]
    for s in range(n_slabs):
        ext_ref[s, CONV_HALO:, :] = glu[:, s * LANES:(s + 1) * LANES]
    for s in range(n_slabs):
        _depthwise_conv_halves(ext_ref, y_ref, wt_ref, b_ref, s, tt // 2)
    _layernorm_silu_slabs(y_ref, lg_ref, lb_ref, conv_ref)


def _inproj_conv_prompt(x, norm_g, w_in, q_gain, k_gain, w_taps, b_dw, ln_g, ln_b, layer, i_even):
    n, t, d = x.shape
    n_in = w_in.shape[2]
    tt = ATTN_TIME_TILE
    tile = lambda width: pl.BlockSpec((None, tt, width), lambda b, i: (b, i, 0))
    even = lambda width: pl.BlockSpec((None, 1, width), lambda b, i: (i_even, 0, 0))
    return pl.pallas_call(
        _inproj_conv_prompt_body,
        grid=(n, t // tt),
        in_specs=[tile(d),
                  pl.BlockSpec((None, 1, d), lambda b, i: (layer, 0, 0)),
                  pl.BlockSpec((None, d, n_in), lambda b, i: (i_even, 0, 0), pipeline_mode=pl.Buffered(1)),
                  even(D_ATTN), even(D_KV),
                  pl.BlockSpec((None, CONV_WIDTH, 16, D_CONV), lambda b, i: (i_even, 0, 0, 0)),
                  even(D_CONV), even(D_CONV), even(D_CONV)],
        out_specs=[tile(D_CONV), tile(D_ATTN), tile(D_KV), tile(D_KV),
                   pl.BlockSpec((None, CONV_HALO, D_CONV), lambda b, i: (b, 0, 0))],
        out_shape=[jax.ShapeDtypeStruct((n, t, D_CONV), BF16), jax.ShapeDtypeStruct((n, t, D_ATTN), BF16),
                   jax.ShapeDtypeStruct((n, t, D_KV), F32), jax.ShapeDtypeStruct((n, t, D_KV), F32),
                   jax.ShapeDtypeStruct((n, CONV_HALO, D_CONV), F32)],
        scratch_shapes=[pltpu.VMEM((D_CONV // LANES, CONV_HALO + tt, LANES), F32),
                        pltpu.VMEM((D_CONV // LANES, tt, LANES), F32)],
        compiler_params=_params(2),
        name="inproj_conv_prompt",
    )(x, norm_g, w_in, q_gain, k_gain, w_taps, b_dw, ln_g, ln_b)


def _window_sums(vals, w, n_out):
    shared = list(range(n_out - 1, w))
    base = None
    for e in shared:
        base = vals[e] if base is None else base + vals[e]
    sums = []
    for i in range(n_out):
        acc = base
        for e in range(i, i + w):
            if e not in shared:
                acc = vals[e] if acc is None else acc + vals[e]
        sums.append(acc)
    return sums


def _pool_deltas_strided(ext_ref, d_ref, pos0, n_rows):
    n_slabs = ext_ref.shape[0]
    slabs_per_group = n_slabs // len(POOL_WINDOWS)
    rows = 8 * POOL_PHASES

    def chunk(c, carry):
        r0 = c * rows
        sub = lax.broadcasted_iota(jnp.int32, (8, LANES), 0)
        pos = [pos0 + r0 + ph + POOL_PHASES * sub for ph in range(POOL_PHASES)]
        inv = {w: [1.0 / jnp.minimum(w, p + 1).astype(F32) for p in pos] for w in POOL_WINDOWS}
        for slab in range(n_slabs):
            w = POOL_WINDOWS[slab // slabs_per_group]
            first = POOL_HALO - (w - 1)
            vals = [ext_ref[slab, pl.ds(r0 + first + e, 8, stride=POOL_PHASES), :] for e in range(w + POOL_PHASES - 1)]
            sums = _window_sums(vals, w, POOL_PHASES)
            for ph in range(POOL_PHASES):
                d_ref[slab, pl.ds(r0 + ph, 8, stride=POOL_PHASES), :] = sums[ph] * inv[w][ph] - vals[ph + w - 1]
        return carry

    lax.fori_loop(0, n_rows // rows, chunk, 0)


def _pool_ffn_prompt_body(x_ref, g_ref, w_ref, sc_ref, gf_ref, wg_ref, wu_ref, wd_ref, o_ref, tail_ref, ext_ref, d_ref):
    tt, d = x_ref.shape
    n_slabs = d // LANES
    slabs_per_group = n_slabs // len(POOL_WINDOWS)
    i = pl.program_id(1)

    @pl.when(i == 0)
    def _():
        ext_ref[:, 0:POOL_HALO, :] = jnp.zeros((n_slabs, POOL_HALO, LANES), F32)

    @pl.when(i > 0)
    def _():
        ext_ref[:, 0:POOL_HALO, :] = ext_ref[:, tt:tt + POOL_HALO, :]

    x = x_ref[...]
    xn = _rmsnorm(x, g_ref[...])
    for s in range(n_slabs):
        ext_ref[s, POOL_HALO:, :] = xn[:, s * LANES:(s + 1) * LANES]
    tail_ref[...] = xn[tt - POOL_HALO:, :]
    _pool_deltas_strided(ext_ref, d_ref, i * tt, tt)
    ys = []
    for g in range(len(POOL_WINDOWS)):
        dg = jnp.concatenate([d_ref[s] for s in range(g * slabs_per_group, (g + 1) * slabs_per_group)], axis=-1)
        ys.append(jnp.dot(dg.astype(BF16), w_ref[g], preferred_element_type=F32))
    x1 = x + jnp.concatenate(ys, axis=-1) * sc_ref[...]
    o_ref[...] = _ffn_apply(x1, gf_ref, wg_ref, wu_ref, wd_ref)


def _pool_ffn_prompt(x, norm_g, w_pool, pool_scale, layer, i_odd, norm_ffn, w_gate, w_up, w_down):
    n, t, d = x.shape
    tt = POOL_TIME_TILE
    ng, gd, _ = w_pool.shape[1:]
    tile = pl.BlockSpec((None, tt, d), lambda b, i: (b, i, 0))
    return pl.pallas_call(
        _pool_ffn_prompt_body,
        grid=(n, t // tt),
        in_specs=[tile,
                  pl.BlockSpec((None, 1, d), lambda b, i: (layer, 0, 0)),
                  pl.BlockSpec((None, ng, gd, gd), lambda b, i: (i_odd, 0, 0, 0)),
                  pl.BlockSpec((None, 1, d), lambda b, i: (i_odd, 0, 0))] + _ffn_specs(w_gate, layer, 2),
        out_specs=[tile, pl.BlockSpec((None, POOL_HALO, d), lambda b, i: (b, 0, 0))],
        out_shape=[jax.ShapeDtypeStruct((n, t, d), F32), jax.ShapeDtypeStruct((n, POOL_HALO, d), F32)],
        scratch_shapes=[pltpu.VMEM((d // LANES, POOL_HALO + tt, LANES), F32), pltpu.VMEM((d // LANES, tt, LANES), F32)],
        compiler_params=_params(2),
        name="pool_ffn_prompt",
    )(x, norm_g, w_pool, pool_scale, norm_ffn, w_gate, w_up, w_down)


def _grown_state(prev, slab_dims, tile_axis, nb):
    layers = 0 if prev is None else prev.shape[0]

    def spec(n_layers):
        block = list(slab_dims)
        block[tile_axis] = nb

        def index(b):
            idx = [0] * (len(slab_dims) + 1)
            idx[tile_axis + 1] = b
            return tuple(idx)

        return pl.BlockSpec((n_layers, *block), index)

    ins, in_specs = ([], []) if prev is None else ([prev], [spec(layers)])
    return ins, in_specs, spec(layers + 1), jax.ShapeDtypeStruct((layers + 1, *slab_dims), F32)


def _copy_earlier_layers(new_ref, prev_refs):
    for prev_ref in prev_refs:
        for l in range(prev_ref.shape[0]):
            new_ref[l] = prev_ref[l]


def _conv_sample_body(cache_ref, glu_ref, w_ref, b_ref, g_ref, beta_ref, *refs):
    o_ref, all_new_ref = refs[-2:]
    _copy_earlier_layers(all_new_ref, refs[:-2])
    this_layer = all_new_ref.shape[0] - 1
    hist = cache_ref.shape[0]
    t = glu_ref.shape[0]
    ext = lambda j: cache_ref[j] if j < hist else glu_ref[j - hist]
    rounded = lambda a: a.astype(BF16).astype(F32)
    for j in range(hist):
        all_new_ref[this_layer, j] = ext(t + j)
    for step in range(t):
        acc = rounded(w_ref[0:1, :]) * rounded(ext(step)) + b_ref[...]
        for k in range(1, CONV_WIDTH):
            acc = acc + rounded(w_ref[k:k + 1, :]) * rounded(ext(step + k))
        o_ref[step] = _layernorm_silu(acc, g_ref[...], beta_ref[...]).astype(o_ref.dtype)


def _conv_sample(cache_tm, glu_tm, w_dw, b_dw, ln_g, ln_b, i_even, prev_new):
    _, hist, n, c = cache_tm.shape
    t = glu_tm.shape[0]
    nb = SAMPLE_SEQ_TILE
    vec = pl.BlockSpec((None, 1, c), lambda b: (i_even, 0, 0))
    prev_in, prev_specs, new_spec, new_shape = _grown_state(prev_new, (hist, n, c), 1, nb)
    return pl.pallas_call(
        _conv_sample_body,
        grid=(n // nb,),
        in_specs=[pl.BlockSpec((None, hist, nb, c), lambda b: (i_even, 0, b, 0)),
                  pl.BlockSpec((t, nb, c), lambda b: (0, b, 0)),
                  pl.BlockSpec((None, CONV_WIDTH, c), lambda b: (i_even, 0, 0)),
                  vec, vec, vec] + prev_specs,
        out_specs=[pl.BlockSpec((t, nb, c), lambda b: (0, b, 0)), new_spec],
        out_shape=[jax.ShapeDtypeStruct((t, n, c), BF16), new_shape],
        compiler_params=_params(1),
        name="conv_sample",
    )(cache_tm, glu_tm, w_dw, b_dw, ln_g, ln_b, *prev_in)


def _attn_sample_body(sink_ref, q_ref, kc_ref, kn_ref, vc_ref, vn_ref, *refs):
    o_ref, all_knew_ref, all_vnew_ref = refs[-3:]
    _copy_earlier_layers(all_knew_ref, refs[:-3][0:1])
    _copy_earlier_layers(all_vnew_ref, refs[:-3][1:2])
    knew_ref = all_knew_ref.at[all_knew_ref.shape[0] - 1]
    vnew_ref = all_vnew_ref.at[all_vnew_ref.shape[0] - 1]
    nb, nkv, rows, hd = q_ref.shape
    buf = kc_ref.shape[-1]
    t = kn_ref.shape[-1]
    batch = nb * nkv
    q = q_ref[...].reshape(batch, rows, hd)
    kc = kc_ref[...].reshape(batch, hd, buf)
    kn = kn_ref[...].reshape(batch, hd, t)
    vc = vc_ref[...].reshape(batch, hd, buf)
    vn = vn_ref[...].reshape(batch, hd, t)

    lane = lax.broadcasted_iota(jnp.int32, (1, 1, buf), 2)
    pad = jnp.zeros((batch, hd, buf - t), F32)
    k_new = jnp.where(lane < buf - t, pltpu.roll(kc, buf - t, 2), jnp.concatenate([pad, kn], axis=-1))
    v_new = jnp.where(lane < buf - t, pltpu.roll(vc, buf - t, 2), jnp.concatenate([pad, vn], axis=-1))
    knew_ref[...] = k_new.reshape(nb, nkv, hd, buf)
    vnew_ref[...] = v_new.reshape(nb, nkv, hd, buf)

    s_c = jnp.einsum("bqd,bdk->bqk", q, kc.astype(BF16), preferred_element_type=F32)
    s_n = jnp.einsum("bqd,bdk->bqk", q, kn.astype(BF16), preferred_element_type=F32)
    step = lax.broadcasted_iota(jnp.int32, (rows, 1), 0) % t
    head_in_group = lax.broadcasted_iota(jnp.int32, (rows, 1), 0) // t
    dist_c = buf + step - lax.broadcasted_iota(jnp.int32, (rows, buf), 1)
    dist_n = step - lax.broadcasted_iota(jnp.int32, (rows, t), 1)
    s_c = s_c.reshape(nb, nkv, rows, buf)
    s_n = s_n.reshape(nb, nkv, rows, t)
    ps_c, ps_n, invs = [], [], []
    for kv in range(nkv):
        slope = jnp.zeros((rows, 1), F32)
        sink = jnp.zeros((rows, 1), F32)
        for j in range(GQA_GROUP):
            slope = jnp.where(head_in_group == j, ALIBI_SLOPES[kv * GQA_GROUP + j], slope)
            sink = jnp.where(head_in_group == j, sink_ref[kv * GQA_GROUP + j], sink)
        sc = jnp.where((dist_c >= 0) & (dist_c < WINDOW), s_c[:, kv] - slope * dist_c.astype(F32), NEG_INF)
        sn = jnp.where((dist_n >= 0) & (dist_n < WINDOW), s_n[:, kv] - slope * dist_n.astype(F32), NEG_INF)
        m = jnp.maximum(jnp.maximum(jnp.max(sc, axis=-1, keepdims=True), jnp.max(sn, axis=-1, keepdims=True)), sink)
        pc = jnp.exp(sc - m)
        pn = jnp.exp(sn - m)
        denom = jnp.sum(pc, axis=-1, keepdims=True) + jnp.sum(pn, axis=-1, keepdims=True) + jnp.exp(sink - m)
        ps_c.append(pc.astype(BF16))
        ps_n.append(pn.astype(BF16))
        invs.append(1.0 / denom)
    p_c = jnp.stack(ps_c, axis=1).reshape(batch, rows, buf)
    p_n = jnp.stack(ps_n, axis=1).reshape(batch, rows, t)
    inv = jnp.stack(invs, axis=1).reshape(batch, rows, 1)
    o = jnp.einsum("bqk,bdk->bqd", p_c, vc.astype(BF16), preferred_element_type=F32)
    o = o + jnp.einsum("bqk,bdk->bqd", p_n, vn.astype(BF16), preferred_element_type=F32)
    o_ref[...] = (o * inv).reshape(nb, nkv, rows, hd).astype(o_ref.dtype)


def _attn_sample(q, kc, kn, vc, vn, sinks, i_even, prev_new):
    n, nkv, rows, hd = q.shape
    buf = kc.shape[-1]
    t = kn.shape[-1]
    nb = SAMPLE_SEQ_TILE
    cache = pl.BlockSpec((None, nb, nkv, hd, buf), lambda b: (i_even, b, 0, 0, 0))
    new = pl.BlockSpec((nb, nkv, hd, t), lambda b: (b, 0, 0, 0))
    qspec = pl.BlockSpec((nb, nkv, rows, hd), lambda b: (b, 0, 0, 0))
    prev_k, prev_v = (None, None) if prev_new is None else prev_new
    k_in, k_specs, k_spec, k_shape = _grown_state(prev_k, (n, nkv, hd, buf), 0, nb)
    v_in, v_specs, v_spec, v_shape = _grown_state(prev_v, (n, nkv, hd, buf), 0, nb)
    return pl.pallas_call(
        _attn_sample_body,
        grid=(n // nb,),
        in_specs=[pl.BlockSpec(memory_space=pltpu.SMEM), qspec, cache, new, cache, new] + k_specs + v_specs,
        out_specs=[qspec, k_spec, v_spec],
        out_shape=[jax.ShapeDtypeStruct((n, nkv, rows, hd), BF16), k_shape, v_shape],
        compiler_params=_params(1),
        name="attn_sample",
    )(sinks, q, kc, kn, vc, vn, *k_in, *v_in)


def _pool_ffn_sample_body(x_ref, st_ref, g_ref, w_ref, sc_ref, gf_ref, wg_ref, wu_ref, wd_ref, *refs, pos0):
    o_ref, all_new_ref = refs[-2:]
    _copy_earlier_layers(all_new_ref, refs[:-2])
    new_ref = all_new_ref.at[all_new_ref.shape[0] - 1]
    t, nb, d = x_ref.shape
    hist = st_ref.shape[0]
    gdim = d // len(POOL_WINDOWS)
    x = x_ref[...].reshape(t * nb, d)
    xn = _rmsnorm(x, g_ref[...])
    ext = lambda j: st_ref[j] if j < hist else xn[(j - hist) * nb:(j - hist + 1) * nb]
    for j in range(hist):
        new_ref[j] = ext(t + j)
    ys = []
    for g, w in enumerate(POOL_WINDOWS):
        lanes = slice(g * gdim, (g + 1) * gdim)
        steps = []
        for step in range(t):
            acc = ext(hist + step)[:, lanes]
            for j in range(1, w):
                acc = acc + ext(hist + step - j)[:, lanes]
            cnt = float(min(w, pos0 + step + 1))
            steps.append(acc / cnt - xn[step * nb:(step + 1) * nb, lanes])
        dg = jnp.concatenate(steps, axis=0)
        ys.append(jnp.dot(dg.astype(BF16), w_ref[g], preferred_element_type=F32))
    x1 = x + jnp.concatenate(ys, axis=-1) * sc_ref[...]
    o_ref[...] = _ffn_apply(x1, gf_ref, wg_ref, wu_ref, wd_ref).reshape(t, nb, d)


def _pool_ffn_sample(x_tm, state_tm, norm_g, w_pool, pool_scale, layer, i_odd, pos0, norm_ffn, w_gate, w_up, w_down,
                     prev_new):
    t, n, d = x_tm.shape
    hist = state_tm.shape[1]
    nb = SAMPLE_FFN_SEQ_TILE
    ng, gd, _ = w_pool.shape[1:]
    prev_in, prev_specs, new_spec, new_shape = _grown_state(prev_new, (hist, n, d), 1, nb)
    return pl.pallas_call(
        functools.partial(_pool_ffn_sample_body, pos0=pos0),
        grid=(n // nb,),
        in_specs=[pl.BlockSpec((t, nb, d), lambda b: (0, b, 0)),
                  pl.BlockSpec((None, hist, nb, d), lambda b: (i_odd, 0, b, 0)),
                  pl.BlockSpec((None, 1, d), lambda b: (layer, 0, 0)),
                  pl.BlockSpec((None, ng, gd, gd), lambda b: (i_odd, 0, 0, 0)),
                  pl.BlockSpec((None, 1, d), lambda b: (i_odd, 0, 0))] + _ffn_specs(w_gate, layer, 1) + prev_specs,
        out_specs=[pl.BlockSpec((t, nb, d), lambda b: (0, b, 0)), new_spec],
        out_shape=[jax.ShapeDtypeStruct((t, n, d), F32), new_shape],
        compiler_params=_params(1),
        name="pool_ffn_sample",
    )(x_tm, state_tm, norm_g, w_pool, pool_scale, norm_ffn, w_gate, w_up, w_down, *prev_in)


def _prepare_weights(norm_mix, norm_ffn, w_in, q_norm, k_norm, sinks, w_dw, b_dw, conv_norm_g, conv_norm_b, w_out,
                     w_pool, pool_scale, w_gate, w_up, w_down):
    vec = lambda a: a[:, None, :]
    return dict(
        norm_mix=vec(norm_mix), norm_ffn=vec(norm_ffn), w_in=w_in.astype(BF16),
        q_gain=vec(jnp.tile(q_norm, (1, N_HEADS))), k_gain=vec(jnp.tile(k_norm, (1, N_KV_HEADS))), sinks=sinks,
        w_dw=w_dw, b_dw=vec(b_dw),
        w_taps=jnp.broadcast_to(w_dw.astype(BF16)[:, :, None, :], w_dw.shape[:2] + (16, w_dw.shape[2])),
        ln_g=vec(conv_norm_g), ln_b=vec(conv_norm_b), w_out=w_out.astype(BF16),
        w_pool=w_pool.astype(BF16), pool_scale=vec(pool_scale),
        w_gate=w_gate.astype(BF16), w_up=w_up.astype(BF16), w_down=w_down.astype(BF16))


def _trunks(x_prompt, x_sample, conv_bufs, k_bufs, v_bufs, pool_bufs, pos0, p, depth):
    n, t, d = x_prompt.shape
    ns, ts, _ = x_sample.shape
    xp = x_prompt.reshape(n * t, d)
    x_tm = x_sample.transpose(1, 0, 2)
    conv_tm = conv_bufs.transpose(0, 2, 1, 3)
    pool_tm = pool_bufs.transpose(0, 2, 1, 3)
    k_t = k_bufs.transpose(0, 1, 3, 4, 2)
    v_t = v_bufs.transpose(0, 1, 3, 4, 2)
    ffn = lambda layer: (p["norm_ffn"], p["w_gate"], p["w_up"], p["w_down"], layer)
    new_conv, new_k, new_v, new_pool = [], [], [], []
    conv_new = kv_new = pool_new = None
    for layer in range(depth):
        i = layer // 2
        if layer % 2 == 0:
            conv_p, q, k, v, glu_tail = _inproj_conv_prompt(
                xp.reshape(n, t, d), p["norm_mix"], p["w_in"], p["q_gain"], p["k_gain"], p["w_taps"], p["b_dw"],
                p["ln_g"], p["ln_b"], layer, i)
            attn_p = _attn_prompt(q, k, v, p["sinks"][i])
            keep = min(WINDOW, t)
            new_conv.append(glu_tail[:, CONV_HALO - (CONV_WIDTH - 1):])
            new_k.append(k[:, t - keep:].reshape(n, keep, N_KV_HEADS, HEAD_DIM))
            new_v.append(v[:, t - keep:].reshape(n, keep, N_KV_HEADS, HEAD_DIM))

            xs = x_tm.reshape(ts * ns, d)
            glu, q, k, v = _inproj(xs, p["norm_mix"], p["w_in"], p["q_gain"], p["k_gain"], layer, i)
            conv_s, conv_new = _conv_sample(conv_tm, glu.reshape(ts, ns, D_CONV), p["w_dw"], p["b_dw"], p["ln_g"],
                                            p["ln_b"], i, conv_new)
            qh = q.reshape(ts, ns, N_KV_HEADS, GQA_GROUP, HEAD_DIM).transpose(1, 2, 3, 0, 4)
            qh = qh.reshape(ns, N_KV_HEADS, GQA_GROUP * ts, HEAD_DIM)
            kn = k.reshape(ts, ns, N_KV_HEADS, HEAD_DIM).transpose(1, 2, 3, 0)
            vn = v.reshape(ts, ns, N_KV_HEADS, HEAD_DIM).transpose(1, 2, 3, 0)
            oh, *kv_new = _attn_sample(qh, k_t, kn, v_t, vn, p["sinks"][i], i, kv_new)
            attn_s = oh.reshape(ns, N_KV_HEADS, GQA_GROUP, ts, HEAD_DIM).transpose(3, 0, 1, 2, 4)

            xp, xs = _outproj_ffn((xp, conv_p.reshape(n * t, -1), attn_p.reshape(n * t, -1)),
                                  (xs, conv_s.reshape(ts * ns, D_CONV), attn_s.reshape(ts * ns, D_ATTN)),
                                  p["w_out"], i, *ffn(layer))
            x_tm = xs.reshape(ts, ns, d)
        else:
            y, tail = _pool_ffn_prompt(xp.reshape(n, t, d), p["norm_mix"], p["w_pool"], p["pool_scale"], layer, i,
                                       *ffn(layer)[:-1])
            xp = y.reshape(n * t, d)
            new_pool.append(tail[:, POOL_HALO - POOL_BUF:])
            x_tm, pool_new = _pool_ffn_sample(x_tm, pool_tm, p["norm_mix"], p["w_pool"], p["pool_scale"], layer, i, pos0,
                                              *ffn(layer)[:-1], pool_new)
    return (xp.reshape(n, t, d), x_tm.transpose(1, 0, 2),
            jnp.stack(new_conv), jnp.stack(new_k), jnp.stack(new_v), jnp.stack(new_pool),
            conv_new.transpose(0, 2, 1, 3), kv_new[0].transpose(0, 1, 4, 2, 3), kv_new[1].transpose(0, 1, 4, 2, 3),
            pool_new.transpose(0, 2, 1, 3))


def kernel(x_prompt, x_sample, cache_conv, cache_k, cache_v, state_pool, norm_mix, norm_ffn, w_in, q_norm, k_norm,
           sinks, w_dw, b_dw, conv_norm_g, conv_norm_b, w_out, w_pool, pool_scale, w_gate, w_up, w_down):
    depth = norm_mix.shape[0]
    p = _prepare_weights(norm_mix, norm_ffn, w_in, q_norm, k_norm, sinks, w_dw, b_dw, conv_norm_g, conv_norm_b, w_out,
                         w_pool, pool_scale, w_gate, w_up, w_down)
    return _trunks(x_prompt, x_sample, cache_conv, cache_k, cache_v, state_pool, PAST_LEN, p, depth)
```

```python
import functools

import jax
import jax.numpy as jnp
from jax import lax
from jax.experimental import pallas as pl
from jax.experimental.pallas import tpu as pltpu

F32 = jnp.float32
BF16 = jnp.bfloat16

HEAD_DIM = 64
N_HEADS = 8
N_KV_HEADS = 2
GQA_GROUP = N_HEADS // N_KV_HEADS
D_ATTN = N_HEADS * HEAD_DIM
D_KV = N_KV_HEADS * HEAD_DIM
WINDOW = 128
PAST_LEN = 8192
D_CONV = 512
CONV_WIDTH = 31
CONV_HALO = 32
POOL_WINDOWS = (2, 4, 8, 16)
POOL_BUF = max(POOL_WINDOWS) - 1
POOL_HALO = 16
RMS_EPS = 1e-6
LN_EPS = 1e-5
NEG_INF = -1e30
LANES = 128
ALIBI_SLOPES = tuple(2.0 ** (-8.0 * (h + 1) / N_HEADS) for h in range(N_HEADS))

TOKEN_TILE = 512
CONV_PHASES = 4
INPROJ_CONV_TIME_TILE = 1024
ATTN_TIME_TILE = 2048
POOL_TIME_TILE = 512
POOL_PHASES = 4
FFN_CHUNK = 1536
SAMPLE_SEQ_TILE = 32
SAMPLE_FFN_SEQ_TILE = 32
VMEM_LIMIT = 56 * 1024 * 1024


def _params(n_axes):
    return pltpu.CompilerParams(dimension_semantics=("arbitrary",) * n_axes, vmem_limit_bytes=VMEM_LIMIT)


def _rmsnorm(x, g):
    return x * lax.rsqrt(jnp.mean(x * x, axis=-1, keepdims=True) + RMS_EPS) * g


def _head_rmsnorm(x, g):
    lane = lax.broadcasted_iota(jnp.int32, (x.shape[0], LANES), 1)
    low = lane < HEAD_DIM
    outs = []
    for s in range(x.shape[1] // LANES):
        xs = x[:, s * LANES:(s + 1) * LANES]
        sq = xs * xs
        s_low = jnp.sum(jnp.where(low, sq, 0.0), axis=-1, keepdims=True)
        s_high = jnp.sum(jnp.where(low, 0.0, sq), axis=-1, keepdims=True)
        ms = jnp.where(low, s_low, s_high) * (1.0 / HEAD_DIM)
        outs.append(xs * lax.rsqrt(ms + RMS_EPS) * g[:, s * LANES:(s + 1) * LANES])
    return outs[0] if len(outs) == 1 else jnp.concatenate(outs, axis=-1)


def _layernorm_silu(y, g, b):
    mu = jnp.mean(y, axis=-1, keepdims=True)
    yc = y - mu
    var = jnp.mean(yc * yc, axis=-1, keepdims=True)
    z = yc * lax.rsqrt(var + LN_EPS) * g + b
    return z * jax.nn.sigmoid(z)


def _ffn_apply(x, g_ref, wg_ref, wu_ref, wd_ref, arriving=None):
    h = _rmsnorm(x, g_ref[...]).astype(BF16)
    acc = x
    d_ff = wg_ref.shape[1]
    for c0 in range(0, d_ff, FFN_CHUNK):
        c1 = min(c0 + FFN_CHUNK, d_ff)
        if arriving is not None:
            arriving["gate", c0].wait()
        gate = jnp.dot(h, wg_ref[:, c0:c1], preferred_element_type=F32)
        if arriving is not None:
            arriving["up", c0].wait()
        up = jnp.dot(h, wu_ref[:, c0:c1], preferred_element_type=F32)
        a = (gate * jax.nn.sigmoid(gate) * up).astype(BF16)
        if arriving is not None:
            arriving["down", c0].wait()
        acc = acc + jnp.dot(a, wd_ref[c0:c1, :], preferred_element_type=F32)
    return acc


def _ffn_weight_copies(wg_hbm, wu_hbm, wd_hbm, wg_ref, wu_ref, wd_ref, sem_ref, layer):
    copies = {}
    d_ff = wg_ref.shape[1]
    for c0 in range(0, d_ff, FFN_CHUNK):
        c1 = min(c0 + FFN_CHUNK, d_ff)
        pieces = (("gate", wg_hbm.at[layer, :, c0:c1], wg_ref.at[:, c0:c1]),
                  ("up", wu_hbm.at[layer, :, c0:c1], wu_ref.at[:, c0:c1]),
                  ("down", wd_hbm.at[layer, c0:c1, :], wd_ref.at[c0:c1, :]))
        for name, src, dst in pieces:
            copies[name, c0] = pltpu.make_async_copy(src, dst, sem_ref.at[len(copies)])
    return copies


def _ffn_specs(w_gate, layer, n_axes):
    d = w_gate.shape[1]
    at_layer = {1: lambda i: (layer, 0, 0), 2: lambda b, i: (layer, 0, 0)}[n_axes]
    return [pl.BlockSpec((None, 1, d), at_layer)] + [pl.BlockSpec(memory_space=pl.ANY)] * 3


def _ffn_scratch(w_gate):
    _, d, f = w_gate.shape
    n_pieces = 3 * (-(-f // FFN_CHUNK))
    return [pltpu.VMEM((d, f), w_gate.dtype), pltpu.VMEM((d, f), w_gate.dtype), pltpu.VMEM((f, d), w_gate.dtype),
            pltpu.SemaphoreType.DMA((n_pieces,))]


def _inproj_body(x_ref, g_ref, w_ref, qn_ref, kn_ref, glu_ref, q_ref, k_ref, v_ref):
    h = _rmsnorm(x_ref[...], g_ref[...]).astype(BF16)
    o_q = 2 * D_CONV
    ua = jnp.dot(h, w_ref[:, o_q:], preferred_element_type=F32)
    uc = jnp.dot(h, w_ref[:, :o_q], preferred_element_type=F32)
    q_ref[...] = (_head_rmsnorm(ua[:, :D_ATTN], qn_ref[...]) * (HEAD_DIM ** -0.5)).astype(BF16)
    k_ref[...] = _head_rmsnorm(ua[:, D_ATTN:D_ATTN + D_KV], kn_ref[...])
    v_ref[...] = ua[:, D_ATTN + D_KV:]
    glu_ref[...] = uc[:, :D_CONV] * jax.nn.sigmoid(uc[:, D_CONV:])


def _inproj(x, norm_g, w_in, q_gain, k_gain, layer, i_even):
    t, d = x.shape
    n = w_in.shape[2]
    row = lambda width: pl.BlockSpec((TOKEN_TILE, width), lambda i: (i, 0))
    return pl.pallas_call(
        _inproj_body,
        grid=(t // TOKEN_TILE,),
        in_specs=[row(d),
                  pl.BlockSpec((None, 1, d), lambda i: (layer, 0, 0)),
                  pl.BlockSpec((None, d, n), lambda i: (i_even, 0, 0), pipeline_mode=pl.Buffered(1)),
                  pl.BlockSpec((None, 1, D_ATTN), lambda i: (i_even, 0, 0)),
                  pl.BlockSpec((None, 1, D_KV), lambda i: (i_even, 0, 0))],
        out_specs=[row(D_CONV), row(D_ATTN), row(D_KV), row(D_KV)],
        out_shape=[jax.ShapeDtypeStruct((t, D_CONV), F32), jax.ShapeDtypeStruct((t, D_ATTN), BF16),
                   jax.ShapeDtypeStruct((t, D_KV), F32), jax.ShapeDtypeStruct((t, D_KV), F32)],
        compiler_params=_params(1),
        name="inproj",
    )(x, norm_g, w_in, q_gain, k_gain)


def _outproj_ffn_body(xa_ref, ca_ref, aa_ref, xb_ref, cb_ref, ab_ref, w_ref, gf_ref, wg_hbm, wu_hbm, wd_hbm,
                      oa_ref, ob_ref, wg_ref, wu_ref, wd_ref, sem_ref, *, steps_a, layer):
    def tile(x_ref, c_ref, a_ref, o_ref, arriving=None):
        y = jnp.dot(c_ref[...], w_ref[:D_CONV, :], preferred_element_type=F32)
        y = y + jnp.dot(a_ref[...], w_ref[D_CONV:, :], preferred_element_type=F32)
        o_ref[...] = _ffn_apply(x_ref[...] + y, gf_ref, wg_ref, wu_ref, wd_ref, arriving)

    i = pl.program_id(0)

    @pl.when(i == 0)
    def _():
        copies = _ffn_weight_copies(wg_hbm, wu_hbm, wd_hbm, wg_ref, wu_ref, wd_ref, sem_ref, layer)
        for copy in copies.values():
            copy.start()
        tile(xa_ref, ca_ref, aa_ref, oa_ref, copies)

    @pl.when((i > 0) & (i < steps_a))
    def _():
        tile(xa_ref, ca_ref, aa_ref, oa_ref)

    @pl.when(i >= steps_a)
    def _():
        tile(xb_ref, cb_ref, ab_ref, ob_ref)


def _outproj_ffn(group_a, group_b, w_out, i_even, norm_ffn, w_gate, w_up, w_down, layer):
    (xa, ca, aa), (xb, cb, ab) = group_a, group_b
    d = xa.shape[1]
    steps_a, steps_b = xa.shape[0] // TOKEN_TILE, xb.shape[0] // TOKEN_TILE
    row_a = lambda width: pl.BlockSpec((TOKEN_TILE, width), lambda i: (jnp.minimum(i, steps_a - 1), 0))
    row_b = lambda width: pl.BlockSpec((TOKEN_TILE, width), lambda i: (jnp.maximum(i - steps_a, 0), 0))
    return pl.pallas_call(
        functools.partial(_outproj_ffn_body, steps_a=steps_a, layer=layer),
        grid=(steps_a + steps_b,),
        in_specs=[row_a(d), row_a(D_CONV), row_a(D_ATTN), row_b(d), row_b(D_CONV), row_b(D_ATTN),
                  pl.BlockSpec((None, D_CONV + D_ATTN, d), lambda i: (i_even, 0, 0), pipeline_mode=pl.Buffered(1))]
        + _ffn_specs(w_gate, layer, 1),
        out_specs=[row_a(d), row_b(d)],
        out_shape=[jax.ShapeDtypeStruct(xa.shape, F32), jax.ShapeDtypeStruct(xb.shape, F32)],
        scratch_shapes=_ffn_scratch(w_gate),
        compiler_params=_params(1),
        name="outproj_ffn",
    )(xa, ca, aa, xb, cb, ab, w_out, norm_ffn, w_gate, w_up, w_down)


def _depthwise_conv_halves(ext_ref, y_ref, w_ref, b_ref, slab, half_rows):
    lead = CONV_HALO - (CONV_WIDTH - 1)
    rows = 8 * CONV_PHASES
    lanes = slice(slab * LANES, (slab + 1) * LANES)
    taps = [w_ref[k, :, lanes] for k in range(CONV_WIDTH)]

    def chunk(c, carry):
        r0 = c * rows
        acc = [jnp.zeros((16, LANES), F32) for _ in range(CONV_PHASES)]
        for e in range(CONV_WIDTH + CONV_PHASES - 1):
            v = jnp.concatenate([ext_ref[slab, pl.ds(r0 + lead + e, 8, stride=CONV_PHASES), :],
                                 ext_ref[slab, pl.ds(half_rows + r0 + lead + e, 8, stride=CONV_PHASES), :]],
                                axis=0).astype(BF16)
            for ph in range(CONV_PHASES):
                k = e - ph
                if 0 <= k < CONV_WIDTH:
                    acc[ph] = acc[ph] + v.astype(F32) * taps[k].astype(F32)
        bias = jnp.broadcast_to(b_ref[0:1, lanes], (16, LANES))
        for ph in range(CONV_PHASES):
            out = acc[ph] + bias
            y_ref[slab, pl.ds(r0 + ph, 8, stride=CONV_PHASES), :] = out[:8]
            y_ref[slab, pl.ds(half_rows + r0 + ph, 8, stride=CONV_PHASES), :] = out[8:]
        return carry

    lax.fori_loop(0, half_rows // rows, chunk, 0)


def _layernorm_silu_slabs(y_ref, g_ref, beta_ref, o_ref):
    n_slabs = y_ref.shape[0]
    ys = [y_ref[s] for s in range(n_slabs)]
    inv_c = 1.0 / (n_slabs * LANES)
    mu = jnp.sum(sum(ys), axis=-1, keepdims=True) * inv_c
    ycs = [y - mu for y in ys]
    var = jnp.sum(sum(yc * yc for yc in ycs), axis=-1, keepdims=True) * inv_c
    rstd = lax.rsqrt(var + LN_EPS)
    for s, yc in enumerate(ycs):
        lanes = slice(s * LANES, (s + 1) * LANES)
        z = yc * rstd * g_ref[:, lanes] + beta_ref[:, lanes]
        o_ref[:, lanes] = (z * jax.nn.sigmoid(z)).astype(o_ref.dtype)


def _banded_attention(sink_ref, q_ref, kcat, vcat, o_ref, bias_ref, sinkcol_ref, s_ref):
    b = pl.program_id(0)
    i = pl.program_id(1)
    w = WINDOW
    pair = 2 * HEAD_DIM
    n_blocks = q_ref.shape[0] // w
    lane = lax.broadcasted_iota(jnp.int32, (1, pair), 1)
    low = lane < HEAD_DIM

    @pl.when((b == 0) & (i == 0))
    def _():
        row = lax.broadcasted_iota(jnp.int32, (w, 2 * w), 0)
        col = lax.broadcasted_iota(jnp.int32, (w, 2 * w), 1)
        dist = w + row - col
        valid = (dist >= 0) & (dist < w)
        distf = dist.astype(F32)
        for h in range(N_HEADS):
            slab, half = h // 2, h % 2
            bias_ref[slab * w:(slab + 1) * w, half * 2 * w:(half + 1) * 2 * w] = (
                jnp.where(valid, -ALIBI_SLOPES[h] * distf, NEG_INF))
            sinkcol_ref[half, slab * w:(slab + 1) * w, :] = jnp.full((w, 1), sink_ref[h], F32)

    krot = pltpu.roll(kcat, HEAD_DIM, 1)
    vrot = pltpu.roll(vcat, HEAD_DIM, 1)
    zero = jnp.zeros((), F32)
    k_low = [jnp.where(low, kcat, zero).astype(BF16), jnp.where(low, krot, zero).astype(BF16)]
    k_high = [jnp.where(low, zero, krot).astype(BF16), jnp.where(low, zero, kcat).astype(BF16)]
    v_low = [jnp.where(low, vcat, zero).astype(BF16), jnp.where(low, vrot, zero).astype(BF16)]
    v_high = [jnp.where(low, zero, vrot).astype(BF16), jnp.where(low, zero, vcat).astype(BF16)]
    ones_low = jnp.broadcast_to(jnp.where(low, 1.0, 0.0).astype(BF16), (2 * w, pair))
    ones_high = jnp.broadcast_to(jnp.where(low, 0.0, 1.0).astype(BF16), (2 * w, pair))
    ones_ext = jnp.concatenate([ones_low, ones_high], axis=0)
    col = lax.broadcasted_iota(jnp.int32, (1, 4 * w), 1)
    no_prev = (col % (2 * w) < w) & (i == 0)

    for jb in range(n_blocks):
        rows = slice(jb * w, (jb + 1) * w)
        keys = slice(jb * w, (jb + 2) * w)
        for g in range(N_KV_HEADS):
            q2 = jnp.concatenate([q_ref[rows, (2 * g) * pair:(2 * g + 1) * pair],
                                  q_ref[rows, (2 * g + 1) * pair:(2 * g + 2) * pair]], axis=0)
            kk = jnp.concatenate([k_low[g][keys], k_high[g][keys]], axis=0)
            s_ref[jb, 2 * g * w:(2 * g + 2) * w, :] = lax.dot_general(
                q2, kk, (((1,), (1,)), ((), ())), preferred_element_type=F32)

    for jb in range(n_blocks):
        rows = slice(jb * w, (jb + 1) * w)
        keys = slice(jb * w, (jb + 2) * w)
        bias = bias_ref[...]
        if jb == 0:
            bias = jnp.where(no_prev, NEG_INF, bias)
        s = s_ref[jb] + bias
        ps, sink_terms = [], []
        for half in range(2):
            sh = s[:, half * 2 * w:(half + 1) * 2 * w]
            sink = sinkcol_ref[half]
            m = jnp.maximum(jnp.max(sh, axis=-1, keepdims=True), sink)
            ps.append(jnp.exp(sh - m).astype(BF16))
            sink_terms.append(jnp.exp(sink - m))
        p = jnp.concatenate(ps, axis=1)
        sink_term = jnp.where(low, sink_terms[0], sink_terms[1])
        for g in range(N_KV_HEADS):
            vv = jnp.concatenate([jnp.concatenate([v_low[g][keys], v_high[g][keys]], axis=0), ones_ext], axis=1)
            o = jnp.dot(p[2 * g * w:(2 * g + 2) * w], vv, preferred_element_type=F32)
            o = o[:, :pair] / (o[:, pair:] + sink_term[2 * g * w:(2 * g + 2) * w])
            o_ref[rows, (2 * g) * pair:(2 * g + 1) * pair] = o[:w].astype(o_ref.dtype)
            o_ref[rows, (2 * g + 1) * pair:(2 * g + 2) * pair] = o[w:].astype(o_ref.dtype)


def _attn_prompt_body(sink_ref, q_ref, kc_ref, kp_ref, vc_ref, vp_ref, o_ref, bias_ref, sinkcol_ref, s_ref):
    kcat = jnp.concatenate([kp_ref[...], kc_ref[...]], axis=0)
    vcat = jnp.concatenate([vp_ref[...], vc_ref[...]], axis=0)
    _banded_attention(sink_ref, q_ref, kcat, vcat, o_ref, bias_ref, sinkcol_ref, s_ref)


def _attn_prompt(q, k, v, sinks):
    n, t, _ = q.shape
    tt = ATTN_TIME_TILE
    cur = lambda width: pl.BlockSpec((None, tt, width), lambda b, i: (b, i, 0))
    prev = lambda width: pl.BlockSpec((None, WINDOW, width),
                                      lambda b, i: (b, jnp.maximum(i * (tt // WINDOW) - 1, 0), 0))
    return pl.pallas_call(
        _attn_prompt_body,
        grid=(n, t // tt),
        in_specs=[pl.BlockSpec(memory_space=pltpu.SMEM), cur(D_ATTN), cur(D_KV), prev(D_KV), cur(D_KV), prev(D_KV)],
        out_specs=cur(D_ATTN),
        out_shape=jax.ShapeDtypeStruct((n, t, D_ATTN), BF16),
        scratch_shapes=[pltpu.VMEM((N_HEADS // 2 * WINDOW, 4 * WINDOW), F32),
                        pltpu.VMEM((2, N_HEADS // 2 * WINDOW, 1), F32),
                        pltpu.VMEM((tt // WINDOW, N_HEADS // 2 * WINDOW, 4 * WINDOW), F32)],
        compiler_params=_params(2),
        name="attn_prompt",
    )(sinks, q, k, k, v, v)


def _inproj_conv_prompt_body(x_ref, g_ref, w_ref, qn_ref, kn_ref, wt_ref, b_ref, lg_ref, lb_ref,
                             conv_ref, q_ref, k_ref, v_ref, glu_tail_ref, ext_ref, y_ref):
    tt = x_ref.shape[0]
    n_slabs = D_CONV // LANES
    i = pl.program_id(1)

    @pl.when(i == 0)
    def _():
        ext_ref[:, 0:CONV_HALO, :] = jnp.zeros((n_slabs, CONV_HALO, LANES), F32)

    @pl.when(i > 0)
    def _():
        ext_ref[:, 0:CONV_HALO, :] = ext_ref[:, tt:tt + CONV_HALO, :]

    h = _rmsnorm(x_ref[...], g_ref[...]).astype(BF16)
    o_q = 2 * D_CONV
    ua = jnp.dot(h, w_ref[:, o_q:], preferred_element_type=F32)
    uc = jnp.dot(h, w_ref[:, :o_q], preferred_element_type=F32)
    q_ref[...] = (_head_rmsnorm(ua[:, :D_ATTN], qn_ref[...]) * (HEAD_DIM ** -0.5)).astype(BF16)
    k_ref[...] = _head_rmsnorm(ua[:, D_ATTN:D_ATTN + D_KV], kn_ref[...])
    v_ref[...] = ua[:, D_ATTN + D_KV:]
    glu = uc[:, :D_CONV] * jax.nn.sigmoid(uc[:, D_CONV:])
    glu_tail_ref[...] = glu[tt - CONV_HALO:, :]
    for s in range(n_slabs):
        ext_ref[s, CONV_HALO:, :] = glu[:, s * LANES:(s + 1) * LANES]
    for s in range(n_slabs):
        _depthwise_conv_halves(ext_ref, y_ref, wt_ref, b_ref, s, tt // 2)
    _layernorm_silu_slabs(y_ref, lg_ref, lb_ref, conv_ref)


def _inproj_conv_prompt(x, norm_g, w_in, q_gain, k_gain, w_taps, b_dw, ln_g, ln_b, layer, i_even):
    n, t, d = x.shape
    n_in = w_in.shape[2]
    tt = INPROJ_CONV_TIME_TILE
    tile = lambda width: pl.BlockSpec((None, tt, width), lambda b, i: (b, i, 0))
    even = lambda width: pl.BlockSpec((None, 1, width), lambda b, i: (i_even, 0, 0))
    return pl.pallas_call(
        _inproj_conv_prompt_body,
        grid=(n, t // tt),
        in_specs=[tile(d),
                  pl.BlockSpec((None, 1, d), lambda b, i: (layer, 0, 0)),
                  pl.BlockSpec((None, d, n_in), lambda b, i: (i_even, 0, 0), pipeline_mode=pl.Buffered(1)),
                  even(D_ATTN), even(D_KV),
                  pl.BlockSpec((None, CONV_WIDTH, 16, D_CONV), lambda b, i: (i_even, 0, 0, 0)),
                  even(D_CONV), even(D_CONV), even(D_CONV)],
        out_specs=[tile(D_CONV), tile(D_ATTN), tile(D_KV), tile(D_KV),
                   pl.BlockSpec((None, CONV_HALO, D_CONV), lambda b, i: (b, 0, 0))],
        out_shape=[jax.ShapeDtypeStruct((n, t, D_CONV), BF16), jax.ShapeDtypeStruct((n, t, D_ATTN), BF16),
                   jax.ShapeDtypeStruct((n, t, D_KV), F32), jax.ShapeDtypeStruct((n, t, D_KV), F32),
                   jax.ShapeDtypeStruct((n, CONV_HALO, D_CONV), F32)],
        scratch_shapes=[pltpu.VMEM((D_CONV // LANES, CONV_HALO + tt, LANES), F32),
                        pltpu.VMEM((D_CONV // LANES, tt, LANES), F32)],
        compiler_params=_params(2),
        name="inproj_conv_prompt",
    )(x, norm_g, w_in, q_gain, k_gain, w_taps, b_dw, ln_g, ln_b)


def _window_sums(vals, w, n_out):
    shared = list(range(n_out - 1, w))
    base = None
    for e in shared:
        base = vals[e] if base is None else base + vals[e]
    sums = []
    for i in range(n_out):
        acc = base
        for e in range(i, i + w):
            if e not in shared:
                acc = vals[e] if acc is None else acc + vals[e]
        sums.append(acc)
    return sums


def _pool_deltas_strided(ext_ref, d_ref, pos0, n_rows):
    n_slabs = ext_ref.shape[0]
    slabs_per_group = n_slabs // len(POOL_WINDOWS)
    rows = 8 * POOL_PHASES

    def chunk(c, carry):
        r0 = c * rows
        sub = lax.broadcasted_iota(jnp.int32, (8, LANES), 0)
        pos = [pos0 + r0 + ph + POOL_PHASES * sub for ph in range(POOL_PHASES)]
        inv = {w: [1.0 / jnp.minimum(w, p + 1).astype(F32) for p in pos] for w in POOL_WINDOWS}
        for slab in range(n_slabs):
            w = POOL_WINDOWS[slab // slabs_per_group]
            first = POOL_HALO - (w - 1)
            vals = [ext_ref[slab, pl.ds(r0 + first + e, 8, stride=POOL_PHASES), :] for e in range(w + POOL_PHASES - 1)]
            sums = _window_sums(vals, w, POOL_PHASES)
            for ph in range(POOL_PHASES):
                d_ref[slab, pl.ds(r0 + ph, 8, stride=POOL_PHASES), :] = sums[ph] * inv[w][ph] - vals[ph + w - 1]
        return carry

    lax.fori_loop(0, n_rows // rows, chunk, 0, unroll=2)


def _pool_ffn_prompt_body(x_ref, g_ref, w_ref, sc_ref, gf_ref, wg_hbm, wu_hbm, wd_hbm, o_ref, tail_ref, ext_ref, d_ref,
                          wg_ref, wu_ref, wd_ref, sem_ref, *, layer):
    tt, d = x_ref.shape
    n_slabs = d // LANES
    slabs_per_group = n_slabs // len(POOL_WINDOWS)
    i = pl.program_id(1)
    first_step = (pl.program_id(0) == 0) & (i == 0)
    weight_copies = functools.partial(_ffn_weight_copies, wg_hbm, wu_hbm, wd_hbm, wg_ref, wu_ref, wd_ref, sem_ref, layer)

    @pl.when(first_step)
    def _():
        for copy in weight_copies().values():
            copy.start()

    @pl.when(i == 0)
    def _():
        ext_ref[:, 0:POOL_HALO, :] = jnp.zeros((n_slabs, POOL_HALO, LANES), F32)

    @pl.when(i > 0)
    def _():
        ext_ref[:, 0:POOL_HALO, :] = ext_ref[:, tt:tt + POOL_HALO, :]

    x = x_ref[...]
    xn = _rmsnorm(x, g_ref[...])
    for s in range(n_slabs):
        ext_ref[s, POOL_HALO:, :] = xn[:, s * LANES:(s + 1) * LANES]
    tail_ref[...] = xn[tt - POOL_HALO:, :]
    _pool_deltas_strided(ext_ref, d_ref, i * tt, tt)
    ys = []
    for g in range(len(POOL_WINDOWS)):
        dg = jnp.concatenate([d_ref[s] for s in range(g * slabs_per_group, (g + 1) * slabs_per_group)], axis=-1)
        ys.append(jnp.dot(dg.astype(BF16), w_ref[g], preferred_element_type=F32))
    x1 = x + jnp.concatenate(ys, axis=-1) * sc_ref[...]

    @pl.when(first_step)
    def _():
        o_ref[...] = _ffn_apply(x1, gf_ref, wg_ref, wu_ref, wd_ref, weight_copies())

    @pl.when(jnp.logical_not(first_step))
    def _():
        o_ref[...] = _ffn_apply(x1, gf_ref, wg_ref, wu_ref, wd_ref)


def _pool_ffn_prompt(x, norm_g, w_pool, pool_scale, layer, i_odd, norm_ffn, w_gate, w_up, w_down):
    n, t, d = x.shape
    tt = POOL_TIME_TILE
    ng, gd, _ = w_pool.shape[1:]
    tile = pl.BlockSpec((None, tt, d), lambda b, i: (b, i, 0))
    return pl.pallas_call(
        functools.partial(_pool_ffn_prompt_body, layer=layer),
        grid=(n, t // tt),
        in_specs=[tile,
                  pl.BlockSpec((None, 1, d), lambda b, i: (layer, 0, 0)),
                  pl.BlockSpec((None, ng, gd, gd), lambda b, i: (i_odd, 0, 0, 0)),
                  pl.BlockSpec((None, 1, d), lambda b, i: (i_odd, 0, 0))] + _ffn_specs(w_gate, layer, 2),
        out_specs=[tile, pl.BlockSpec((None, POOL_HALO, d), lambda b, i: (b, 0, 0))],
        out_shape=[jax.ShapeDtypeStruct((n, t, d), F32), jax.ShapeDtypeStruct((n, POOL_HALO, d), F32)],
        scratch_shapes=[pltpu.VMEM((d // LANES, POOL_HALO + tt, LANES), F32), pltpu.VMEM((d // LANES, tt, LANES), F32)]
        + _ffn_scratch(w_gate),
        compiler_params=_params(2),
        name="pool_ffn_prompt",
    )(x, norm_g, w_pool, pool_scale, norm_ffn, w_gate, w_up, w_down)


def _grown_state(prev, slab_dims, tile_axis, nb):
    layers = 0 if prev is None else prev.shape[0]

    def spec(n_layers):
        block = list(slab_dims)
        block[tile_axis] = nb

        def index(b):
            idx = [0] * (len(slab_dims) + 1)
            idx[tile_axis + 1] = b
            return tuple(idx)

        return pl.BlockSpec((n_layers, *block), index)

    ins, in_specs = ([], []) if prev is None else ([prev], [spec(layers)])
    return ins, in_specs, spec(layers + 1), jax.ShapeDtypeStruct((layers + 1, *slab_dims), F32)


def _copy_earlier_layers(new_ref, prev_refs):
    for prev_ref in prev_refs:
        for l in range(prev_ref.shape[0]):
            new_ref[l] = prev_ref[l]


def _conv_sample_body(cache_ref, glu_ref, w_ref, b_ref, g_ref, beta_ref, *refs):
    o_ref, all_new_ref = refs[-2:]
    _copy_earlier_layers(all_new_ref, refs[:-2])
    this_layer = all_new_ref.shape[0] - 1
    hist = cache_ref.shape[0]
    t = glu_ref.shape[0]
    ext = lambda j: cache_ref[j] if j < hist else glu_ref[j - hist]
    rounded = lambda a: a.astype(BF16).astype(F32)
    for j in range(hist):
        all_new_ref[this_layer, j] = ext(t + j)
    for step in range(t):
        acc = rounded(w_ref[0:1, :]) * rounded(ext(step)) + b_ref[...]
        for k in range(1, CONV_WIDTH):
            acc = acc + rounded(w_ref[k:k + 1, :]) * rounded(ext(step + k))
        o_ref[step] = _layernorm_silu(acc, g_ref[...], beta_ref[...]).astype(o_ref.dtype)


def _conv_sample(cache_tm, glu_tm, w_dw, b_dw, ln_g, ln_b, i_even, prev_new):
    _, hist, n, c = cache_tm.shape
    t = glu_tm.shape[0]
    nb = SAMPLE_SEQ_TILE
    vec = pl.BlockSpec((None, 1, c), lambda b: (i_even, 0, 0))
    prev_in, prev_specs, new_spec, new_shape = _grown_state(prev_new, (hist, n, c), 1, nb)
    return pl.pallas_call(
        _conv_sample_body,
        grid=(n // nb,),
        in_specs=[pl.BlockSpec((None, hist, nb, c), lambda b: (i_even, 0, b, 0)),
                  pl.BlockSpec((t, nb, c), lambda b: (0, b, 0)),
                  pl.BlockSpec((None, CONV_WIDTH, c), lambda b: (i_even, 0, 0)),
                  vec, vec, vec] + prev_specs,
        out_specs=[pl.BlockSpec((t, nb, c), lambda b: (0, b, 0)), new_spec],
        out_shape=[jax.ShapeDtypeStruct((t, n, c), BF16), new_shape],
        compiler_params=_params(1),
        name="conv_sample",
    )(cache_tm, glu_tm, w_dw, b_dw, ln_g, ln_b, *prev_in)


def _attn_sample_body(sink_ref, q_ref, kc_ref, kn_ref, vc_ref, vn_ref, *refs):
    o_ref, all_knew_ref, all_vnew_ref = refs[-3:]
    _copy_earlier_layers(all_knew_ref, refs[:-3][0:1])
    _copy_earlier_layers(all_vnew_ref, refs[:-3][1:2])
    knew_ref = all_knew_ref.at[all_knew_ref.shape[0] - 1]
    vnew_ref = all_vnew_ref.at[all_vnew_ref.shape[0] - 1]
    nb, nkv, rows, hd = q_ref.shape
    buf = kc_ref.shape[-1]
    t = kn_ref.shape[-1]
    batch = nb * nkv
    q = q_ref[...].reshape(batch, rows, hd)
    kc = kc_ref[...].reshape(batch, hd, buf)
    kn = kn_ref[...].reshape(batch, hd, t)
    vc = vc_ref[...].reshape(batch, hd, buf)
    vn = vn_ref[...].reshape(batch, hd, t)

    lane = lax.broadcasted_iota(jnp.int32, (1, 1, buf), 2)
    pad = jnp.zeros((batch, hd, buf - t), F32)
    k_new = jnp.where(lane < buf - t, pltpu.roll(kc, buf - t, 2), jnp.concatenate([pad, kn], axis=-1))
    v_new = jnp.where(lane < buf - t, pltpu.roll(vc, buf - t, 2), jnp.concatenate([pad, vn], axis=-1))
    knew_ref[...] = k_new.reshape(nb, nkv, hd, buf)
    vnew_ref[...] = v_new.reshape(nb, nkv, hd, buf)

    s_c = jnp.einsum("bqd,bdk->bqk", q, kc.astype(BF16), preferred_element_type=F32)
    s_n = jnp.einsum("bqd,bdk->bqk", q, kn.astype(BF16), preferred_element_type=F32)
    step = lax.broadcasted_iota(jnp.int32, (rows, 1), 0) % t
    head_in_group = lax.broadcasted_iota(jnp.int32, (rows, 1), 0) // t
    dist_c = buf + step - lax.broadcasted_iota(jnp.int32, (rows, buf), 1)
    dist_n = step - lax.broadcasted_iota(jnp.int32, (rows, t), 1)
    s_c = s_c.reshape(nb, nkv, rows, buf)
    s_n = s_n.reshape(nb, nkv, rows, t)
    ps_c, ps_n, invs = [], [], []
    for kv in range(nkv):
        slope = jnp.zeros((rows, 1), F32)
        sink = jnp.zeros((rows, 1), F32)
        for j in range(GQA_GROUP):
            slope = jnp.where(head_in_group == j, ALIBI_SLOPES[kv * GQA_GROUP + j], slope)
            sink = jnp.where(head_in_group == j, sink_ref[kv * GQA_GROUP + j], sink)
        sc = jnp.where((dist_c >= 0) & (dist_c < WINDOW), s_c[:, kv] - slope * dist_c.astype(F32), NEG_INF)
        sn = jnp.where((dist_n >= 0) & (dist_n < WINDOW), s_n[:, kv] - slope * dist_n.astype(F32), NEG_INF)
        m = jnp.maximum(jnp.maximum(jnp.max(sc, axis=-1, keepdims=True), jnp.max(sn, axis=-1, keepdims=True)), sink)
        pc = jnp.exp(sc - m)
        pn = jnp.exp(sn - m)
        denom = jnp.sum(pc, axis=-1, keepdims=True) + jnp.sum(pn, axis=-1, keepdims=True) + jnp.exp(sink - m)
        ps_c.append(pc.astype(BF16))
        ps_n.append(pn.astype(BF16))
        invs.append(1.0 / denom)
    p_c = jnp.stack(ps_c, axis=1).reshape(batch, rows, buf)
    p_n = jnp.stack(ps_n, axis=1).reshape(batch, rows, t)
    inv = jnp.stack(invs, axis=1).reshape(batch, rows, 1)
    o = jnp.einsum("bqk,bdk->bqd", p_c, vc.astype(BF16), preferred_element_type=F32)
    o = o + jnp.einsum("bqk,bdk->bqd", p_n, vn.astype(BF16), preferred_element_type=F32)
    o_ref[...] = (o * inv).reshape(nb, nkv, rows, hd).astype(o_ref.dtype)


def _attn_sample(q, kc, kn, vc, vn, sinks, i_even, prev_new):
    n, nkv, rows, hd = q.shape
    buf = kc.shape[-1]
    t = kn.shape[-1]
    nb = SAMPLE_SEQ_TILE
    cache = pl.BlockSpec((None, nb, nkv, hd, buf), lambda b: (i_even, b, 0, 0, 0))
    new = pl.BlockSpec((nb, nkv, hd, t), lambda b: (b, 0, 0, 0))
    qspec = pl.BlockSpec((nb, nkv, rows, hd), lambda b: (b, 0, 0, 0))
    prev_k, prev_v = (None, None) if prev_new is None else prev_new
    k_in, k_specs, k_spec, k_shape = _grown_state(prev_k, (n, nkv, hd, buf), 0, nb)
    v_in, v_specs, v_spec, v_shape = _grown_state(prev_v, (n, nkv, hd, buf), 0, nb)
    return pl.pallas_call(
        _attn_sample_body,
        grid=(n // nb,),
        in_specs=[pl.BlockSpec(memory_space=pltpu.SMEM), qspec, cache, new, cache, new] + k_specs + v_specs,
        out_specs=[qspec, k_spec, v_spec],
        out_shape=[jax.ShapeDtypeStruct((n, nkv, rows, hd), BF16), k_shape, v_shape],
        compiler_params=_params(1),
        name="attn_sample",
    )(sinks, q, kc, kn, vc, vn, *k_in, *v_in)


def _pool_ffn_sample_body(x_ref, st_ref, g_ref, w_ref, sc_ref, gf_ref, wg_hbm, wu_hbm, wd_hbm, *refs, pos0, layer):
    o_ref, all_new_ref, wg_ref, wu_ref, wd_ref, sem_ref = refs[-6:]
    first_step = pl.program_id(0) == 0
    weight_copies = functools.partial(_ffn_weight_copies, wg_hbm, wu_hbm, wd_hbm, wg_ref, wu_ref, wd_ref, sem_ref, layer)

    @pl.when(first_step)
    def _():
        for copy in weight_copies().values():
            copy.start()

    _copy_earlier_layers(all_new_ref, refs[:-6])
    new_ref = all_new_ref.at[all_new_ref.shape[0] - 1]
    t, nb, d = x_ref.shape
    hist = st_ref.shape[0]
    gdim = d // len(POOL_WINDOWS)
    x = x_ref[...].reshape(t * nb, d)
    xn = _rmsnorm(x, g_ref[...])
    ext = lambda j: st_ref[j] if j < hist else xn[(j - hist) * nb:(j - hist + 1) * nb]
    for j in range(hist):
        new_ref[j] = ext(t + j)
    ys = []
    for g, w in enumerate(POOL_WINDOWS):
        lanes = slice(g * gdim, (g + 1) * gdim)
        steps = []
        for step in range(t):
            acc = ext(hist + step)[:, lanes]
            for j in range(1, w):
                acc = acc + ext(hist + step - j)[:, lanes]
            cnt = float(min(w, pos0 + step + 1))
            steps.append(acc / cnt - xn[step * nb:(step + 1) * nb, lanes])
        dg = jnp.concatenate(steps, axis=0)
        ys.append(jnp.dot(dg.astype(BF16), w_ref[g], preferred_element_type=F32))
    x1 = x + jnp.concatenate(ys, axis=-1) * sc_ref[...]

    @pl.when(first_step)
    def _():
        o_ref[...] = _ffn_apply(x1, gf_ref, wg_ref, wu_ref, wd_ref, weight_copies()).reshape(t, nb, d)

    @pl.when(jnp.logical_not(first_step))
    def _():
        o_ref[...] = _ffn_apply(x1, gf_ref, wg_ref, wu_ref, wd_ref).reshape(t, nb, d)


def _pool_ffn_sample(x_tm, state_tm, norm_g, w_pool, pool_scale, layer, i_odd, pos0, norm_ffn, w_gate, w_up, w_down,
                     prev_new):
    t, n, d = x_tm.shape
    hist = state_tm.shape[1]
    nb = SAMPLE_FFN_SEQ_TILE
    ng, gd, _ = w_pool.shape[1:]
    prev_in, prev_specs, new_spec, new_shape = _grown_state(prev_new, (hist, n, d), 1, nb)
    return pl.pallas_call(
        functools.partial(_pool_ffn_sample_body, pos0=pos0, layer=layer),
        grid=(n // nb,),
        in_specs=[pl.BlockSpec((t, nb, d), lambda b: (0, b, 0)),
                  pl.BlockSpec((None, hist, nb, d), lambda b: (i_odd, 0, b, 0)),
                  pl.BlockSpec((None, 1, d), lambda b: (layer, 0, 0)),
                  pl.BlockSpec((None, ng, gd, gd), lambda b: (i_odd, 0, 0, 0)),
                  pl.BlockSpec((None, 1, d), lambda b: (i_odd, 0, 0))] + _ffn_specs(w_gate, layer, 1) + prev_specs,
        out_specs=[pl.BlockSpec((t, nb, d), lambda b: (0, b, 0)), new_spec],
        out_shape=[jax.ShapeDtypeStruct((t, n, d), F32), new_shape],
        scratch_shapes=_ffn_scratch(w_gate),
        compiler_params=_params(1),
        name="pool_ffn_sample",
    )(x_tm, state_tm, norm_g, w_pool, pool_scale, norm_ffn, w_gate, w_up, w_down, *prev_in)


def _prepare_weights(norm_mix, norm_ffn, w_in, q_norm, k_norm, sinks, w_dw, b_dw, conv_norm_g, conv_norm_b, w_out,
                     w_pool, pool_scale, w_gate, w_up, w_down):
    vec = lambda a: a[:, None, :]
    return dict(
        norm_mix=vec(norm_mix), norm_ffn=vec(norm_ffn), w_in=w_in.astype(BF16),
        q_gain=vec(jnp.tile(q_norm, (1, N_HEADS))), k_gain=vec(jnp.tile(k_norm, (1, N_KV_HEADS))), sinks=sinks,
        w_dw=w_dw, b_dw=vec(b_dw),
        w_taps=jnp.broadcast_to(w_dw.astype(BF16)[:, :, None, :], w_dw.shape[:2] + (16, w_dw.shape[2])),
        ln_g=vec(conv_norm_g), ln_b=vec(conv_norm_b), w_out=w_out.astype(BF16),
        w_pool=w_pool.astype(BF16), pool_scale=vec(pool_scale),
        w_gate=w_gate.astype(BF16), w_up=w_up.astype(BF16), w_down=w_down.astype(BF16))


def _trunks(x_prompt, x_sample, conv_bufs, k_bufs, v_bufs, pool_bufs, pos0, p, depth):
    n, t, d = x_prompt.shape
    ns, ts, _ = x_sample.shape
    xp = x_prompt.reshape(n * t, d)
    x_tm = x_sample.transpose(1, 0, 2)
    conv_tm = conv_bufs.transpose(0, 2, 1, 3)
    pool_tm = pool_bufs.transpose(0, 2, 1, 3)
    k_t = k_bufs.transpose(0, 1, 3, 4, 2)
    v_t = v_bufs.transpose(0, 1, 3, 4, 2)
    ffn = lambda layer: (p["norm_ffn"], p["w_gate"], p["w_up"], p["w_down"], layer)
    new_conv, new_k, new_v, new_pool = [], [], [], []
    conv_new = kv_new = pool_new = None
    for layer in range(depth):
        i = layer // 2
        if layer % 2 == 0:
            conv_p, q, k, v, glu_tail = _inproj_conv_prompt(
                xp.reshape(n, t, d), p["norm_mix"], p["w_in"], p["q_gain"], p["k_gain"], p["w_taps"], p["b_dw"],
                p["ln_g"], p["ln_b"], layer, i)
            attn_p = _attn_prompt(q, k, v, p["sinks"][i])
            keep = min(WINDOW, t)
            new_conv.append(glu_tail[:, CONV_HALO - (CONV_WIDTH - 1):])
            new_k.append(k[:, t - keep:].reshape(n, keep, N_KV_HEADS, HEAD_DIM))
            new_v.append(v[:, t - keep:].reshape(n, keep, N_KV_HEADS, HEAD_DIM))

            xs = x_tm.reshape(ts * ns, d)
            glu, q, k, v = _inproj(xs, p["norm_mix"], p["w_in"], p["q_gain"], p["k_gain"], layer, i)
            conv_s, conv_new = _conv_sample(conv_tm, glu.reshape(ts, ns, D_CONV), p["w_dw"], p["b_dw"], p["ln_g"],
                                            p["ln_b"], i, conv_new)
            qh = q.reshape(ts, ns, N_KV_HEADS, GQA_GROUP, HEAD_DIM).transpose(1, 2, 3, 0, 4)
            qh = qh.reshape(ns, N_KV_HEADS, GQA_GROUP * ts, HEAD_DIM)
            kn = k.reshape(ts, ns, N_KV_HEADS, HEAD_DIM).transpose(1, 2, 3, 0)
            vn = v.reshape(ts, ns, N_KV_HEADS, HEAD_DIM).transpose(1, 2, 3, 0)
            oh, *kv_new = _attn_sample(qh, k_t, kn, v_t, vn, p["sinks"][i], i, kv_new)
            attn_s = oh.reshape(ns, N_KV_HEADS, GQA_GROUP, ts, HEAD_DIM).transpose(3, 0, 1, 2, 4)

            xp, xs = _outproj_ffn((xp, conv_p.reshape(n * t, -1), attn_p.reshape(n * t, -1)),
                                  (xs, conv_s.reshape(ts * ns, D_CONV), attn_s.reshape(ts * ns, D_ATTN)),
                                  p["w_out"], i, *ffn(layer))
            x_tm = xs.reshape(ts, ns, d)
        else:
            y, tail = _pool_ffn_prompt(xp.reshape(n, t, d), p["norm_mix"], p["w_pool"], p["pool_scale"], layer, i,
                                       *ffn(layer)[:-1])
            xp = y.reshape(n * t, d)
            new_pool.append(tail[:, POOL_HALO - POOL_BUF:])
            x_tm, pool_new = _pool_ffn_sample(x_tm, pool_tm, p["norm_mix"], p["w_pool"], p["pool_scale"], layer, i, pos0,
                                              *ffn(layer)[:-1], pool_new)
    return (xp.reshape(n, t, d), x_tm.transpose(1, 0, 2),
            jnp.stack(new_conv), jnp.stack(new_k), jnp.stack(new_v), jnp.stack(new_pool),
            conv_new.transpose(0, 2, 1, 3), kv_new[0].transpose(0, 1, 4, 2, 3), kv_new[1].transpose(0, 1, 4, 2, 3),
            pool_new.transpose(0, 2, 1, 3))


def kernel(x_prompt, x_sample, cache_conv, cache_k, cache_v, state_pool, norm_mix, norm_ffn, w_in, q_norm, k_norm,
           sinks, w_dw, b_dw, conv_norm_g, conv_norm_b, w_out, w_pool, pool_scale, w_gate, w_up, w_down):
    depth = norm_mix.shape[0]
    p = _prepare_weights(norm_mix, norm_ffn, w_in, q_norm, k_norm, sinks, w_dw, b_dw, conv_norm_g, conv_norm_b, w_out,
                         w_pool, pool_scale, w_gate, w_up, w_down)
    return _trunks(x_prompt, x_sample, cache_conv, cache_k, cache_v, state_pool, PAST_LEN, p, depth)
```

```python
import functools

import jax
import jax.numpy as jnp
from jax import lax
from jax.experimental import pallas as pl
from jax.experimental.pallas import tpu as pltpu

F32 = jnp.float32
BF16 = jnp.bfloat16

HEAD_DIM = 64
N_HEADS = 8
N_KV_HEADS = 2
GQA_GROUP = N_HEADS // N_KV_HEADS
D_ATTN = N_HEADS * HEAD_DIM
D_KV = N_KV_HEADS * HEAD_DIM
WINDOW = 128
PAST_LEN = 8192
D_CONV = 512
CONV_WIDTH = 31
CONV_HALO = 32
POOL_WINDOWS = (2, 4, 8, 16)
POOL_BUF = max(POOL_WINDOWS) - 1
POOL_HALO = 16
RMS_EPS = 1e-6
LN_EPS = 1e-5
NEG_INF = -1e30
LANES = 128
ALIBI_SLOPES = tuple(2.0 ** (-8.0 * (h + 1) / N_HEADS) for h in range(N_HEADS))

TOKEN_TILE = 512
CONV_PHASES = 4
INPROJ_CONV_TIME_TILE = 1024
ATTN_TIME_TILE = 2048
POOL_TIME_TILE = 512
POOL_PHASES = 4
FFN_CHUNK = 1536
SAMPLE_SEQ_TILE = 32
SAMPLE_FFN_SEQ_TILE = 32
VMEM_LIMIT = 56 * 1024 * 1024


def _params(n_axes):
    return pltpu.CompilerParams(dimension_semantics=("arbitrary",) * n_axes, vmem_limit_bytes=VMEM_LIMIT)


def _rmsnorm(x, g):
    return x * lax.rsqrt(jnp.mean(x * x, axis=-1, keepdims=True) + RMS_EPS) * g


def _head_rmsnorm(x, g):
    lane = lax.broadcasted_iota(jnp.int32, (x.shape[0], LANES), 1)
    low = lane < HEAD_DIM
    outs = []
    for s in range(x.shape[1] // LANES):
        xs = x[:, s * LANES:(s + 1) * LANES]
        sq = xs * xs
        s_low = jnp.sum(jnp.where(low, sq, 0.0), axis=-1, keepdims=True)
        s_high = jnp.sum(jnp.where(low, 0.0, sq), axis=-1, keepdims=True)
        ms = jnp.where(low, s_low, s_high) * (1.0 / HEAD_DIM)
        outs.append(xs * lax.rsqrt(ms + RMS_EPS) * g[:, s * LANES:(s + 1) * LANES])
    return outs[0] if len(outs) == 1 else jnp.concatenate(outs, axis=-1)


def _layernorm_silu(y, g, b):
    mu = jnp.mean(y, axis=-1, keepdims=True)
    yc = y - mu
    var = jnp.mean(yc * yc, axis=-1, keepdims=True)
    z = yc * lax.rsqrt(var + LN_EPS) * g + b
    return z * jax.nn.sigmoid(z)


def _ffn_apply(x, g_ref, wg_ref, wu_ref, wd_ref):
    h = _rmsnorm(x, g_ref[...]).astype(BF16)
    acc = x
    d_ff = wg_ref.shape[1]
    for c0 in range(0, d_ff, FFN_CHUNK):
        c1 = min(c0 + FFN_CHUNK, d_ff)
        gate = jnp.dot(h, wg_ref[:, c0:c1], preferred_element_type=F32)
        up = jnp.dot(h, wu_ref[:, c0:c1], preferred_element_type=F32)
        a = (gate * jax.nn.sigmoid(gate) * up).astype(BF16)
        acc = acc + jnp.dot(a, wd_ref[c0:c1, :], preferred_element_type=F32)
    return acc


def _ffn_specs(w_gate, layer, n_axes):
    _, d, f = w_gate.shape
    at_layer = {1: lambda i: (layer, 0, 0), 2: lambda b, i: (layer, 0, 0)}[n_axes]
    resident = functools.partial(pl.BlockSpec, pipeline_mode=pl.Buffered(1))
    return [pl.BlockSpec((None, 1, d), at_layer), resident((None, d, f), at_layer), resident((None, d, f), at_layer),
            resident((None, f, d), at_layer)]


def _outproj_ffn_body(xa_ref, ca_ref, aa_ref, xb_ref, cb_ref, ab_ref, w_ref, gf_ref, wg_ref, wu_ref, wd_ref,
                      oa_ref, ob_ref, *, steps_a):
    def tile(x_ref, c_ref, a_ref, o_ref):
        y = jnp.dot(c_ref[...], w_ref[:D_CONV, :], preferred_element_type=F32)
        y = y + jnp.dot(a_ref[...], w_ref[D_CONV:, :], preferred_element_type=F32)
        o_ref[...] = _ffn_apply(x_ref[...] + y, gf_ref, wg_ref, wu_ref, wd_ref)

    i = pl.program_id(0)

    @pl.when(i < steps_a)
    def _():
        tile(xa_ref, ca_ref, aa_ref, oa_ref)

    @pl.when(i >= steps_a)
    def _():
        tile(xb_ref, cb_ref, ab_ref, ob_ref)


def _outproj_ffn(group_a, group_b, w_out, i_even, norm_ffn, w_gate, w_up, w_down, layer):
    (xa, ca, aa), (xb, cb, ab) = group_a, group_b
    d = xa.shape[1]
    steps_a, steps_b = xa.shape[0] // TOKEN_TILE, xb.shape[0] // TOKEN_TILE
    row_a = lambda width: pl.BlockSpec((TOKEN_TILE, width), lambda i: (jnp.minimum(i, steps_a - 1), 0))
    row_b = lambda width: pl.BlockSpec((TOKEN_TILE, width), lambda i: (jnp.maximum(i - steps_a, 0), 0))
    return pl.pallas_call(
        functools.partial(_outproj_ffn_body, steps_a=steps_a),
        grid=(steps_a + steps_b,),
        in_specs=[row_a(d), row_a(D_CONV), row_a(D_ATTN), row_b(d), row_b(D_CONV), row_b(D_ATTN),
                  pl.BlockSpec((None, D_CONV + D_ATTN, d), lambda i: (i_even, 0, 0), pipeline_mode=pl.Buffered(1))]
        + _ffn_specs(w_gate, layer, 1),
        out_specs=[row_a(d), row_b(d)],
        out_shape=[jax.ShapeDtypeStruct(xa.shape, F32), jax.ShapeDtypeStruct(xb.shape, F32)],
        compiler_params=_params(1),
        name="outproj_ffn",
    )(xa, ca, aa, xb, cb, ab, w_out, norm_ffn, w_gate, w_up, w_down)


def _depthwise_conv_halves(ext_ref, y_ref, w_ref, b_ref, slab, half_rows):
    lead = CONV_HALO - (CONV_WIDTH - 1)
    rows = 8 * CONV_PHASES
    lanes = slice(slab * LANES, (slab + 1) * LANES)
    taps = [w_ref[k, :, lanes] for k in range(CONV_WIDTH)]

    def chunk(c, carry):
        r0 = c * rows
        acc = [jnp.zeros((16, LANES), F32) for _ in range(CONV_PHASES)]
        for e in range(CONV_WIDTH + CONV_PHASES - 1):
            v = jnp.concatenate([ext_ref[slab, pl.ds(r0 + lead + e, 8, stride=CONV_PHASES), :],
                                 ext_ref[slab, pl.ds(half_rows + r0 + lead + e, 8, stride=CONV_PHASES), :]],
                                axis=0).astype(BF16)
            for ph in range(CONV_PHASES):
                k = e - ph
                if 0 <= k < CONV_WIDTH:
                    acc[ph] = acc[ph] + v.astype(F32) * taps[k].astype(F32)
        bias = jnp.broadcast_to(b_ref[0:1, lanes], (16, LANES))
        for ph in range(CONV_PHASES):
            out = acc[ph] + bias
            y_ref[slab, pl.ds(r0 + ph, 8, stride=CONV_PHASES), :] = out[:8]
            y_ref[slab, pl.ds(half_rows + r0 + ph, 8, stride=CONV_PHASES), :] = out[8:]
        return carry

    lax.fori_loop(0, half_rows // rows, chunk, 0)


def _layernorm_silu_slabs(y_ref, g_ref, beta_ref, o_ref):
    n_slabs = y_ref.shape[0]
    ys = [y_ref[s] for s in range(n_slabs)]
    inv_c = 1.0 / (n_slabs * LANES)
    mu = jnp.sum(sum(ys), axis=-1, keepdims=True) * inv_c
    ycs = [y - mu for y in ys]
    var = jnp.sum(sum(yc * yc for yc in ycs), axis=-1, keepdims=True) * inv_c
    rstd = lax.rsqrt(var + LN_EPS)
    for s, yc in enumerate(ycs):
        lanes = slice(s * LANES, (s + 1) * LANES)
        z = yc * rstd * g_ref[:, lanes] + beta_ref[:, lanes]
        o_ref[:, lanes] = (z * jax.nn.sigmoid(z)).astype(o_ref.dtype)


def _banded_attention(sink_ref, q_ref, kcat, vcat, o_ref, bias_ref, sinkcol_ref, s_ref):
    b = pl.program_id(0)
    i = pl.program_id(1)
    w = WINDOW
    pair = 2 * HEAD_DIM
    n_blocks = q_ref.shape[0] // w
    lane = lax.broadcasted_iota(jnp.int32, (1, pair), 1)
    low = lane < HEAD_DIM

    @pl.when((b == 0) & (i == 0))
    def _():
        row = lax.broadcasted_iota(jnp.int32, (w, 2 * w), 0)
        col = lax.broadcasted_iota(jnp.int32, (w, 2 * w), 1)
        dist = w + row - col
        valid = (dist >= 0) & (dist < w)
        distf = dist.astype(F32)
        for h in range(N_HEADS):
            slab, half = h // 2, h % 2
            bias_ref[slab * w:(slab + 1) * w, half * 2 * w:(half + 1) * 2 * w] = (
                jnp.where(valid, -ALIBI_SLOPES[h] * distf, NEG_INF))
            sinkcol_ref[half, slab * w:(slab + 1) * w, :] = jnp.full((w, 1), sink_ref[h], F32)

    krot = pltpu.roll(kcat, HEAD_DIM, 1)
    vrot = pltpu.roll(vcat, HEAD_DIM, 1)
    zero = jnp.zeros((), F32)
    k_low = [jnp.where(low, kcat, zero).astype(BF16), jnp.where(low, krot, zero).astype(BF16)]
    k_high = [jnp.where(low, zero, krot).astype(BF16), jnp.where(low, zero, kcat).astype(BF16)]
    v_low = [jnp.where(low, vcat, zero).astype(BF16), jnp.where(low, vrot, zero).astype(BF16)]
    v_high = [jnp.where(low, zero, vrot).astype(BF16), jnp.where(low, zero, vcat).astype(BF16)]
    ones_low = jnp.broadcast_to(jnp.where(low, 1.0, 0.0).astype(BF16), (2 * w, pair))
    ones_high = jnp.broadcast_to(jnp.where(low, 0.0, 1.0).astype(BF16), (2 * w, pair))
    ones_ext = jnp.concatenate([ones_low, ones_high], axis=0)
    col = lax.broadcasted_iota(jnp.int32, (1, 4 * w), 1)
    no_prev = (col % (2 * w) < w) & (i == 0)

    for jb in range(n_blocks):
        rows = slice(jb * w, (jb + 1) * w)
        keys = slice(jb * w, (jb + 2) * w)
        for g in range(N_KV_HEADS):
            q2 = jnp.concatenate([q_ref[rows, (2 * g) * pair:(2 * g + 1) * pair],
                                  q_ref[rows, (2 * g + 1) * pair:(2 * g + 2) * pair]], axis=0)
            kk = jnp.concatenate([k_low[g][keys], k_high[g][keys]], axis=0)
            s_ref[jb, 2 * g * w:(2 * g + 2) * w, :] = lax.dot_general(
                q2, kk, (((1,), (1,)), ((), ())), preferred_element_type=F32)

    for jb in range(n_blocks):
        rows = slice(jb * w, (jb + 1) * w)
        keys = slice(jb * w, (jb + 2) * w)
        bias = bias_ref[...]
        if jb == 0:
            bias = jnp.where(no_prev, NEG_INF, bias)
        s = s_ref[jb] + bias
        ps, sink_terms = [], []
        for half in range(2):
            sh = s[:, half * 2 * w:(half + 1) * 2 * w]
            sink = sinkcol_ref[half]
            m = jnp.maximum(jnp.max(sh, axis=-1, keepdims=True), sink)
            ps.append(jnp.exp(sh - m).astype(BF16))
            sink_terms.append(jnp.exp(sink - m))
        p = jnp.concatenate(ps, axis=1)
        sink_term = jnp.where(low, sink_terms[0], sink_terms[1])
        for g in range(N_KV_HEADS):
            vv = jnp.concatenate([jnp.concatenate([v_low[g][keys], v_high[g][keys]], axis=0), ones_ext], axis=1)
            o = jnp.dot(p[2 * g * w:(2 * g + 2) * w], vv, preferred_element_type=F32)
            o = o[:, :pair] / (o[:, pair:] + sink_term[2 * g * w:(2 * g + 2) * w])
            o_ref[rows, (2 * g) * pair:(2 * g + 1) * pair] = o[:w].astype(o_ref.dtype)
            o_ref[rows, (2 * g + 1) * pair:(2 * g + 2) * pair] = o[w:].astype(o_ref.dtype)


def _attn_prompt_body(sink_ref, q_ref, kc_ref, kp_ref, vc_ref, vp_ref, o_ref, bias_ref, sinkcol_ref, s_ref):
    kcat = jnp.concatenate([kp_ref[...], kc_ref[...]], axis=0)
    vcat = jnp.concatenate([vp_ref[...], vc_ref[...]], axis=0)
    _banded_attention(sink_ref, q_ref, kcat, vcat, o_ref, bias_ref, sinkcol_ref, s_ref)


def _attn_prompt(q, k, v, sinks):
    n, t, _ = q.shape
    tt = ATTN_TIME_TILE
    cur = lambda width: pl.BlockSpec((None, tt, width), lambda b, i: (b, i, 0))
    prev = lambda width: pl.BlockSpec((None, WINDOW, width),
                                      lambda b, i: (b, jnp.maximum(i * (tt // WINDOW) - 1, 0), 0))
    return pl.pallas_call(
        _attn_prompt_body,
        grid=(n, t // tt),
        in_specs=[pl.BlockSpec(memory_space=pltpu.SMEM), cur(D_ATTN), cur(D_KV), prev(D_KV), cur(D_KV), prev(D_KV)],
        out_specs=cur(D_ATTN),
        out_shape=jax.ShapeDtypeStruct((n, t, D_ATTN), BF16),
        scratch_shapes=[pltpu.VMEM((N_HEADS // 2 * WINDOW, 4 * WINDOW), F32),
                        pltpu.VMEM((2, N_HEADS // 2 * WINDOW, 1), F32),
                        pltpu.VMEM((tt // WINDOW, N_HEADS // 2 * WINDOW, 4 * WINDOW), F32)],
        compiler_params=_params(2),
        name="attn_prompt",
    )(sinks, q, k, k, v, v)


def _inproj_conv_prompt_body(x_ref, g_ref, w_ref, qn_ref, kn_ref, wt_ref, b_ref, lg_ref, lb_ref,
                             conv_ref, q_ref, k_ref, v_ref, glu_tail_ref, ext_ref, y_ref):
    tt = x_ref.shape[0]
    n_slabs = D_CONV // LANES
    i = pl.program_id(1)

    @pl.when(i == 0)
    def _():
        ext_ref[:, 0:CONV_HALO, :] = jnp.zeros((n_slabs, CONV_HALO, LANES), F32)

    @pl.when(i > 0)
    def _():
        ext_ref[:, 0:CONV_HALO, :] = ext_ref[:, tt:tt + CONV_HALO, :]

    h = _rmsnorm(x_ref[...], g_ref[...]).astype(BF16)
    o_q = 2 * D_CONV
    ua = jnp.dot(h, w_ref[:, o_q:], preferred_element_type=F32)
    uc = jnp.dot(h, w_ref[:, :o_q], preferred_element_type=F32)
    q_ref[...] = (_head_rmsnorm(ua[:, :D_ATTN], qn_ref[...]) * (HEAD_DIM ** -0.5)).astype(BF16)
    k_ref[...] = _head_rmsnorm(ua[:, D_ATTN:D_ATTN + D_KV], kn_ref[...])
    v_ref[...] = ua[:, D_ATTN + D_KV:]
    glu = uc[:, :D_CONV] * jax.nn.sigmoid(uc[:, D_CONV:])
    glu_tail_ref[...] = glu[tt - CONV_HALO:, :]
    for s in range(n_slabs):
        ext_ref[s, CONV_HALO:, :] = glu[:, s * LANES:(s + 1) * LANES]
    for s in range(n_slabs):
        _depthwise_conv_halves(ext_ref, y_ref, wt_ref, b_ref, s, tt // 2)
    _layernorm_silu_slabs(y_ref, lg_ref, lb_ref, conv_ref)


def _inproj_conv_prompt(x, norm_g, w_in, q_gain, k_gain, w_taps, b_dw, ln_g, ln_b, layer, i_even):
    n, t, d = x.shape
    n_in = w_in.shape[2]
    tt = INPROJ_CONV_TIME_TILE
    tile = lambda width: pl.BlockSpec((None, tt, width), lambda b, i: (b, i, 0))
    even = lambda width: pl.BlockSpec((None, 1, width), lambda b, i: (i_even, 0, 0))
    return pl.pallas_call(
        _inproj_conv_prompt_body,
        grid=(n, t // tt),
        in_specs=[tile(d),
                  pl.BlockSpec((None, 1, d), lambda b, i: (layer, 0, 0)),
                  pl.BlockSpec((None, d, n_in), lambda b, i: (i_even, 0, 0), pipeline_mode=pl.Buffered(1)),
                  even(D_ATTN), even(D_KV),
                  pl.BlockSpec((None, CONV_WIDTH, 16, D_CONV), lambda b, i: (i_even, 0, 0, 0)),
                  even(D_CONV), even(D_CONV), even(D_CONV)],
        out_specs=[tile(D_CONV), tile(D_ATTN), tile(D_KV), tile(D_KV),
                   pl.BlockSpec((None, CONV_HALO, D_CONV), lambda b, i: (b, 0, 0))],
        out_shape=[jax.ShapeDtypeStruct((n, t, D_CONV), BF16), jax.ShapeDtypeStruct((n, t, D_ATTN), BF16),
                   jax.ShapeDtypeStruct((n, t, D_KV), F32), jax.ShapeDtypeStruct((n, t, D_KV), F32),
                   jax.ShapeDtypeStruct((n, CONV_HALO, D_CONV), F32)],
        scratch_shapes=[pltpu.VMEM((D_CONV // LANES, CONV_HALO + tt, LANES), F32),
                        pltpu.VMEM((D_CONV // LANES, tt, LANES), F32)],
        compiler_params=_params(2),
        name="inproj_conv_prompt",
    )(x, norm_g, w_in, q_gain, k_gain, w_taps, b_dw, ln_g, ln_b)


def _window_sums(vals, w, n_out):
    shared = list(range(n_out - 1, w))
    base = None
    for e in shared:
        base = vals[e] if base is None else base + vals[e]
    sums = []
    for i in range(n_out):
        acc = base
        for e in range(i, i + w):
            if e not in shared:
                acc = vals[e] if acc is None else acc + vals[e]
        sums.append(acc)
    return sums


def _pool_deltas_strided(ext_ref, d_ref, pos0, n_rows):
    n_slabs = ext_ref.shape[0]
    slabs_per_group = n_slabs // len(POOL_WINDOWS)
    rows = 8 * POOL_PHASES

    def chunk(c, carry):
        r0 = c * rows
        sub = lax.broadcasted_iota(jnp.int32, (8, LANES), 0)
        pos = [pos0 + r0 + ph + POOL_PHASES * sub for ph in range(POOL_PHASES)]
        inv = {w: [1.0 / jnp.minimum(w, p + 1).astype(F32) for p in pos] for w in POOL_WINDOWS}
        for slab in range(n_slabs):
            w = POOL_WINDOWS[slab // slabs_per_group]
            first = POOL_HALO - (w - 1)
            vals = [ext_ref[slab, pl.ds(r0 + first + e, 8, stride=POOL_PHASES), :] for e in range(w + POOL_PHASES - 1)]
            sums = _window_sums(vals, w, POOL_PHASES)
            for ph in range(POOL_PHASES):
                d_ref[slab, pl.ds(r0 + ph, 8, stride=POOL_PHASES), :] = sums[ph] * inv[w][ph] - vals[ph + w - 1]
        return carry

    lax.fori_loop(0, n_rows // rows, chunk, 0, unroll=2)


def _pool_ffn_prompt_body(x_ref, g_ref, w_ref, sc_ref, gf_ref, wg_ref, wu_ref, wd_ref, o_ref, tail_ref, ext_ref, d_ref):
    tt, d = x_ref.shape
    n_slabs = d // LANES
    slabs_per_group = n_slabs // len(POOL_WINDOWS)
    i = pl.program_id(1)

    @pl.when(i == 0)
    def _():
        ext_ref[:, 0:POOL_HALO, :] = jnp.zeros((n_slabs, POOL_HALO, LANES), F32)

    @pl.when(i > 0)
    def _():
        ext_ref[:, 0:POOL_HALO, :] = ext_ref[:, tt:tt + POOL_HALO, :]

    x = x_ref[...]
    xn = _rmsnorm(x, g_ref[...])
    for s in range(n_slabs):
        ext_ref[s, POOL_HALO:, :] = xn[:, s * LANES:(s + 1) * LANES]
    tail_ref[...] = xn[tt - POOL_HALO:, :]
    _pool_deltas_strided(ext_ref, d_ref, i * tt, tt)
    ys = []
    for g in range(len(POOL_WINDOWS)):
        dg = jnp.concatenate([d_ref[s] for s in range(g * slabs_per_group, (g + 1) * slabs_per_group)], axis=-1)
        ys.append(jnp.dot(dg.astype(BF16), w_ref[g], preferred_element_type=F32))
    x1 = x + jnp.concatenate(ys, axis=-1) * sc_ref[...]
    o_ref[...] = _ffn_apply(x1, gf_ref, wg_ref, wu_ref, wd_ref)


def _pool_ffn_prompt(x, norm_g, w_pool, pool_scale, layer, i_odd, norm_ffn, w_gate, w_up, w_down):
    n, t, d = x.shape
    tt = POOL_TIME_TILE
    ng, gd, _ = w_pool.shape[1:]
    tile = pl.BlockSpec((None, tt, d), lambda b, i: (b, i, 0))
    return pl.pallas_call(
        _pool_ffn_prompt_body,
        grid=(n, t // tt),
        in_specs=[tile,
                  pl.BlockSpec((None, 1, d), lambda b, i: (layer, 0, 0)),
                  pl.BlockSpec((None, ng, gd, gd), lambda b, i: (i_odd, 0, 0, 0)),
                  pl.BlockSpec((None, 1, d), lambda b, i: (i_odd, 0, 0))] + _ffn_specs(w_gate, layer, 2),
        out_specs=[tile, pl.BlockSpec((None, POOL_HALO, d), lambda b, i: (b, 0, 0))],
        out_shape=[jax.ShapeDtypeStruct((n, t, d), F32), jax.ShapeDtypeStruct((n, POOL_HALO, d), F32)],
        scratch_shapes=[pltpu.VMEM((d // LANES, POOL_HALO + tt, LANES), F32), pltpu.VMEM((d // LANES, tt, LANES), F32)],
        compiler_params=_params(2),
        name="pool_ffn_prompt",
    )(x, norm_g, w_pool, pool_scale, norm_ffn, w_gate, w_up, w_down)


def _grown_state(prev, slab_dims, tile_axis, nb):
    layers = 0 if prev is None else prev.shape[0]

    def spec(n_layers):
        block = list(slab_dims)
        block[tile_axis] = nb

        def index(b):
            idx = [0] * (len(slab_dims) + 1)
            idx[tile_axis + 1] = b
            return tuple(idx)

        return pl.BlockSpec((n_layers, *block), index)

    ins, in_specs = ([], []) if prev is None else ([prev], [spec(layers)])
    return ins, in_specs, spec(layers + 1), jax.ShapeDtypeStruct((layers + 1, *slab_dims), F32)


def _copy_earlier_layers(new_ref, prev_refs):
    for prev_ref in prev_refs:
        for l in range(prev_ref.shape[0]):
            new_ref[l] = prev_ref[l]


def _inproj_conv_sample_body(x_ref, gm_ref, win_ref, qn_ref, kn_ref, cache_ref, w_ref, b_ref, g_ref, beta_ref, *refs):
    o_ref, q_ref, k_ref, v_ref, all_new_ref = refs[-5:]
    _copy_earlier_layers(all_new_ref, refs[:-5])
    this_layer = all_new_ref.shape[0] - 1
    hist = cache_ref.shape[0]
    t, nb, d = x_ref.shape
    h = _rmsnorm(x_ref[...].reshape(t * nb, d), gm_ref[...]).astype(BF16)
    o_q = 2 * D_CONV
    ua = jnp.dot(h, win_ref[:, o_q:], preferred_element_type=F32)
    uc = jnp.dot(h, win_ref[:, :o_q], preferred_element_type=F32)
    q = (_head_rmsnorm(ua[:, :D_ATTN], qn_ref[...]) * (HEAD_DIM ** -0.5)).astype(BF16)
    q_ref[...] = q.reshape(t, nb, D_ATTN)
    k_ref[...] = _head_rmsnorm(ua[:, D_ATTN:D_ATTN + D_KV], kn_ref[...]).reshape(t, nb, D_KV)
    v_ref[...] = ua[:, D_ATTN + D_KV:].reshape(t, nb, D_KV)
    glu = uc[:, :D_CONV] * jax.nn.sigmoid(uc[:, D_CONV:])
    ext = lambda j: cache_ref[j] if j < hist else glu[(j - hist) * nb:(j - hist + 1) * nb]
    rounded = lambda a: a.astype(BF16).astype(F32)
    for j in range(hist):
        all_new_ref[this_layer, j] = ext(t + j)
    for step in range(t):
        acc = rounded(w_ref[0:1, :]) * rounded(ext(step)) + b_ref[...]
        for k in range(1, CONV_WIDTH):
            acc = acc + rounded(w_ref[k:k + 1, :]) * rounded(ext(step + k))
        o_ref[step] = _layernorm_silu(acc, g_ref[...], beta_ref[...]).astype(o_ref.dtype)


def _inproj_conv_sample(x_tm, norm_g, w_in, q_gain, k_gain, cache_tm, w_dw, b_dw, ln_g, ln_b, layer, i_even, prev_new):
    _, hist, n, c = cache_tm.shape
    t, _, d = x_tm.shape
    n_in = w_in.shape[2]
    nb = SAMPLE_SEQ_TILE
    even = lambda width: pl.BlockSpec((None, 1, width), lambda b: (i_even, 0, 0))
    step = lambda width: pl.BlockSpec((t, nb, width), lambda b: (0, b, 0))
    prev_in, prev_specs, new_spec, new_shape = _grown_state(prev_new, (hist, n, c), 1, nb)
    return pl.pallas_call(
        _inproj_conv_sample_body,
        grid=(n // nb,),
        in_specs=[step(d),
                  pl.BlockSpec((None, 1, d), lambda b: (layer, 0, 0)),
                  pl.BlockSpec((None, d, n_in), lambda b: (i_even, 0, 0), pipeline_mode=pl.Buffered(1)),
                  even(D_ATTN), even(D_KV),
                  pl.BlockSpec((None, hist, nb, c), lambda b: (i_even, 0, b, 0)),
                  pl.BlockSpec((None, CONV_WIDTH, c), lambda b: (i_even, 0, 0)),
                  even(c), even(c), even(c)] + prev_specs,
        out_specs=[step(c), step(D_ATTN), step(D_KV), step(D_KV), new_spec],
        out_shape=[jax.ShapeDtypeStruct((t, n, c), BF16), jax.ShapeDtypeStruct((t, n, D_ATTN), BF16),
                   jax.ShapeDtypeStruct((t, n, D_KV), F32), jax.ShapeDtypeStruct((t, n, D_KV), F32), new_shape],
        compiler_params=_params(1),
        name="inproj_conv_sample",
    )(x_tm, norm_g, w_in, q_gain, k_gain, cache_tm, w_dw, b_dw, ln_g, ln_b, *prev_in)


def _attn_sample_body(sink_ref, q_ref, kc_ref, kn_ref, vc_ref, vn_ref, *refs):
    o_ref, all_knew_ref, all_vnew_ref = refs[-3:]
    _copy_earlier_layers(all_knew_ref, refs[:-3][0:1])
    _copy_earlier_layers(all_vnew_ref, refs[:-3][1:2])
    knew_ref = all_knew_ref.at[all_knew_ref.shape[0] - 1]
    vnew_ref = all_vnew_ref.at[all_vnew_ref.shape[0] - 1]
    nb, nkv, rows, hd = q_ref.shape
    buf = kc_ref.shape[-1]
    t = kn_ref.shape[-1]
    batch = nb * nkv
    q = q_ref[...].reshape(batch, rows, hd)
    kc = kc_ref[...].reshape(batch, hd, buf)
    kn = kn_ref[...].reshape(batch, hd, t)
    vc = vc_ref[...].reshape(batch, hd, buf)
    vn = vn_ref[...].reshape(batch, hd, t)

    lane = lax.broadcasted_iota(jnp.int32, (1, 1, buf), 2)
    pad = jnp.zeros((batch, hd, buf - t), F32)
    k_new = jnp.where(lane < buf - t, pltpu.roll(kc, buf - t, 2), jnp.concatenate([pad, kn], axis=-1))
    v_new = jnp.where(lane < buf - t, pltpu.roll(vc, buf - t, 2), jnp.concatenate([pad, vn], axis=-1))
    knew_ref[...] = k_new.reshape(nb, nkv, hd, buf)
    vnew_ref[...] = v_new.reshape(nb, nkv, hd, buf)

    s_c = jnp.einsum("bqd,bdk->bqk", q, kc.astype(BF16), preferred_element_type=F32)
    s_n = jnp.einsum("bqd,bdk->bqk", q, kn.astype(BF16), preferred_element_type=F32)
    step = lax.broadcasted_iota(jnp.int32, (rows, 1), 0) % t
    head_in_group = lax.broadcasted_iota(jnp.int32, (rows, 1), 0) // t
    dist_c = buf + step - lax.broadcasted_iota(jnp.int32, (rows, buf), 1)
    dist_n = step - lax.broadcasted_iota(jnp.int32, (rows, t), 1)
    s_c = s_c.reshape(nb, nkv, rows, buf)
    s_n = s_n.reshape(nb, nkv, rows, t)
    ps_c, ps_n, invs = [], [], []
    for kv in range(nkv):
        slope = jnp.zeros((rows, 1), F32)
        sink = jnp.zeros((rows, 1), F32)
        for j in range(GQA_GROUP):
            slope = jnp.where(head_in_group == j, ALIBI_SLOPES[kv * GQA_GROUP + j], slope)
            sink = jnp.where(head_in_group == j, sink_ref[kv * GQA_GROUP + j], sink)
        sc = jnp.where((dist_c >= 0) & (dist_c < WINDOW), s_c[:, kv] - slope * dist_c.astype(F32), NEG_INF)
        sn = jnp.where((dist_n >= 0) & (dist_n < WINDOW), s_n[:, kv] - slope * dist_n.astype(F32), NEG_INF)
        m = jnp.maximum(jnp.maximum(jnp.max(sc, axis=-1, keepdims=True), jnp.max(sn, axis=-1, keepdims=True)), sink)
        pc = jnp.exp(sc - m)
        pn = jnp.exp(sn - m)
        denom = jnp.sum(pc, axis=-1, keepdims=True) + jnp.sum(pn, axis=-1, keepdims=True) + jnp.exp(sink - m)
        ps_c.append(pc.astype(BF16))
        ps_n.append(pn.astype(BF16))
        invs.append(1.0 / denom)
    p_c = jnp.stack(ps_c, axis=1).reshape(batch, rows, buf)
    p_n = jnp.stack(ps_n, axis=1).reshape(batch, rows, t)
    inv = jnp.stack(invs, axis=1).reshape(batch, rows, 1)
    o = jnp.einsum("bqk,bdk->bqd", p_c, vc.astype(BF16), preferred_element_type=F32)
    o = o + jnp.einsum("bqk,bdk->bqd", p_n, vn.astype(BF16), preferred_element_type=F32)
    o_ref[...] = (o * inv).reshape(nb, nkv, rows, hd).astype(o_ref.dtype)


def _attn_sample(q, kc, kn, vc, vn, sinks, i_even, prev_new):
    n, nkv, rows, hd = q.shape
    buf = kc.shape[-1]
    t = kn.shape[-1]
    nb = SAMPLE_SEQ_TILE
    cache = pl.BlockSpec((None, nb, nkv, hd, buf), lambda b: (i_even, b, 0, 0, 0))
    new = pl.BlockSpec((nb, nkv, hd, t), lambda b: (b, 0, 0, 0))
    qspec = pl.BlockSpec((nb, nkv, rows, hd), lambda b: (b, 0, 0, 0))
    prev_k, prev_v = (None, None) if prev_new is None else prev_new
    k_in, k_specs, k_spec, k_shape = _grown_state(prev_k, (n, nkv, hd, buf), 0, nb)
    v_in, v_specs, v_spec, v_shape = _grown_state(prev_v, (n, nkv, hd, buf), 0, nb)
    return pl.pallas_call(
        _attn_sample_body,
        grid=(n // nb,),
        in_specs=[pl.BlockSpec(memory_space=pltpu.SMEM), qspec, cache, new, cache, new] + k_specs + v_specs,
        out_specs=[qspec, k_spec, v_spec],
        out_shape=[jax.ShapeDtypeStruct((n, nkv, rows, hd), BF16), k_shape, v_shape],
        compiler_params=_params(1),
        name="attn_sample",
    )(sinks, q, kc, kn, vc, vn, *k_in, *v_in)


def _pool_ffn_sample_body(x_ref, st_ref, g_ref, w_ref, sc_ref, gf_ref, wg_ref, wu_ref, wd_ref, *refs, pos0):
    o_ref, all_new_ref = refs[-2:]
    _copy_earlier_layers(all_new_ref, refs[:-2])
    new_ref = all_new_ref.at[all_new_ref.shape[0] - 1]
    t, nb, d = x_ref.shape
    hist = st_ref.shape[0]
    gdim = d // len(POOL_WINDOWS)
    x = x_ref[...].reshape(t * nb, d)
    xn = _rmsnorm(x, g_ref[...])
    ext = lambda j: st_ref[j] if j < hist else xn[(j - hist) * nb:(j - hist + 1) * nb]
    for j in range(hist):
        new_ref[j] = ext(t + j)
    ys = []
    for g, w in enumerate(POOL_WINDOWS):
        lanes = slice(g * gdim, (g + 1) * gdim)
        steps = []
        for step in range(t):
            acc = ext(hist + step)[:, lanes]
            for j in range(1, w):
                acc = acc + ext(hist + step - j)[:, lanes]
            cnt = float(min(w, pos0 + step + 1))
            steps.append(acc / cnt - xn[step * nb:(step + 1) * nb, lanes])
        dg = jnp.concatenate(steps, axis=0)
        ys.append(jnp.dot(dg.astype(BF16), w_ref[g], preferred_element_type=F32))
    x1 = x + jnp.concatenate(ys, axis=-1) * sc_ref[...]
    o_ref[...] = _ffn_apply(x1, gf_ref, wg_ref, wu_ref, wd_ref).reshape(t, nb, d)


def _pool_ffn_sample(x_tm, state_tm, norm_g, w_pool, pool_scale, layer, i_odd, pos0, norm_ffn, w_gate, w_up, w_down,
                     prev_new):
    t, n, d = x_tm.shape
    hist = state_tm.shape[1]
    nb = SAMPLE_FFN_SEQ_TILE
    ng, gd, _ = w_pool.shape[1:]
    prev_in, prev_specs, new_spec, new_shape = _grown_state(prev_new, (hist, n, d), 1, nb)
    return pl.pallas_call(
        functools.partial(_pool_ffn_sample_body, pos0=pos0),
        grid=(n // nb,),
        in_specs=[pl.BlockSpec((t, nb, d), lambda b: (0, b, 0)),
                  pl.BlockSpec((None, hist, nb, d), lambda b: (i_odd, 0, b, 0)),
                  pl.BlockSpec((None, 1, d), lambda b: (layer, 0, 0)),
                  pl.BlockSpec((None, ng, gd, gd), lambda b: (i_odd, 0, 0, 0)),
                  pl.BlockSpec((None, 1, d), lambda b: (i_odd, 0, 0))] + _ffn_specs(w_gate, layer, 1) + prev_specs,
        out_specs=[pl.BlockSpec((t, nb, d), lambda b: (0, b, 0)), new_spec],
        out_shape=[jax.ShapeDtypeStruct((t, n, d), F32), new_shape],
        compiler_params=_params(1),
        name="pool_ffn_sample",
    )(x_tm, state_tm, norm_g, w_pool, pool_scale, norm_ffn, w_gate, w_up, w_down, *prev_in)


def _prepare_weights(norm_mix, norm_ffn, w_in, q_norm, k_norm, sinks, w_dw, b_dw, conv_norm_g, conv_norm_b, w_out,
                     w_pool, pool_scale, w_gate, w_up, w_down):
    vec = lambda a: a[:, None, :]
    return dict(
        norm_mix=vec(norm_mix), norm_ffn=vec(norm_ffn), w_in=w_in.astype(BF16),
        q_gain=vec(jnp.tile(q_norm, (1, N_HEADS))), k_gain=vec(jnp.tile(k_norm, (1, N_KV_HEADS))), sinks=sinks,
        w_dw=w_dw, b_dw=vec(b_dw),
        w_taps=jnp.broadcast_to(w_dw.astype(BF16)[:, :, None, :], w_dw.shape[:2] + (16, w_dw.shape[2])),
        ln_g=vec(conv_norm_g), ln_b=vec(conv_norm_b), w_out=w_out.astype(BF16),
        w_pool=w_pool.astype(BF16), pool_scale=vec(pool_scale),
        w_gate=w_gate.astype(BF16), w_up=w_up.astype(BF16), w_down=w_down.astype(BF16))


def _trunks(x_prompt, x_sample, conv_bufs, k_bufs, v_bufs, pool_bufs, pos0, p, depth):
    n, t, d = x_prompt.shape
    ns, ts, _ = x_sample.shape
    xp = x_prompt.reshape(n * t, d)
    x_tm = x_sample.transpose(1, 0, 2)
    conv_tm = conv_bufs.transpose(0, 2, 1, 3)
    pool_tm = pool_bufs.transpose(0, 2, 1, 3)
    k_t = k_bufs.transpose(0, 1, 3, 4, 2)
    v_t = v_bufs.transpose(0, 1, 3, 4, 2)
    ffn = lambda layer: (p["norm_ffn"], p["w_gate"], p["w_up"], p["w_down"], layer)
    new_conv, new_k, new_v, new_pool = [], [], [], []
    conv_new = kv_new = pool_new = None
    for layer in range(depth):
        i = layer // 2
        if layer % 2 == 0:
            conv_p, q, k, v, glu_tail = _inproj_conv_prompt(
                xp.reshape(n, t, d), p["norm_mix"], p["w_in"], p["q_gain"], p["k_gain"], p["w_taps"], p["b_dw"],
                p["ln_g"], p["ln_b"], layer, i)
            attn_p = _attn_prompt(q, k, v, p["sinks"][i])
            keep = min(WINDOW, t)
            new_conv.append(glu_tail[:, CONV_HALO - (CONV_WIDTH - 1):])
            new_k.append(k[:, t - keep:].reshape(n, keep, N_KV_HEADS, HEAD_DIM))
            new_v.append(v[:, t - keep:].reshape(n, keep, N_KV_HEADS, HEAD_DIM))

            xs = x_tm.reshape(ts * ns, d)
            conv_s, q, k, v, conv_new = _inproj_conv_sample(
                x_tm, p["norm_mix"], p["w_in"], p["q_gain"], p["k_gain"], conv_tm, p["w_dw"], p["b_dw"], p["ln_g"],
                p["ln_b"], layer, i, conv_new)
            qh = q.reshape(ts, ns, N_KV_HEADS, GQA_GROUP, HEAD_DIM).transpose(1, 2, 3, 0, 4)
            qh = qh.reshape(ns, N_KV_HEADS, GQA_GROUP * ts, HEAD_DIM)
            kn = k.reshape(ts, ns, N_KV_HEADS, HEAD_DIM).transpose(1, 2, 3, 0)
            vn = v.reshape(ts, ns, N_KV_HEADS, HEAD_DIM).transpose(1, 2, 3, 0)
            oh, *kv_new = _attn_sample(qh, k_t, kn, v_t, vn, p["sinks"][i], i, kv_new)
            attn_s = oh.reshape(ns, N_KV_HEADS, GQA_GROUP, ts, HEAD_DIM).transpose(3, 0, 1, 2, 4)

            xp, xs = _outproj_ffn((xp, conv_p.reshape(n * t, -1), attn_p.reshape(n * t, -1)),
                                  (xs, conv_s.reshape(ts * ns, D_CONV), attn_s.reshape(ts * ns, D_ATTN)),
                                  p["w_out"], i, *ffn(layer))
            x_tm = xs.reshape(ts, ns, d)
        else:
            y, tail = _pool_ffn_prompt(xp.reshape(n, t, d), p["norm_mix"], p["w_pool"], p["pool_scale"], layer, i,
                                       *ffn(layer)[:-1])
            xp = y.reshape(n * t, d)
            new_pool.append(tail[:, POOL_HALO - POOL_BUF:])
            x_tm, pool_new = _pool_ffn_sample(x_tm, pool_tm, p["norm_mix"], p["w_pool"], p["pool_scale"], layer, i, pos0,
                                              *ffn(layer)[:-1], pool_new)
    return (xp.reshape(n, t, d), x_tm.transpose(1, 0, 2),
            jnp.stack(new_conv), jnp.stack(new_k), jnp.stack(new_v), jnp.stack(new_pool),
            conv_new.transpose(0, 2, 1, 3), kv_new[0].transpose(0, 1, 4, 2, 3), kv_new[1].transpose(0, 1, 4, 2, 3),
            pool_new.transpose(0, 2, 1, 3))


def kernel(x_prompt, x_sample, cache_conv, cache_k, cache_v, state_pool, norm_mix, norm_ffn, w_in, q_norm, k_norm,
           sinks, w_dw, b_dw, conv_norm_g, conv_norm_b, w_out, w_pool, pool_scale, w_gate, w_up, w_down):
    depth = norm_mix.shape[0]
    p = _prepare_weights(norm_mix, norm_ffn, w_in, q_norm, k_norm, sinks, w_dw, b_dw, conv_norm_g, conv_norm_b, w_out,
                         w_pool, pool_scale, w_gate, w_up, w_down)
    return _trunks(x_prompt, x_sample, cache_conv, cache_k, cache_v, state_pool, PAST_LEN, p, depth)
```
